```python
import math
import jax, jax.numpy as jnp
from jax import lax
import numpy as np

D_MODEL = 2048
BATCH = 4
SEQ = 4096
DEPTH = 1

NSA_HEADS = 16
NSA_KV_HEADS = 4
NSA_GROUP = NSA_HEADS // NSA_KV_HEADS
HEAD_DIM = 64
NSA_WIDTH = NSA_HEADS * HEAD_DIM
KV_WIDTH = NSA_KV_HEADS * HEAD_DIM
CMP_LEN = 32
CMP_STRIDE = 16
CMP_HIDDEN = 256
SEL_BLOCK = 64
SEL_TOP = 16
SEL_Q_BLOCK = 64
WINDOW = 512
WIN_Q_BLOCK = 128
S5_WIDTH = 1024
S5_GROUP = 16
S5_GROUPS = S5_WIDTH // S5_GROUP
S5_STATE = 64
DT_MIN = 1e-3
DT_MAX = 1e-1
REL_BUCKETS = 32
REL_MAX_DIST = 128
PEER_KEYS = 128
PEER_EXPERTS = PEER_KEYS * PEER_KEYS
PEER_HEADS = 8
PEER_TOPK = 16
PEER_QDIM = 256
PEER_CHUNK = 128
N_BRANCHES = 2
IN_WIDTH = NSA_WIDTH + 6 * KV_WIDTH + 3 * NSA_HEADS + S5_WIDTH + N_BRANCHES * D_MODEL
EPS = 1e-6
NEG_INF = -1e30
FORCE_BONUS = 1e4

kernel_name = "hybrid_nsa_s5_peer_block"


def _split_points():
    widths = [NSA_WIDTH] + [KV_WIDTH] * 6 + [3 * NSA_HEADS, S5_WIDTH, N_BRANCHES * D_MODEL]
    pts, acc = [], 0
    for w in widths[:-1]:
        acc += w
        pts.append(acc)
    return pts


def rms_norm(x, g):
    xf = x.astype(jnp.float32)
    xf = xf * lax.rsqrt(jnp.mean(xf * xf, axis=-1, keepdims=True) + EPS)
    return xf.astype(x.dtype) * g


def t5_bucket(dist):
    dist = jnp.maximum(dist, 0)
    max_exact = REL_BUCKETS // 2
    log_ratio = jnp.log(jnp.maximum(dist, 1).astype(jnp.float32) / max_exact) / math.log(REL_MAX_DIST / max_exact)
    large = jnp.minimum(max_exact + (log_ratio * (REL_BUCKETS - max_exact)).astype(jnp.int32), REL_BUCKETS - 1)
    return jnp.where(dist < max_exact, dist, large)


def compress_kv(k, pos, w1, b1, w2, b2):
    b, l = k.shape[:2]
    kb = k.reshape(b, l // CMP_STRIDE, CMP_STRIDE, NSA_KV_HEADS, HEAD_DIM)
    blocks = jnp.concatenate([kb[:, :-1], kb[:, 1:]], axis=2)
    blocks = blocks + pos[None, None, :, None, :]
    flat = blocks.transpose(0, 1, 3, 2, 4).reshape(b, -1, NSA_KV_HEADS, CMP_LEN * HEAD_DIM)
    return jax.nn.gelu(flat @ w1 + b1) @ w2 + b2


def cmp_to_sel_matrix(n_cmp, n_sel):
    cs = np.arange(n_cmp)[:, None] * CMP_STRIDE
    ss = np.arange(n_sel)[None, :] * SEL_BLOCK
    ov = np.clip(np.minimum(cs + CMP_LEN, ss + SEL_BLOCK) - np.maximum(cs, ss), 0, None)
    return jnp.asarray(ov / CMP_LEN, dtype=jnp.float32)


def nsa(q, kc, vc, ks, vs, kw, vw, gate_logits, rel_bias):
    b, l = q.shape[:2]
    scale = HEAD_DIM ** -0.5
    qg = q.reshape(b, l, NSA_KV_HEADS, NSA_GROUP, HEAD_DIM)
    t = jnp.arange(l)

    n_cmp = kc.shape[1]
    cmp_end = jnp.arange(n_cmp) * CMP_STRIDE + CMP_LEN - 1
    dist_c = t[:, None] - cmp_end[None, :]
    valid_c = dist_c >= 0
    bias_c = rel_bias[t5_bucket(dist_c)].transpose(2, 0, 1).reshape(NSA_KV_HEADS, NSA_GROUP, l, n_cmp)
    s_c = jnp.einsum('blkgd,bnkd->bkgln', qg, kc).astype(jnp.float32) * scale + bias_c
    p_c = jax.nn.softmax(jnp.where(valid_c, s_c, NEG_INF), axis=-1) * valid_c
    o_c = jnp.einsum('bkgln,bnkd->blkgd', p_c.astype(vc.dtype), vc)

    n_sel = l // SEL_BLOCK
    n_top = min(SEL_TOP, n_sel)
    imp = jnp.einsum('bkgln,nj->bklj', p_c, cmp_to_sel_matrix(n_cmp, n_sel))
    blk = jnp.arange(n_sel)[None, :]
    cur = (t // SEL_BLOCK)[:, None]
    forced = (blk == 0) | (blk == cur) | (blk == cur - 1)
    imp = jnp.where(blk <= cur, imp + FORCE_BONUS * forced, NEG_INF)
    _, sel_idx = lax.top_k(imp, n_top)

    ks_blocks = ks.reshape(b, n_sel, SEL_BLOCK, NSA_KV_HEADS, HEAD_DIM).transpose(0, 3, 1, 2, 4)
    vs_blocks = vs.reshape(b, n_sel, SEL_BLOCK, NSA_KV_HEADS, HEAD_DIM).transpose(0, 3, 1, 2, 4)
    gather = jax.vmap(jax.vmap(lambda blocks, idx: blocks[idx]))
    tbl = rel_bias.reshape(REL_BUCKETS, NSA_KV_HEADS, NSA_GROUP).transpose(1, 0, 2)
    lookup = jax.vmap(lambda tb, bk: tb[bk], in_axes=(0, 1), out_axes=1)

    def sel_block(args):
        q_b, idx_b, t_b = args
        k_g = gather(ks_blocks, idx_b).reshape(b, NSA_KV_HEADS, SEL_Q_BLOCK, -1, HEAD_DIM)
        v_g = gather(vs_blocks, idx_b).reshape(b, NSA_KV_HEADS, SEL_Q_BLOCK, -1, HEAD_DIM)
        pos = (idx_b[..., None] * SEL_BLOCK + jnp.arange(SEL_BLOCK)).reshape(b, NSA_KV_HEADS, SEL_Q_BLOCK, -1)
        dist = t_b[:, None] - pos
        bias = lookup(tbl, t5_bucket(dist)).transpose(0, 1, 4, 2, 3)
        s = jnp.einsum('btkgd,bktsd->bkgts', q_b, k_g).astype(jnp.float32) * scale + bias
        s = jnp.where((dist >= 0)[:, :, None], s, NEG_INF)
        p = jax.nn.softmax(s, axis=-1).astype(v_g.dtype)
        return jnp.einsum('bkgts,bktsd->btkgd', p, v_g)

    n_qb = l // SEL_Q_BLOCK
    q_blocks = qg.reshape(b, n_qb, SEL_Q_BLOCK, NSA_KV_HEADS, NSA_GROUP, HEAD_DIM).swapaxes(0, 1)
    idx_blocks = sel_idx.reshape(b, NSA_KV_HEADS, n_qb, SEL_Q_BLOCK, n_top).transpose(2, 0, 1, 3, 4)
    t_blocks = t.reshape(n_qb, SEL_Q_BLOCK)
    o_s = lax.map(sel_block, (q_blocks, idx_blocks, t_blocks))
    o_s = o_s.swapaxes(0, 1).reshape(b, l, NSA_KV_HEADS, NSA_GROUP, HEAD_DIM)

    n_wb = l // WIN_Q_BLOCK
    nw = WINDOW // WIN_Q_BLOCK

    def band(k):
        kb = k.reshape(b, n_wb, WIN_Q_BLOCK, NSA_KV_HEADS, HEAD_DIM)
        kp = jnp.pad(kb, ((0, 0), (nw, 0), (0, 0), (0, 0), (0, 0)))
        return jnp.concatenate([kp[:, i:i + n_wb] for i in range(nw + 1)], axis=2)

    k_band, v_band = band(kw), band(vw)
    qi = jnp.arange(WIN_Q_BLOCK)
    kj = jnp.arange((nw + 1) * WIN_Q_BLOCK)
    dist_w = qi[:, None] + nw * WIN_Q_BLOCK - kj[None, :]
    in_win = (dist_w >= 0) & (dist_w < WINDOW)
    kpos = jnp.arange(n_wb)[:, None] * WIN_Q_BLOCK + kj[None, :] - nw * WIN_Q_BLOCK
    mask_w = in_win[None] & (kpos >= 0)[:, None, :]
    bias_w = rel_bias[t5_bucket(dist_w)].transpose(2, 0, 1).reshape(NSA_KV_HEADS, NSA_GROUP, WIN_Q_BLOCK, -1)
    qw = qg.reshape(b, n_wb, WIN_Q_BLOCK, NSA_KV_HEADS, NSA_GROUP, HEAD_DIM)
    s_w = jnp.einsum('bnqkgd,bnskd->bnkgqs', qw, k_band).astype(jnp.float32) * scale + bias_w
    s_w = jnp.where(mask_w[None, :, None, None], s_w, NEG_INF)
    p_w = jax.nn.softmax(s_w, axis=-1).astype(v_band.dtype)
    o_w = jnp.einsum('bnkgqs,bnskd->bnqkgd', p_w, v_band).reshape(b, l, NSA_KV_HEADS, NSA_GROUP, HEAD_DIM)

    g = jax.nn.sigmoid(gate_logits).reshape(b, l, NSA_KV_HEADS, NSA_GROUP, 3, 1)
    o = g[..., 0, :] * o_c + g[..., 1, :] * o_s + g[..., 2, :] * o_w
    return o.reshape(b, l, NSA_WIDTH)


def s5(u, lam_re, lam_im, log_step, b_re, b_im, c_re, c_im, d_skip):
    b, l, _ = u.shape
    f32 = jnp.float32
    ug = u.astype(f32).reshape(b, l, S5_GROUPS, S5_GROUP).transpose(1, 0, 2, 3)
    lam_re, lam_im = lam_re.astype(f32), lam_im.astype(f32)
    step = jnp.exp(log_step.astype(f32))[:, None]
    mag = jnp.exp(lam_re * step)
    lb_re, lb_im = mag * jnp.cos(lam_im * step), mag * jnp.sin(lam_im * step)
    den = lam_re * lam_re + lam_im * lam_im
    n_re = lb_re - 1.0
    f_re = (n_re * lam_re + lb_im * lam_im) / den
    f_im = (lb_im * lam_re - n_re * lam_im) / den
    b_re, b_im = b_re.astype(f32), b_im.astype(f32)
    bb_re = f_re[..., None] * b_re - f_im[..., None] * b_im
    bb_im = f_re[..., None] * b_im + f_im[..., None] * b_re
    bu_re = jnp.einsum('lbgh,gph->lbgp', ug, bb_re)
    bu_im = jnp.einsum('lbgh,gph->lbgp', ug, bb_im)
    a_re = jnp.broadcast_to(lb_re, bu_re.shape)
    a_im = jnp.broadcast_to(lb_im, bu_re.shape)

    def combine(e1, e2):
        a1r, a1i, b1r, b1i = e1
        a2r, a2i, b2r, b2i = e2
        return (a2r * a1r - a2i * a1i, a2r * a1i + a2i * a1r,
                a2r * b1r - a2i * b1i + b2r, a2r * b1i + a2i * b1r + b2i)

    _, _, x_re, x_im = lax.associative_scan(combine, (a_re, a_im, bu_re, bu_im), axis=0)
    y = jnp.einsum('lbgp,ghp->lbgh', x_re, c_re.astype(f32)) - jnp.einsum('lbgp,ghp->lbgh', x_im, c_im.astype(f32))
    y = y + d_skip.astype(f32).reshape(S5_GROUPS, S5_GROUP) * ug
    return y.transpose(1, 0, 2, 3).reshape(b, l, S5_WIDTH).astype(u.dtype)


def peer(h, w_q, sub_keys, u_tab, v_tab):
    b, l, d = h.shape
    n_tok = b * l
    hf = h.reshape(n_tok, d)
    q = (hf @ w_q).reshape(n_tok, PEER_HEADS, 2, PEER_QDIM // 2)
    s = jnp.einsum('thcd,hcnd->thcn', q, sub_keys).astype(jnp.float32)
    v1, i1 = lax.top_k(s[:, :, 0], PEER_TOPK)
    v2, i2 = lax.top_k(s[:, :, 1], PEER_TOPK)
    cand_s = (v1[..., :, None] + v2[..., None, :]).reshape(n_tok, PEER_HEADS, -1)
    cand_i = (i1[..., :, None] * PEER_KEYS + i2[..., None, :]).reshape(n_tok, PEER_HEADS, -1)
    top_s, top_p = lax.top_k(cand_s, PEER_TOPK)
    expert = jnp.take_along_axis(cand_i, top_p, axis=-1)
    gate = jax.nn.softmax(top_s, axis=-1)
    n_ch = n_tok // PEER_CHUNK

    def chunk(args):
        x_c, e_c, g_c = args
        a = jnp.einsum('td,thkd->thk', x_c, u_tab[e_c]).astype(jnp.float32)
        w = (g_c * jax.nn.gelu(a)).astype(x_c.dtype)
        return jnp.einsum('thk,thkd->td', w, v_tab[e_c])

    out = lax.map(chunk, (hf.reshape(n_ch, PEER_CHUNK, d),
                          expert.reshape(n_ch, PEER_CHUNK, PEER_HEADS, PEER_TOPK),
                          gate.reshape(n_ch, PEER_CHUNK, PEER_HEADS, PEER_TOPK)))
    return out.reshape(b, l, d)


def setup_inputs(seed: int = 0) -> dict:
    key = jax.random.key(seed)
    ks = jax.random.split(key, 40)
    f32 = jnp.float32

    def nrm(k, shape, s):
        return jax.random.normal(k, shape, f32) * s

    L = DEPTH
    lam_im = jnp.broadcast_to(math.pi * jnp.arange(S5_STATE, dtype=f32), (L, S5_GROUPS, S5_STATE))
    return {
        "x": nrm(ks[0], (BATCH, SEQ, D_MODEL), 1.0),
        "c": nrm(ks[1], (BATCH, D_MODEL), 1.0),
        "ada_w": nrm(ks[2], (L, D_MODEL, 6 * D_MODEL), 0.5 * D_MODEL ** -0.5),
        "ada_b": nrm(ks[3], (L, 6 * D_MODEL), 0.02),
        "norm_mix_g": 1.0 + nrm(ks[4], (L, D_MODEL), 0.02),
        "norm_ffn_g": 1.0 + nrm(ks[5], (L, D_MODEL), 0.02),
        "w_in": nrm(ks[6], (L, D_MODEL, IN_WIDTH), D_MODEL ** -0.5),
        "cmp_pos": nrm(ks[7], (L, 2, CMP_LEN, HEAD_DIM), 0.1),
        "cmp_w1": nrm(ks[8], (L, 2, CMP_LEN * HEAD_DIM, CMP_HIDDEN), (CMP_LEN * HEAD_DIM) ** -0.5),
        "cmp_b1": nrm(ks[9], (L, 2, CMP_HIDDEN), 0.02),
        "cmp_w2": nrm(ks[10], (L, 2, CMP_HIDDEN, HEAD_DIM), CMP_HIDDEN ** -0.5),
        "cmp_b2": nrm(ks[11], (L, 2, HEAD_DIM), 0.02),
        "rel_bias": nrm(ks[12], (REL_BUCKETS, NSA_HEADS), 0.3),
        "s5_lam_re": -0.5 + nrm(ks[13], (L, S5_GROUPS, S5_STATE), 0.01),
        "s5_lam_im": lam_im + nrm(ks[14], (L, S5_GROUPS, S5_STATE), 0.01),
        "s5_log_step": jax.random.uniform(ks[15], (L, S5_GROUPS), f32, math.log(DT_MIN), math.log(DT_MAX)),
        "s5_b_re": nrm(ks[16], (L, S5_GROUPS, S5_STATE, S5_GROUP), (2 * S5_GROUP) ** -0.5),
        "s5_b_im": nrm(ks[17], (L, S5_GROUPS, S5_STATE, S5_GROUP), (2 * S5_GROUP) ** -0.5),
        "s5_c_re": nrm(ks[18], (L, S5_GROUPS, S5_GROUP, S5_STATE), 0.5),
        "s5_c_im": nrm(ks[19], (L, S5_GROUPS, S5_GROUP, S5_STATE), 0.5),
        "s5_d": nrm(ks[20], (L, S5_WIDTH), 1.0),
        "glu_w": nrm(ks[21], (L, S5_WIDTH, S5_WIDTH), S5_WIDTH ** -0.5),
        "glu_b": nrm(ks[22], (L, S5_WIDTH), 0.02),
        "w_up_attn": nrm(ks[23], (L, NSA_WIDTH, D_MODEL), NSA_WIDTH ** -0.5),
        "w_up_ssm": nrm(ks[24], (L, S5_WIDTH, D_MODEL), S5_WIDTH ** -0.5),
        "w_out": nrm(ks[25], (L, D_MODEL, D_MODEL), D_MODEL ** -0.5),
        "peer_w_q": nrm(ks[26], (L, D_MODEL, PEER_HEADS * PEER_QDIM), D_MODEL ** -0.5),
        "peer_sub_keys": nrm(ks[27], (L, PEER_HEADS, 2, PEER_KEYS, PEER_QDIM // 2), (PEER_QDIM // 2) ** -0.5),
        "peer_u": nrm(ks[28], (L, PEER_EXPERTS, D_MODEL), D_MODEL ** -0.5),
        "peer_v": nrm(ks[29], (L, PEER_EXPERTS, D_MODEL), 0.5),
        "final_g": 1.0 + nrm(ks[30], (D_MODEL,), 0.02),
    }


def reference(x, c, ada_w, ada_b, norm_mix_g, norm_ffn_g, w_in, cmp_pos, cmp_w1, cmp_b1, cmp_w2, cmp_b2,
              rel_bias, s5_lam_re, s5_lam_im, s5_log_step, s5_b_re, s5_b_im, s5_c_re, s5_c_im, s5_d,
              glu_w, glu_b, w_up_attn, w_up_ssm, w_out, peer_w_q, peer_sub_keys, peer_u, peer_v, final_g):
    b, l, _ = x.shape
    points = _split_points()
    for i in range(DEPTH):
        mod = jax.nn.silu(c) @ ada_w[i] + ada_b[i]
        sh1, sc1, g1, sh2, sc2, g2 = jnp.split(mod[:, None, :], 6, axis=-1)

        h = rms_norm(x, norm_mix_g[i]) * (1.0 + sc1) + sh1
        proj = h @ w_in[i]
        q, kc, vc, ks_, vs_, kw, vw, nsa_gates, u, merge_g = jnp.split(proj, points, axis=-1)
        kv = lambda t_: t_.reshape(b, l, NSA_KV_HEADS, HEAD_DIM)
        kc_c = compress_kv(kv(kc), cmp_pos[i, 0], cmp_w1[i, 0], cmp_b1[i, 0], cmp_w2[i, 0], cmp_b2[i, 0])
        vc_c = compress_kv(kv(vc), cmp_pos[i, 1], cmp_w1[i, 1], cmp_b1[i, 1], cmp_w2[i, 1], cmp_b2[i, 1])
        o_attn = nsa(q.reshape(b, l, NSA_HEADS, HEAD_DIM), kc_c, vc_c, kv(ks_), kv(vs_), kv(kw), kv(vw),
                     nsa_gates, rel_bias)

        y = s5(u, s5_lam_re[i], s5_lam_im[i], s5_log_step[i], s5_b_re[i], s5_b_im[i],
               s5_c_re[i], s5_c_im[i], s5_d[i])
        y = jax.nn.gelu(y)
        y = y * jax.nn.sigmoid(y @ glu_w[i] + glu_b[i])

        ga, gb = jnp.split(jax.nn.sigmoid(merge_g), N_BRANCHES, axis=-1)
        mixed = ga * (o_attn @ w_up_attn[i]) + gb * (y @ w_up_ssm[i])
        x = x + g1 * (mixed @ w_out[i])

        h2 = rms_norm(x, norm_ffn_g[i]) * (1.0 + sc2) + sh2
        x = x + g2 * peer(h2, peer_w_q[i], peer_sub_keys[i], peer_u[i], peer_v[i])
    return rms_norm(x, final_g)
```

```python
import functools
import math

import numpy as np
import jax
import jax.numpy as jnp
from jax import lax
from jax.experimental import pallas as pl
from jax.experimental.pallas import tpu as pltpu

F32 = jnp.float32
BF16 = jnp.bfloat16
HIGHEST = lax.Precision.HIGHEST

D_MODEL = 2048
NSA_HEADS = 16
NSA_KV_HEADS = 4
NSA_GROUP = NSA_HEADS // NSA_KV_HEADS
HEAD_DIM = 64
NSA_WIDTH = NSA_HEADS * HEAD_DIM
KV_WIDTH = NSA_KV_HEADS * HEAD_DIM
CMP_LEN = 32
CMP_STRIDE = 16
CMP_HIDDEN = 256
SEL_BLOCK = 64
SEL_TOP = 16
WINDOW = 512
S5_WIDTH = 1024
S5_GROUP = 16
S5_GROUPS = S5_WIDTH // S5_GROUP
S5_STATE = 64
S5_FLAT = S5_GROUPS * S5_STATE
REL_BUCKETS = 32
REL_MAX_DIST = 128
PEER_KEYS = 128
PEER_EXPERTS = PEER_KEYS * PEER_KEYS
PEER_HEADS = 8
PEER_TOPK = 16
PEER_QDIM = 256
EPS = 1e-6
NEG_INF = -1e30
FORCE_BONUS = 1e4

VMEM_LIMIT_BYTES = 56 * 1024 * 1024
LANES = 128
SUBLANES = 8

PROJ_WIDTH = 8192
COL_Q = 0
COL_U = 1024
COL_MERGE = 2048
COL_SEL = 6144
COL_WIN = 6656
COL_CMP = 7168
COL_GATES = 7680

ATT_TILE = 128
BIAS_SPAN = 256


def _params(*sem):
    return pltpu.CompilerParams(dimension_semantics=sem, vmem_limit_bytes=VMEM_LIMIT_BYTES)


def _gelu(x):
    return 0.5 * x * (1.0 + jnp.tanh(math.sqrt(2.0 / math.pi) * (x + 0.044715 * (x * x * x))))


def _sigmoid(x):
    return 1.0 / (1.0 + jnp.exp(-x))


def _adaln_kernel(c_ref, w_ref, b_ref, o_ref):
    c = c_ref[...]
    a = c * _sigmoid(c)
    o_ref[...] = jnp.dot(a, w_ref[...], precision=HIGHEST, preferred_element_type=F32) + b_ref[...]


def adaln(c, w, b):
    bsz, d = c.shape
    n = w.shape[1]
    tn = 1024
    cp = jnp.zeros((SUBLANES, d), F32).at[:bsz].set(c)
    out = pl.pallas_call(
        _adaln_kernel,
        grid=(n // tn,),
        in_specs=[pl.BlockSpec((SUBLANES, d), lambda j: (0, 0)),
                  pl.BlockSpec((d, tn), lambda j: (0, j)),
                  pl.BlockSpec((1, tn), lambda j: (0, j))],
        out_specs=pl.BlockSpec((SUBLANES, tn), lambda j: (0, j)),
        out_shape=jax.ShapeDtypeStruct((SUBLANES, n), F32),
        compiler_params=_params("parallel"),
        name="adaln",
    )(cp, w, b.reshape(1, n))
    return out[:bsz]


def _modulated_norm(x, g, sc, sh):
    ms = jnp.mean(x * x, axis=-1, keepdims=True)
    return (x * lax.rsqrt(ms + EPS)) * g * (1.0 + sc) + sh


def _inproj_kernel(x_ref, g_ref, sc_ref, sh_ref, w_ref, o_ref, h_scr):
    @pl.when(pl.program_id(2) == 0)
    def _():
        h_scr[...] = _modulated_norm(x_ref[0], g_ref[...], sc_ref[0], sh_ref[0]).astype(BF16)

    o_ref[0] = jnp.dot(h_scr[...], w_ref[...], preferred_element_type=F32).astype(BF16)


def inproj(x, g, sc, sh, wcat, tm=1024, tn=1024):
    bsz, l, d = x.shape
    n = wcat.shape[1]
    tm = min(tm, l)
    return pl.pallas_call(
        _inproj_kernel,
        grid=(bsz, l // tm, n // tn),
        in_specs=[pl.BlockSpec((1, tm, d), lambda b, i, j: (b, i, 0)),
                  pl.BlockSpec((1, d), lambda b, i, j: (0, 0)),
                  pl.BlockSpec((1, 1, d), lambda b, i, j: (b, 0, 0)),
                  pl.BlockSpec((1, 1, d), lambda b, i, j: (b, 0, 0)),
                  pl.BlockSpec((d, tn), lambda b, i, j: (0, j))],
        out_specs=pl.BlockSpec((1, tm, tn), lambda b, i, j: (b, i, j)),
        out_shape=jax.ShapeDtypeStruct((bsz, l, n), BF16),
        scratch_shapes=[pltpu.VMEM((tm, d), BF16)],
        compiler_params=_params("parallel", "parallel", "arbitrary"),
        name="inproj",
    )(x, g.reshape(1, d), sc, sh, wcat)


def build_wcat(w_in):
    d = w_in.shape[0]
    pts = np.cumsum([NSA_WIDTH] + [KV_WIDTH] * 6 + [3 * NSA_HEADS, S5_WIDTH])
    q, kc, vc, ks, vs, kw, vw, gates, u, merge = jnp.split(w_in, pts, axis=1)

    def pair(k, v):
        return jnp.stack([k.reshape(d, NSA_KV_HEADS, HEAD_DIM), v.reshape(d, NSA_KV_HEADS, HEAD_DIM)],
                         axis=2).reshape(d, 2 * KV_WIDTH)

    gates_p = jnp.pad(gates, ((0, 0), (0, LANES - 3 * NSA_HEADS)))
    pad = jnp.zeros((d, PROJ_WIDTH - COL_GATES - LANES), w_in.dtype)
    wcat = jnp.concatenate([q * (HEAD_DIM ** -0.5), u, merge, pair(ks, vs), pair(kw, vw),
                            kc, vc, gates_p, pad], axis=1)
    return wcat.astype(BF16)


def _compress_kernel(gk_ref, gv_ref, w1_ref, pos_ref, b1_ref, w2_ref, b2_ref, o_ref):
    half = CMP_STRIDE * HEAD_DIM
    nrow = gk_ref.shape[2]
    out = b2_ref[...]
    for j, g_ref in enumerate((gk_ref, gv_ref)):
        g = g_ref[0, 0]
        w1 = w1_ref[j]
        top = jnp.dot(g, w1[:half].astype(BF16), preferred_element_type=F32)
        bot = jnp.dot(g, w1[half:].astype(BF16), preferred_element_type=F32)
        cst = jnp.dot(pos_ref[j], w1, precision=HIGHEST, preferred_element_type=F32)[0:1] + b1_ref[j]
        hid = top + pltpu.roll(bot, nrow - 1, axis=0) + cst
        out = out + jnp.dot(_gelu(hid).astype(BF16), w2_ref[j].astype(BF16), preferred_element_type=F32)
    o_ref[0, 0] = out.astype(BF16)


def compress(gk, gv, w1, pos, b1, w2, b2):
    bsz, hkv, n16, wd = gk.shape
    posf = jnp.zeros((2, SUBLANES, CMP_LEN * HEAD_DIM), F32).at[:, 0].set(pos.reshape(2, -1))
    w2p = jnp.zeros((2, CMP_HIDDEN, 2 * HEAD_DIM), F32)
    w2p = w2p.at[0, :, :HEAD_DIM].set(w2[0]).at[1, :, HEAD_DIM:].set(w2[1])
    b2p = jnp.concatenate([b2[0], b2[1]]).reshape(1, 2 * HEAD_DIM)
    gspec = pl.BlockSpec((1, 1, n16, wd), lambda b, h: (b, h, 0, 0))
    full = lambda a: pl.BlockSpec(a.shape, lambda b, h: (0,) * a.ndim)
    b1r = b1.reshape(2, 1, CMP_HIDDEN)
    return pl.pallas_call(
        _compress_kernel,
        grid=(bsz, hkv),
        in_specs=[gspec, gspec, full(w1), full(posf), full(b1r), full(w2p), full(b2p)],
        out_specs=pl.BlockSpec((1, 1, n16, 2 * HEAD_DIM), lambda b, h: (b, h, 0, 0)),
        out_shape=jax.ShapeDtypeStruct((bsz, hkv, n16, 2 * HEAD_DIM), BF16),
        compiler_params=_params("parallel", "parallel"),
        name="compress",
    )(gk, gv, w1, posf, b1r, w2p, b2p)


def _t5_bucket_table(n):
    dist = np.arange(n)
    max_exact = REL_BUCKETS // 2
    ratio = np.log(np.maximum(dist, 1).astype(np.float32) / np.float32(max_exact)) / np.float32(
        math.log(REL_MAX_DIST / max_exact))
    large = np.minimum(max_exact + (ratio * np.float32(REL_BUCKETS - max_exact)).astype(np.int32), REL_BUCKETS - 1)
    return np.where(dist < max_exact, dist, large)


def _cmp_to_sel_t(n_cmp, n_sel):
    cs = np.arange(n_cmp)[:, None] * CMP_STRIDE
    ss = np.arange(n_sel)[None, :] * SEL_BLOCK
    ov = np.clip(np.minimum(cs + CMP_LEN, ss + SEL_BLOCK) - np.maximum(cs, ss), 0, None)
    return np.ascontiguousarray((ov / CMP_LEN).T.astype(np.float32))


def attention_tables(rel_bias):
    tq = ATT_TILE
    bucket = _t5_bucket_table(BIAS_SPAN)
    assert np.all(bucket[113:] == REL_BUCKETS - 1)
    bv = (rel_bias[bucket] - rel_bias[REL_BUCKETS - 1][None, :]).T
    i = np.arange(tq)[:, None]
    j = np.arange(tq)[None, :]
    d0 = np.clip(i - j, 0, BIAS_SPAN - 1)
    d1 = tq + i - j
    diag = jnp.stack([bv[:, d0], bv[:, d1]], axis=0)
    diag = diag.reshape(2, NSA_KV_HEADS, NSA_GROUP * tq, tq).transpose(1, 0, 2, 3)
    r = np.arange(16)[None, :]
    dc = i - CMP_STRIDE * r + (8 * CMP_STRIDE - CMP_LEN + 1)
    ok = (dc >= 0) & (dc < BIAS_SPAN)
    cmpb = jnp.where(ok[None], bv[:, np.clip(dc, 0, BIAS_SPAN - 1)], 0.0)
    cmpb = cmpb.reshape(NSA_KV_HEADS, NSA_GROUP * tq, 16)
    return diag.astype(F32), cmpb.astype(F32)


def _attn_kernel(q_ref, sel_ref, win_ref, cmp_ref, gate_ref, diag_ref, cmpb_ref, mt_ref, o_ref,
                 m_scr, l_scr, acc_scr, mask_scr):
    tq = ATT_TILE
    rows = NSA_GROUP * tq
    kvh = pl.program_id(1)
    qt = pl.program_id(2)
    n_cmp = cmp_ref.shape[2]
    n_sel = mt_ref.shape[0]
    lane = lax.broadcasted_iota(jnp.int32, (tq, LANES), 1)

    qf = q_ref[0].astype(F32)
    parts = []
    for g in range(NSA_GROUP):
        blk = qf[:, (g // 2) * LANES:(g // 2 + 1) * LANES]
        if g % 2 == 1:
            blk = pltpu.roll(blk, HEAD_DIM, axis=1)
        parts.append(jnp.where(lane < HEAD_DIM, blk, 0.0))
    qpad = jnp.concatenate(parts, axis=0).astype(BF16)

    row_i = lax.broadcasted_iota(jnp.int32, (rows, 1), 0) & (tq - 1)
    t_row = qt * tq + row_i

    ckv = cmp_ref[0, 0]
    s = lax.dot_general(qpad, ckv, (((1,), (1,)), ((), ())), preferred_element_type=F32)
    col_n = lax.broadcasted_iota(jnp.int32, (16, n_cmp), 1)
    row_r = lax.broadcasted_iota(jnp.int32, (16, n_cmp), 0)
    shift = (col_n == qt * (tq // CMP_STRIDE) - 8 + row_r).astype(F32)
    s = s + jnp.dot(cmpb_ref[0], shift, precision=HIGHEST, preferred_element_type=F32)
    n_idx = lax.broadcasted_iota(jnp.int32, (rows, n_cmp), 1)
    valid = (n_idx * CMP_STRIDE + (CMP_LEN - 1)) <= t_row
    s = jnp.where(valid, s, NEG_INF)
    mx = jnp.max(s, axis=-1, keepdims=True)
    p = jnp.where(valid, jnp.exp(s - mx), 0.0)
    den = jnp.sum(p, axis=-1, keepdims=True)
    p = p * jnp.where(den > 0.0, 1.0 / den, 0.0)
    o_c = jnp.dot(p.astype(BF16), ckv, preferred_element_type=F32)

    psum = p[0:tq] + p[tq:2 * tq] + p[2 * tq:3 * tq] + p[3 * tq:4 * tq]
    imp = lax.dot_general(mt_ref[...], psum, (((1,), (1,)), ((), ())), precision=HIGHEST,
                          preferred_element_type=F32)
    blk = lax.broadcasted_iota(jnp.int32, (n_sel, tq), 0)
    qi = lax.broadcasted_iota(jnp.int32, (n_sel, tq), 1)
    cur = (qt * tq + qi) // SEL_BLOCK
    forced = (blk == 0) | (blk == cur) | (blk == cur - 1)
    imp = jnp.where(blk <= cur, imp + FORCE_BONUS * forced.astype(F32), NEG_INF)
    rank = jnp.zeros((n_sel, tq), F32)
    for k in range(n_sel):
        rk = imp[k:k + 1, :]
        beats = (rk > imp) | ((rk == imp) & (blk > k))
        rank = rank + beats.astype(F32)
    sel_t = jnp.where(rank < float(SEL_TOP), 1.0, 0.0)
    sel_sq = jnp.concatenate([sel_t, jnp.zeros((tq - n_sel, tq), F32)], axis=0) if n_sel < tq else sel_t
    sel = sel_sq.T[:, :n_sel].astype(BF16)
    n_kt = mask_scr.shape[0]
    e_blk = lax.broadcasted_iota(jnp.int32, (n_sel, n_kt * tq), 0)
    e_key = lax.broadcasted_iota(jnp.int32, (n_sel, n_kt * tq), 1)
    expand = (e_key // SEL_BLOCK == e_blk).astype(BF16)
    keymask = jnp.dot(sel, expand, preferred_element_type=F32)
    for kt in range(n_kt):
        mask_scr[kt] = keymask[:, kt * tq:(kt + 1) * tq]

    key_j = lax.broadcasted_iota(jnp.int32, (rows, tq), 1)
    rel = row_i - key_j

    def reset():
        m_scr[...] = jnp.full(m_scr.shape, NEG_INF, F32)
        l_scr[...] = jnp.zeros(l_scr.shape, F32)
        acc_scr[...] = jnp.zeros(acc_scr.shape, F32)

    def tile_update(kv_ref, kt, keep, bias):
        kv = kv_ref[0, pl.ds(pl.multiple_of(kt * tq, tq), tq), :]
        sc = lax.dot_general(qpad, kv, (((1,), (1,)), ((), ())), preferred_element_type=F32)
        if bias is not None:
            sc = sc + bias
        if keep is not None:
            sc = jnp.where(keep, sc, NEG_INF)
        m_old = m_scr[...]
        m_new = jnp.maximum(m_old, jnp.max(sc, axis=-1, keepdims=True))
        alpha = jnp.exp(m_old - m_new)
        pe = jnp.exp(sc - m_new)
        l_scr[...] = alpha * l_scr[...] + jnp.sum(pe, axis=-1, keepdims=True)
        acc_scr[...] = alpha * acc_scr[...] + jnp.dot(pe.astype(BF16), kv, preferred_element_type=F32)
        m_scr[...] = m_new

    def finish():
        return acc_scr[...] * (1.0 / l_scr[...])

    def sel_keep(kt):
        mk = mask_scr[kt]
        return jnp.concatenate([mk] * NSA_GROUP, axis=0) > 0.5

    reset()

    def sel_body(kt, carry):
        tile_update(sel_ref, kt, sel_keep(kt), None)
        return carry

    lax.fori_loop(0, jnp.maximum(qt - 1, 0), sel_body, 0)

    @pl.when(qt >= 1)
    def _():
        tile_update(sel_ref, qt - 1, sel_keep(qt - 1), diag_ref[0, 1])

    tile_update(sel_ref, qt, sel_keep(qt) & (rel >= 0), diag_ref[0, 0])
    o_s = finish()

    reset()
    n_far = WINDOW // tq

    @pl.when(qt >= n_far)
    def _():
        tile_update(win_ref, qt - n_far, rel < 0, None)

    def win_body(kt, carry):
        tile_update(win_ref, kt, None, None)
        return carry

    lax.fori_loop(jnp.maximum(qt - n_far + 1, 0), jnp.maximum(qt - 1, 0), win_body, 0)

    @pl.when(qt >= 1)
    def _():
        tile_update(win_ref, qt - 1, None, diag_ref[0, 1])

    tile_update(win_ref, qt, rel >= 0, diag_ref[0, 0])
    o_w = finish()

    gate = _sigmoid(gate_ref[0].astype(F32))
    g_row = lax.broadcasted_iota(jnp.int32, (LANES, NSA_GROUP * HEAD_DIM), 0)
    g_col = lax.broadcasted_iota(jnp.int32, (LANES, NSA_GROUP * HEAD_DIM), 1)
    out = jnp.zeros((tq, NSA_GROUP * HEAD_DIM), F32)
    for br, o_br in enumerate((o_c, o_s, o_w)):
        pick = (g_row == (kvh * NSA_GROUP + g_col // HEAD_DIM) * 3 + br).astype(F32)
        gexp = jnp.dot(gate, pick, precision=HIGHEST, preferred_element_type=F32)
        pairs = []
        for pr in range(NSA_GROUP // 2):
            even = pltpu.roll(o_br[(2 * pr) * tq:(2 * pr + 1) * tq], HEAD_DIM, axis=1)
            odd = o_br[(2 * pr + 1) * tq:(2 * pr + 2) * tq]
            pairs.append(jnp.where(lane < HEAD_DIM, even, odd))
        out = out + gexp * jnp.concatenate(pairs, axis=1)
    o_ref[0] = out.astype(BF16)


def attention(proj, cmp_kv, diag, cmpb):
    bsz, l, _ = proj.shape
    tq = ATT_TILE
    n_kt = l // tq
    n_cmp = cmp_kv.shape[2]
    n_sel = l // SEL_BLOCK
    mt = jnp.asarray(np.pad(_cmp_to_sel_t(n_cmp - 1, n_sel), ((0, 0), (0, 1))))
    gw = NSA_GROUP * HEAD_DIM
    return pl.pallas_call(
        _attn_kernel,
        grid=(bsz, NSA_KV_HEADS, n_kt),
        in_specs=[pl.BlockSpec((1, tq, gw), lambda b, h, i: (b, i, COL_Q // gw + h)),
                  pl.BlockSpec((1, l, LANES), lambda b, h, i: (b, 0, COL_SEL // LANES + h)),
                  pl.BlockSpec((1, l, LANES), lambda b, h, i: (b, 0, COL_WIN // LANES + h)),
                  pl.BlockSpec((1, 1, n_cmp, LANES), lambda b, h, i: (b, h, 0, 0)),
                  pl.BlockSpec((1, tq, LANES), lambda b, h, i: (b, i, COL_GATES // LANES)),
                  pl.BlockSpec((1, 2, NSA_GROUP * tq, tq), lambda b, h, i: (h, 0, 0, 0)),
                  pl.BlockSpec((1, NSA_GROUP * tq, 16), lambda b, h, i: (h, 0, 0)),
                  pl.BlockSpec((n_sel, n_cmp), lambda b, h, i: (0, 0))],
        out_specs=pl.BlockSpec((1, tq, gw), lambda b, h, i: (b, i, h)),
        out_shape=jax.ShapeDtypeStruct((bsz, l, NSA_WIDTH), BF16),
        scratch_shapes=[pltpu.VMEM((NSA_GROUP * tq, 1), F32),
                        pltpu.VMEM((NSA_GROUP * tq, 1), F32),
                        pltpu.VMEM((NSA_GROUP * tq, LANES), F32),
                        pltpu.VMEM((n_kt, tq, tq), F32)],
        compiler_params=_params("parallel", "parallel", "arbitrary"),
        name="nsa_attention",
    )(proj, proj, proj, cmp_kv, proj, diag, cmpb, mt)


S5_BLOCK = 256
S5_CHUNK = S5_BLOCK // SUBLANES
S5_SLAB = 1024


def _s5_kernel(u_ref, perm_ref, permt_ref, bre_ref, bim_ref, are_ref, aim_ref, ace_ref, aci_ref, cre_ref, cim_ref,
               d_ref, gw_ref, gb_ref, o_ref, bu_re, bu_im, xs_re, xs_im, carry_re, carry_im, init_re, init_im):
    nr = S5_GROUPS // 16
    rw = 16 * S5_GROUP
    sw = 16 * S5_STATE

    @pl.when(pl.program_id(1) == 0)
    def _():
        carry_re[...] = jnp.zeros(carry_re.shape, F32)
        carry_im[...] = jnp.zeros(carry_im.shape, F32)

    perm = perm_ref[...]
    up = jnp.dot(perm, u_ref[0], preferred_element_type=F32).astype(BF16)
    for r in range(nr):
        ur = up[:, r * rw:(r + 1) * rw]
        bu_re[:, r * sw:(r + 1) * sw] = jnp.dot(ur, bre_ref[r], preferred_element_type=F32)
        bu_im[:, r * sw:(r + 1) * sw] = jnp.dot(ur, bim_ref[r], preferred_element_type=F32)

    nslab = S5_FLAT // S5_SLAB
    row8 = lax.broadcasted_iota(jnp.int32, (SUBLANES, S5_SLAB), 0)
    for sl in range(nslab):
        cs = slice(sl * S5_SLAB, (sl + 1) * S5_SLAB)
        a_re = jnp.broadcast_to(are_ref[:, cs], (SUBLANES, S5_SLAB))
        a_im = jnp.broadcast_to(aim_ref[:, cs], (SUBLANES, S5_SLAB))

        def step1(s, st):
            s_re, s_im = st
            off = pl.multiple_of(s * SUBLANES, SUBLANES)
            n_re = a_re * s_re - a_im * s_im + bu_re[pl.ds(off, SUBLANES), cs]
            n_im = a_re * s_im + a_im * s_re + bu_im[pl.ds(off, SUBLANES), cs]
            return n_re, n_im

        zero = jnp.zeros((SUBLANES, S5_SLAB), F32)
        e_re, e_im = lax.fori_loop(0, S5_CHUNK, step1, (zero, zero))

        ac_re = ace_ref[:, cs]
        ac_im = aci_ref[:, cs]
        x_re = carry_re[:, cs]
        x_im = carry_im[:, cs]
        st_re = jnp.zeros((SUBLANES, S5_SLAB), F32)
        st_im = jnp.zeros((SUBLANES, S5_SLAB), F32)
        for c in range(SUBLANES):
            st_re = jnp.where(row8 == c, x_re, st_re)
            st_im = jnp.where(row8 == c, x_im, st_im)
            n_re = ac_re * x_re - ac_im * x_im + e_re[c:c + 1]
            n_im = ac_re * x_im + ac_im * x_re + e_im[c:c + 1]
            x_re, x_im = n_re, n_im
        carry_re[:, cs] = x_re
        carry_im[:, cs] = x_im
        init_re[:, cs] = st_re
        init_im[:, cs] = st_im

        def step2(s, st):
            s_re, s_im = st
            off = pl.multiple_of(s * SUBLANES, SUBLANES)
            n_re = a_re * s_re - a_im * s_im + bu_re[pl.ds(off, SUBLANES), cs]
            n_im = a_re * s_im + a_im * s_re + bu_im[pl.ds(off, SUBLANES), cs]
            xs_re[pl.ds(off, SUBLANES), cs] = n_re
            xs_im[pl.ds(off, SUBLANES), cs] = n_im
            return n_re, n_im

        lax.fori_loop(0, S5_CHUNK, step2, (st_re, st_im))

    ys = []
    for r in range(nr):
        xr = xs_re[:, r * sw:(r + 1) * sw].astype(BF16)
        xi = xs_im[:, r * sw:(r + 1) * sw].astype(BF16)
        ys.append(jnp.dot(xr, cre_ref[r], preferred_element_type=F32)
                  - jnp.dot(xi, cim_ref[r], preferred_element_type=F32))
    y = jnp.concatenate(ys, axis=1) + d_ref[...] * up.astype(F32)
    y = _gelu(y)
    z = jnp.dot(y.astype(BF16), gw_ref[...], preferred_element_type=F32) + gb_ref[...]
    y = (y * _sigmoid(z)).astype(BF16)
    o_ref[0] = jnp.dot(permt_ref[...], y, preferred_element_type=F32).astype(BF16)


def s5_tables(lam_re, lam_im, log_step, b_re, b_im, c_re, c_im):
    f32 = F32
    lam_re, lam_im = lam_re.astype(f32), lam_im.astype(f32)
    step = jnp.exp(log_step.astype(f32))[:, None]
    mag = jnp.exp(lam_re * step)
    lb_re, lb_im = mag * jnp.cos(lam_im * step), mag * jnp.sin(lam_im * step)
    den = lam_re * lam_re + lam_im * lam_im
    n_re = lb_re - 1.0
    f_re = (n_re * lam_re + lb_im * lam_im) / den
    f_im = (lb_im * lam_re - n_re * lam_im) / den
    b_re, b_im = b_re.astype(f32), b_im.astype(f32)
    bb_re = f_re[..., None] * b_re - f_im[..., None] * b_im
    bb_im = f_re[..., None] * b_im + f_im[..., None] * b_re
    magc = jnp.exp(lam_re * step * S5_CHUNK)
    ac_re, ac_im = magc * jnp.cos(lam_im * step * S5_CHUNK), magc * jnp.sin(lam_im * step * S5_CHUNK)

    nr = S5_GROUPS // 16
    eye = jnp.eye(16, dtype=f32)

    def b_blocks(bb):
        bb = bb.reshape(nr, 16, S5_STATE, S5_GROUP)
        return jnp.einsum('rgph,gk->rghkp', bb, eye).reshape(nr, 16 * S5_GROUP, 16 * S5_STATE).astype(BF16)

    def c_blocks(cc):
        cc = cc.astype(f32).reshape(nr, 16, S5_GROUP, S5_STATE)
        return jnp.einsum('rghp,gk->rgpkh', cc, eye).reshape(nr, 16 * S5_STATE, 16 * S5_GROUP).astype(BF16)

    flat = lambda a: a.reshape(1, S5_FLAT)
    return (b_blocks(bb_re), b_blocks(bb_im), flat(lb_re), flat(lb_im), flat(ac_re), flat(ac_im),
            c_blocks(c_re), c_blocks(c_im))


def _s5_perm():
    p = np.zeros((S5_BLOCK, S5_BLOCK), np.float32)
    for c in range(SUBLANES):
        for s in range(S5_CHUNK):
            p[s * SUBLANES + c, c * S5_CHUNK + s] = 1.0
    return p


def s5_glu(proj, tables, d_skip, glu_w, glu_b):
    bsz, l, _ = proj.shape
    tb = S5_BLOCK
    bre, bim, are, aim, ace, aci, cre, cim = tables
    perm = jnp.asarray(_s5_perm(), BF16)
    permt = jnp.asarray(_s5_perm().T, BF16)
    full = lambda a: pl.BlockSpec(a.shape, lambda b, i: (0,) * a.ndim)
    d2 = d_skip.reshape(1, S5_WIDTH).astype(F32)
    gb2 = glu_b.reshape(1, S5_WIDTH).astype(F32)
    gw = glu_w.astype(BF16)
    return pl.pallas_call(
        _s5_kernel,
        grid=(bsz, l // tb),
        in_specs=[pl.BlockSpec((1, tb, S5_WIDTH), lambda b, i: (b, i, COL_U // S5_WIDTH)),
                  full(perm), full(permt), full(bre), full(bim), full(are), full(aim), full(ace), full(aci),
                  full(cre), full(cim), full(d2), full(gw), full(gb2)],
        out_specs=pl.BlockSpec((1, tb, S5_WIDTH), lambda b, i: (b, i, 0)),
        out_shape=jax.ShapeDtypeStruct((bsz, l, S5_WIDTH), BF16),
        scratch_shapes=[pltpu.VMEM((tb, S5_FLAT), F32), pltpu.VMEM((tb, S5_FLAT), F32),
                        pltpu.VMEM((tb, S5_FLAT), F32), pltpu.VMEM((tb, S5_FLAT), F32),
                        pltpu.VMEM((1, S5_FLAT), F32), pltpu.VMEM((1, S5_FLAT), F32),
                        pltpu.VMEM((SUBLANES, S5_FLAT), F32), pltpu.VMEM((SUBLANES, S5_FLAT), F32)],
        compiler_params=_params("parallel", "arbitrary"),
        name="s5_glu",
    )(proj, perm, permt, bre, bim, are, aim, ace, aci, cre, cim, d2, gw, gb2)


def _merge_kernel(x_ref, oa_ref, ys_ref, ga_ref, gb_ref, g1_ref, wa_ref, ws_ref, wo_ref,
                  ng_ref, sc_ref, sh_ref, x1_ref, h2_ref):
    a = jnp.dot(oa_ref[0], wa_ref[...], preferred_element_type=F32)
    s = jnp.dot(ys_ref[0], ws_ref[...], preferred_element_type=F32)
    mixed = _sigmoid(ga_ref[0].astype(F32)) * a + _sigmoid(gb_ref[0].astype(F32)) * s
    out = jnp.dot(mixed.astype(BF16), wo_ref[...], preferred_element_type=F32)
    x1 = x_ref[0] + g1_ref[0] * out
    x1_ref[0] = x1
    h2_ref[0] = _modulated_norm(x1, ng_ref[...], sc_ref[0], sh_ref[0]).astype(BF16)


def merge(x, o_attn, y_ssm, proj, g1, w_up_attn, w_up_ssm, w_out, norm_g, sc2, sh2, tm=256):
    bsz, l, d = x.shape
    tok = lambda w, cb=0: pl.BlockSpec((1, tm, w), lambda b, i: (b, i, cb))
    vec = pl.BlockSpec((1, 1, d), lambda b, i: (b, 0, 0))
    full = lambda a: pl.BlockSpec(a.shape, lambda b, i: (0,) * a.ndim)
    wa, ws, wo = w_up_attn.astype(BF16), w_up_ssm.astype(BF16), w_out.astype(BF16)
    ng = norm_g.reshape(1, d)
    return pl.pallas_call(
        _merge_kernel,
        grid=(bsz, l // tm),
        in_specs=[tok(d), tok(NSA_WIDTH), tok(S5_WIDTH), tok(d, COL_MERGE // d), tok(d, COL_MERGE // d + 1),
                  vec, full(wa), full(ws), full(wo), full(ng), vec, vec],
        out_specs=[tok(d), tok(d)],
        out_shape=[jax.ShapeDtypeStruct((bsz, l, d), F32), jax.ShapeDtypeStruct((bsz, l, d), BF16)],
        compiler_params=_params("parallel", "parallel"),
        name="merge",
    )(x, o_attn, y_ssm, proj, proj, g1, wa, ws, wo, ng, sc2, sh2)


def _extract_topk(vals, k):
    out = []
    rem = vals
    for _ in range(k):
        m = jnp.max(rem, axis=0, keepdims=True)
        out.append(m)
        rem = jnp.where(rem == m, -jnp.inf, rem)
    return out


def _peer_score_kernel(ht_ref, wqt_ref, keys_ref, s_ref, st_ref):
    tm = ht_ref.shape[1]
    half = PEER_QDIM // 2
    for h in range(PEER_HEADS):
        tops = []
        for c in range(2):
            hc = 2 * h + c
            qt = jnp.dot(wqt_ref[hc * half:(hc + 1) * half, :], ht_ref[...], preferred_element_type=F32)
            sc = jnp.dot(keys_ref[h, c], qt, precision=HIGHEST, preferred_element_type=F32)
            s_ref[hc] = sc
            tops.append(_extract_topk(sc, PEER_TOPK))
        v1, v2 = tops
        v2m = jnp.concatenate(v2, axis=0)
        sums = jnp.concatenate([v1[a] + v2m for a in range(PEER_TOPK)], axis=0)
        best = _extract_topk(sums, PEER_TOPK)
        theta = best[-1]
        top = best[0]
        z = jnp.sum(jnp.where(sums >= theta, jnp.exp(sums - top), 0.0), axis=0, keepdims=True)
        zero = jnp.zeros((SUBLANES - 4, tm), F32)
        st_ref[h] = jnp.concatenate([theta, v1[0], v2[0], 1.0 / z, zero], axis=0)


def peer_scores(h2t, w_q, sub_keys, tm=256):
    d, t = h2t.shape
    wqt = w_q.T.astype(BF16)
    nk = PEER_KEYS
    return pl.pallas_call(
        _peer_score_kernel,
        grid=(t // tm,),
        in_specs=[pl.BlockSpec((d, tm), lambda i: (0, i)),
                  pl.BlockSpec(wqt.shape, lambda i: (0, 0)),
                  pl.BlockSpec(sub_keys.shape, lambda i: (0, 0, 0, 0))],
        out_specs=[pl.BlockSpec((2 * PEER_HEADS, nk, tm), lambda i: (0, 0, i)),
                   pl.BlockSpec((PEER_HEADS, SUBLANES, tm), lambda i: (0, 0, i))],
        out_shape=[jax.ShapeDtypeStruct((2 * PEER_HEADS, nk, t), F32),
                   jax.ShapeDtypeStruct((PEER_HEADS, SUBLANES, t), F32)],
        compiler_params=_params("parallel"),
        name="peer_scores",
    )(h2t, wqt, sub_keys)


PEER_TM = 512
PEER_TE = 512


def _peer_kernel(ht_ref, u_ref, vt_ref, s_ref, st_ref, x1_ref, g2_ref, fg_ref, o_ref, e_scr, acc_scr):
    j = pl.program_id(1)
    nk = PEER_KEYS
    tm = ht_ref.shape[1]
    te = u_ref.shape[0]

    @pl.when(j == 0)
    def _():
        acc_scr[...] = jnp.zeros(acc_scr.shape, F32)
        for h in range(PEER_HEADS):
            st = st_ref[h]
            e_scr[2 * h] = jnp.exp(s_ref[2 * h] - st[1:2]) * st[3:4]
            e_scr[2 * h + 1] = jnp.exp(s_ref[2 * h + 1] - st[2:3])

    at = jnp.dot(u_ref[...], ht_ref[...], preferred_element_type=F32)
    pieces = []
    for rr in range(te // nk):
        r = j * (te // nk) + rr
        w = jnp.zeros((nk, tm), F32)
        for h in range(PEER_HEADS):
            s1 = s_ref[2 * h, pl.ds(r, 1), :]
            e1 = e_scr[2 * h, pl.ds(r, 1), :]
            keep = (s1 + s_ref[2 * h + 1]) >= st_ref[h, 0:1, :]
            w = w + jnp.where(keep, e1 * e_scr[2 * h + 1], 0.0)
        pieces.append((_gelu(at[rr * nk:(rr + 1) * nk]) * w).astype(BF16))
    gw = jnp.concatenate(pieces, axis=0)
    acc_scr[...] += jnp.dot(vt_ref[...], gw, preferred_element_type=F32)

    @pl.when(j == pl.num_programs(1) - 1)
    def _():
        x2 = x1_ref[...] + g2_ref[0] * acc_scr[...].T
        ms = jnp.mean(x2 * x2, axis=-1, keepdims=True)
        o_ref[...] = (x2 * lax.rsqrt(ms + EPS)) * fg_ref[...]


def peer_experts(h2t, u_bf, vt_bf, scores, stats, x1, g2, final_g, l, tm=PEER_TM, te=PEER_TE):
    d, t = h2t.shape
    ne = u_bf.shape[0]
    tpb = l // tm
    return pl.pallas_call(
        _peer_kernel,
        grid=(t // tm, ne // te),
        in_specs=[pl.BlockSpec((d, tm), lambda i, j: (0, i)),
                  pl.BlockSpec((te, d), lambda i, j: (j, 0)),
                  pl.BlockSpec((d, te), lambda i, j: (0, j)),
                  pl.BlockSpec((2 * PEER_HEADS, PEER_KEYS, tm), lambda i, j: (0, 0, i)),
                  pl.BlockSpec((PEER_HEADS, SUBLANES, tm), lambda i, j: (0, 0, i)),
                  pl.BlockSpec((tm, d), lambda i, j: (i, 0)),
                  pl.BlockSpec((1, 1, d), lambda i, j: (i // tpb, 0, 0)),
                  pl.BlockSpec((1, d), lambda i, j: (0, 0))],
        out_specs=pl.BlockSpec((tm, d), lambda i, j: (i, 0)),
        out_shape=jax.ShapeDtypeStruct((t, d), F32),
        scratch_shapes=[pltpu.VMEM((2 * PEER_HEADS, PEER_KEYS, tm), F32), pltpu.VMEM((d, tm), F32)],
        compiler_params=_params("parallel", "arbitrary"),
        name="peer_experts",
    )(h2t, u_bf, vt_bf, scores, stats, x1, g2, final_g.reshape(1, d))


def kernel(x, c, ada_w, ada_b, norm_mix_g, norm_ffn_g, w_in, cmp_pos, cmp_w1, cmp_b1, cmp_w2, cmp_b2, rel_bias,
           s5_lam_re, s5_lam_im, s5_log_step, s5_b_re, s5_b_im, s5_c_re, s5_c_im, s5_d, glu_w, glu_b,
           w_up_attn, w_up_ssm, w_out, peer_w_q, peer_sub_keys, peer_u, peer_v, final_g):
    bsz, l, d = x.shape
    assert ada_w.shape[0] == 1, "single-layer block"
    mod = adaln(c, ada_w[0], ada_b[0])
    sh1, sc1, g1, sh2, sc2, g2 = [m[:, None, :] for m in jnp.split(mod, 6, axis=-1)]

    proj = inproj(x, norm_mix_g[0], sc1, sh1, build_wcat(w_in[0]))

    n16 = l // CMP_STRIDE

    def blocks16(cols):
        return cols.reshape(bsz, l, NSA_KV_HEADS, HEAD_DIM).transpose(0, 2, 1, 3).reshape(
            bsz, NSA_KV_HEADS, n16, CMP_STRIDE * HEAD_DIM)

    cmp_kv = compress(blocks16(proj[:, :, COL_CMP:COL_CMP + KV_WIDTH]),
                      blocks16(proj[:, :, COL_CMP + KV_WIDTH:COL_CMP + 2 * KV_WIDTH]),
                      cmp_w1[0], cmp_pos[0], cmp_b1[0], cmp_w2[0], cmp_b2[0])
    diag, cmpb = attention_tables(rel_bias)
    o_attn = attention(proj, cmp_kv, diag, cmpb)

    tables = s5_tables(s5_lam_re[0], s5_lam_im[0], s5_log_step[0], s5_b_re[0], s5_b_im[0],
                       s5_c_re[0], s5_c_im[0])
    y_ssm = s5_glu(proj, tables, s5_d[0], glu_w[0], glu_b[0])

    x1, h2 = merge(x, o_attn, y_ssm, proj, g1, w_up_attn[0], w_up_ssm[0], w_out[0],
                   norm_ffn_g[0], sc2, sh2)

    t = bsz * l
    h2t = h2.reshape(t, d).T
    scores, stats = peer_scores(h2t, peer_w_q[0], peer_sub_keys[0])
    out = peer_experts(h2t, peer_u[0].astype(BF16), peer_v[0].T.astype(BF16), scores, stats,
                       x1.reshape(t, d), g2, final_g, l)
    return out.reshape(bsz, l, d)
```

```python
import functools
import math

import numpy as np
import jax
import jax.numpy as jnp
from jax import lax
from jax.experimental import pallas as pl
from jax.experimental.pallas import tpu as pltpu

F32 = jnp.float32
BF16 = jnp.bfloat16
HIGHEST = lax.Precision.HIGHEST

D_MODEL = 2048
NSA_HEADS = 16
NSA_KV_HEADS = 4
NSA_GROUP = NSA_HEADS // NSA_KV_HEADS
HEAD_DIM = 64
NSA_WIDTH = NSA_HEADS * HEAD_DIM
KV_WIDTH = NSA_KV_HEADS * HEAD_DIM
CMP_LEN = 32
CMP_STRIDE = 16
CMP_HIDDEN = 256
SEL_BLOCK = 64
SEL_TOP = 16
WINDOW = 512
S5_WIDTH = 1024
S5_GROUP = 16
S5_GROUPS = S5_WIDTH // S5_GROUP
S5_STATE = 64
S5_FLAT = S5_GROUPS * S5_STATE
REL_BUCKETS = 32
REL_MAX_DIST = 128
PEER_KEYS = 128
PEER_EXPERTS = PEER_KEYS * PEER_KEYS
PEER_HEADS = 8
PEER_TOPK = 16
PEER_QDIM = 256
EPS = 1e-6
NEG_INF = -1e30
FORCE_BONUS = 1e4

VMEM_LIMIT_BYTES = 56 * 1024 * 1024
LANES = 128
SUBLANES = 8

PROJ_WIDTH = 8192
COL_Q = 0
COL_U = 1024
COL_MERGE = 2048
COL_SEL = 6144
COL_WIN = 6656
COL_CMP = 7168
COL_GATES = 7680

ATT_TILE = 128
BIAS_SPAN = 256
ATT_SUPER = 512
ATT_ORIGIN = 2 * ATT_SUPER - ATT_TILE
ATT_TABLE_ROWS = ATT_ORIGIN + WINDOW + ATT_TILE


def _params(*sem):
    return pltpu.CompilerParams(dimension_semantics=sem, vmem_limit_bytes=VMEM_LIMIT_BYTES)


def _gelu(x):
    return 0.5 * x * (1.0 + jnp.tanh(math.sqrt(2.0 / math.pi) * (x + 0.044715 * (x * x * x))))


def _sigmoid(x):
    return 1.0 / (1.0 + jnp.exp(-x))


def _adaln_kernel(c_ref, w_ref, b_ref, o_ref):
    c = c_ref[...]
    a = c * _sigmoid(c)
    o_ref[...] = jnp.dot(a, w_ref[...], precision=HIGHEST, preferred_element_type=F32) + b_ref[...]


def adaln(c, w, b):
    bsz, d = c.shape
    n = w.shape[1]
    tn = 1024
    cp = jnp.zeros((SUBLANES, d), F32).at[:bsz].set(c)
    out = pl.pallas_call(
        _adaln_kernel,
        grid=(n // tn,),
        in_specs=[pl.BlockSpec((SUBLANES, d), lambda j: (0, 0)),
                  pl.BlockSpec((d, tn), lambda j: (0, j)),
                  pl.BlockSpec((1, tn), lambda j: (0, j))],
        out_specs=pl.BlockSpec((SUBLANES, tn), lambda j: (0, j)),
        out_shape=jax.ShapeDtypeStruct((SUBLANES, n), F32),
        compiler_params=_params("parallel"),
        name="adaln",
    )(cp, w, b.reshape(1, n))
    return out[:bsz]


def _modulated_norm(x, g, sc, sh):
    ms = jnp.mean(x * x, axis=-1, keepdims=True)
    return (x * lax.rsqrt(ms + EPS)) * g * (1.0 + sc) + sh


def _inproj_kernel(x_ref, g_ref, sc_ref, sh_ref, w_ref, o_ref, h_scr):
    @pl.when(pl.program_id(2) == 0)
    def _():
        h_scr[...] = _modulated_norm(x_ref[0], g_ref[...], sc_ref[0], sh_ref[0]).astype(BF16)

    o_ref[0] = jnp.dot(h_scr[...], w_ref[...], preferred_element_type=F32).astype(BF16)


def inproj(x, g, sc, sh, wcat, tm=1024, tn=1024):
    bsz, l, d = x.shape
    n = wcat.shape[1]
    tm = min(tm, l)
    return pl.pallas_call(
        _inproj_kernel,
        grid=(bsz, l // tm, n // tn),
        in_specs=[pl.BlockSpec((1, tm, d), lambda b, i, j: (b, i, 0)),
                  pl.BlockSpec((1, d), lambda b, i, j: (0, 0)),
                  pl.BlockSpec((1, 1, d), lambda b, i, j: (b, 0, 0)),
                  pl.BlockSpec((1, 1, d), lambda b, i, j: (b, 0, 0)),
                  pl.BlockSpec((d, tn), lambda b, i, j: (0, j))],
        out_specs=pl.BlockSpec((1, tm, tn), lambda b, i, j: (b, i, j)),
        out_shape=jax.ShapeDtypeStruct((bsz, l, n), BF16),
        scratch_shapes=[pltpu.VMEM((tm, d), BF16)],
        compiler_params=_params("parallel", "parallel", "arbitrary"),
        name="inproj",
    )(x, g.reshape(1, d), sc, sh, wcat)


def build_wcat(w_in):
    d = w_in.shape[0]
    pts = np.cumsum([NSA_WIDTH] + [KV_WIDTH] * 6 + [3 * NSA_HEADS, S5_WIDTH])
    q, kc, vc, ks, vs, kw, vw, gates, u, merge = jnp.split(w_in, pts, axis=1)

    def pair(k, v):
        return jnp.stack([k.reshape(d, NSA_KV_HEADS, HEAD_DIM), v.reshape(d, NSA_KV_HEADS, HEAD_DIM)],
                         axis=2).reshape(d, 2 * KV_WIDTH)

    gates_p = jnp.pad(gates, ((0, 0), (0, LANES - 3 * NSA_HEADS)))
    pad = jnp.zeros((d, PROJ_WIDTH - COL_GATES - LANES), w_in.dtype)
    wcat = jnp.concatenate([q * (HEAD_DIM ** -0.5), u, merge, pair(ks, vs), pair(kw, vw),
                            kc, vc, gates_p, pad], axis=1)
    return wcat.astype(BF16)


def _compress_kernel(gk_ref, gv_ref, w1_ref, pos_ref, b1_ref, w2_ref, b2_ref, o_ref):
    half = CMP_STRIDE * HEAD_DIM
    nrow = gk_ref.shape[2]
    out = b2_ref[...]
    for j, g_ref in enumerate((gk_ref, gv_ref)):
        g = g_ref[0, 0]
        w1 = w1_ref[j]
        top = jnp.dot(g, w1[:half].astype(BF16), preferred_element_type=F32)
        bot = jnp.dot(g, w1[half:].astype(BF16), preferred_element_type=F32)
        cst = jnp.dot(pos_ref[j], w1, precision=HIGHEST, preferred_element_type=F32)[0:1] + b1_ref[j]
        hid = top + pltpu.roll(bot, nrow - 1, axis=0) + cst
        out = out + jnp.dot(_gelu(hid).astype(BF16), w2_ref[j].astype(BF16), preferred_element_type=F32)
    o_ref[0, 0] = out.astype(BF16)


def compress(gk, gv, w1, pos, b1, w2, b2):
    bsz, hkv, n16, wd = gk.shape
    posf = jnp.zeros((2, SUBLANES, CMP_LEN * HEAD_DIM), F32).at[:, 0].set(pos.reshape(2, -1))
    w2p = jnp.zeros((2, CMP_HIDDEN, 2 * HEAD_DIM), F32)
    w2p = w2p.at[0, :, :HEAD_DIM].set(w2[0]).at[1, :, HEAD_DIM:].set(w2[1])
    b2p = jnp.concatenate([b2[0], b2[1]]).reshape(1, 2 * HEAD_DIM)
    gspec = pl.BlockSpec((1, 1, n16, wd), lambda b, h: (b, h, 0, 0))
    full = lambda a: pl.BlockSpec(a.shape, lambda b, h: (0,) * a.ndim)
    b1r = b1.reshape(2, 1, CMP_HIDDEN)
    return pl.pallas_call(
        _compress_kernel,
        grid=(bsz, hkv),
        in_specs=[gspec, gspec, full(w1), full(posf), full(b1r), full(w2p), full(b2p)],
        out_specs=pl.BlockSpec((1, 1, n16, 2 * HEAD_DIM), lambda b, h: (b, h, 0, 0)),
        out_shape=jax.ShapeDtypeStruct((bsz, hkv, n16, 2 * HEAD_DIM), BF16),
        compiler_params=_params("parallel", "parallel"),
        name="compress",
    )(gk, gv, w1, posf, b1r, w2p, b2p)


def _t5_bucket_table(n):
    dist = np.arange(n)
    max_exact = REL_BUCKETS // 2
    ratio = np.log(np.maximum(dist, 1).astype(np.float32) / np.float32(max_exact)) / np.float32(
        math.log(REL_MAX_DIST / max_exact))
    large = np.minimum(max_exact + (ratio * np.float32(REL_BUCKETS - max_exact)).astype(np.int32), REL_BUCKETS - 1)
    return np.where(dist < max_exact, dist, large)


def _cmp_to_sel_t(n_cmp, n_sel):
    cs = np.arange(n_cmp)[:, None] * CMP_STRIDE
    ss = np.arange(n_sel)[None, :] * SEL_BLOCK
    ov = np.clip(np.minimum(cs + CMP_LEN, ss + SEL_BLOCK) - np.maximum(cs, ss), 0, None)
    return np.ascontiguousarray((ov / CMP_LEN).T.astype(np.float32))


def attention_tables(rel_bias):
    tq = ATT_TILE
    bucket = _t5_bucket_table(BIAS_SPAN)
    assert np.all(bucket[113:] == REL_BUCKETS - 1)
    bv = (rel_bias[bucket] - rel_bias[REL_BUCKETS - 1][None, :]).T
    r = np.arange(ATT_TABLE_ROWS)[:, None]
    i = np.arange(tq)[None, :]
    dist = ATT_ORIGIN + i - r
    ok = (dist >= 0) & (dist < BIAS_SPAN)
    tab = jnp.where(ok[None], bv[:, np.clip(dist, 0, BIAS_SPAN - 1)], 0.0)
    tab = tab.reshape(NSA_KV_HEADS, NSA_GROUP, ATT_TABLE_ROWS, tq).transpose(0, 2, 1, 3)
    tab = tab.reshape(NSA_KV_HEADS, ATT_TABLE_ROWS, NSA_GROUP * tq)
    rr = np.arange(16)[None, :]
    ii = np.arange(tq)[:, None]
    dc = ii - CMP_STRIDE * rr + (8 * CMP_STRIDE - CMP_LEN + 1)
    okc = (dc >= 0) & (dc < BIAS_SPAN)
    cmpb = jnp.where(okc[None], bv[:, np.clip(dc, 0, BIAS_SPAN - 1)], 0.0)
    cmpb = cmpb.reshape(NSA_KV_HEADS, NSA_GROUP * tq, 16).transpose(0, 2, 1)
    return tab.astype(F32), cmpb.astype(F32)


def _attn_kernel(q_ref, sel_ref, selt_ref, win_ref, wint_ref, cmp_ref, cmpt_ref, gate_ref, tab_ref, cmpb_ref,
                 mt_ref, o_ref, m_scr, l_scr, acc_scr, sel_scr):
    tq = ATT_TILE
    sk = ATT_SUPER
    cols = NSA_GROUP * tq
    kvh = pl.program_id(1)
    qt = pl.program_id(2)
    n_cmp = cmp_ref.shape[2]
    n_sel = mt_ref.shape[0]
    per = sk // tq

    qT = q_ref[0]
    zpad = jnp.zeros((HEAD_DIM, tq), BF16)
    qpad = jnp.concatenate(
        [jnp.concatenate([qT[g * HEAD_DIM:(g + 1) * HEAD_DIM], zpad], axis=0) for g in range(NSA_GROUP)],
        axis=1)
    col_i = lax.broadcasted_iota(jnp.int32, (1, cols), 1) & (tq - 1)
    t_col = qt * tq + col_i

    ckv = cmp_ref[0, 0]
    s = jnp.dot(ckv, qpad, preferred_element_type=F32)
    row_n = lax.broadcasted_iota(jnp.int32, (n_cmp, 16), 0)
    col_r = lax.broadcasted_iota(jnp.int32, (n_cmp, 16), 1)
    shift = (row_n == qt * (tq // CMP_STRIDE) - 8 + col_r).astype(F32)
    s = s + jnp.dot(shift, cmpb_ref[0], precision=HIGHEST, preferred_element_type=F32)
    n_idx = lax.broadcasted_iota(jnp.int32, (n_cmp, cols), 0)
    valid = (n_idx * CMP_STRIDE + (CMP_LEN - 1)) <= t_col
    s = jnp.where(valid, s, NEG_INF)
    mx = jnp.max(s, axis=0, keepdims=True)
    p = jnp.where(valid, jnp.exp(s - mx), 0.0)
    den = jnp.sum(p, axis=0, keepdims=True)
    p = p * jnp.where(den > 0.0, 1.0 / den, 0.0)
    o_c = jnp.dot(cmpt_ref[0, 0], p.astype(BF16), preferred_element_type=F32)

    psum = p[:, 0:tq] + p[:, tq:2 * tq] + p[:, 2 * tq:3 * tq] + p[:, 3 * tq:4 * tq]
    imp = jnp.dot(mt_ref[...], psum, precision=HIGHEST, preferred_element_type=F32)
    blk = lax.broadcasted_iota(jnp.int32, (n_sel, tq), 0)
    qi = lax.broadcasted_iota(jnp.int32, (n_sel, tq), 1)
    cur = jnp.right_shift(qt * tq + qi, 6)
    forced = (blk == 0) | (blk == cur) | (blk == cur - 1)
    imp = jnp.where(blk <= cur, imp + FORCE_BONUS * forced.astype(F32), NEG_INF)
    rank = jnp.zeros((n_sel, tq), F32)
    for k in range(n_sel):
        rk = imp[k:k + 1, :]
        beats = (rk > imp) | ((rk == imp) & (blk > k))
        rank = rank + beats.astype(F32)
    sel_scr[...] = jnp.where(rank < float(SEL_TOP), 1.0, 0.0)

    m_scr[...] = jnp.full(m_scr.shape, NEG_INF, F32)
    l_scr[...] = jnp.zeros(l_scr.shape, F32)
    acc_scr[...] = jnp.zeros(acc_scr.shape, F32)
    st_last = qt // per
    dq = qt - st_last * per

    def sel_tile(st, table_row, causal_shift):
        kv = sel_ref[0, pl.ds(pl.multiple_of(st * sk, sk), sk), :]
        sc = jnp.dot(kv, qpad, preferred_element_type=F32)
        if table_row is not None:
            sc = sc + tab_ref[0, pl.ds(pl.multiple_of(table_row, tq), sk), :]
        blocks = sk // SEL_BLOCK
        rows = [jnp.broadcast_to(sel_scr[pl.ds(st * blocks + b, 1), :], (SEL_BLOCK, tq)) for b in range(blocks)]
        one = jnp.concatenate(rows, axis=0) > 0.5
        keep = jnp.concatenate([one] * NSA_GROUP, axis=1)
        if causal_shift is not None:
            rel = col_i - lax.broadcasted_iota(jnp.int32, (sk, cols), 0)
            keep = keep & (rel + causal_shift >= 0)
        sc = jnp.where(keep, sc, NEG_INF)
        m_old = m_scr[...]
        m_new = jnp.maximum(m_old, jnp.max(sc, axis=0, keepdims=True))
        alpha = jnp.exp(m_old - m_new)
        pe = jnp.exp(sc - m_new)
        l_scr[...] = alpha * l_scr[...] + jnp.sum(pe, axis=0, keepdims=True)
        acc_scr[...] = alpha * acc_scr[...] + jnp.dot(selt_ref[0, 0, st], pe.astype(BF16),
                                                      preferred_element_type=F32)
        m_scr[...] = m_new

    def sel_body(st, carry):
        sel_tile(st, None, None)
        return carry

    lax.fori_loop(0, jnp.maximum(st_last - 1, 0), sel_body, 0)

    @pl.when(st_last >= 1)
    def _():
        sel_tile(st_last - 1, ATT_ORIGIN - sk - dq * tq, None)

    sel_tile(st_last, ATT_ORIGIN - dq * tq, dq * tq)
    o_s = acc_scr[HEAD_DIM:, :] * (1.0 / l_scr[...])

    wk = WINDOW + tq
    n_w = wk // tq
    k0 = jnp.maximum(qt - WINDOW // tq, 0)
    dw = (qt - k0) * tq
    kv = win_ref[0, pl.ds(pl.multiple_of(k0 * tq, tq), wk), :]
    sc = jnp.dot(kv, qpad, preferred_element_type=F32)
    sc = sc + tab_ref[0, pl.ds(pl.multiple_of(ATT_ORIGIN - dw, tq), wk), :]
    dist = col_i - lax.broadcasted_iota(jnp.int32, (wk, cols), 0) + dw
    sc = jnp.where((dist >= 0) & (dist < WINDOW), sc, NEG_INF)
    mw = jnp.max(sc, axis=0, keepdims=True)
    pw = jnp.exp(sc - mw)
    lw = jnp.sum(pw, axis=0, keepdims=True)
    pw = pw.astype(BF16)
    ow = jnp.dot(wint_ref[0, 0, k0], pw[0:tq], preferred_element_type=F32)
    for m in range(1, n_w):
        ow = ow + jnp.dot(wint_ref[0, 0, k0 + m], pw[m * tq:(m + 1) * tq], preferred_element_type=F32)
    o_w = ow[HEAD_DIM:, :] * (1.0 / lw)

    branches = (o_c[HEAD_DIM:, :], o_s, o_w)
    outs = []
    for g in range(NSA_GROUP):
        acc = jnp.zeros((HEAD_DIM, tq), F32)
        for br in range(3):
            grow = gate_ref[0, pl.ds((kvh * NSA_GROUP + g) * 3 + br, 1), :]
            acc = acc + _sigmoid(grow) * branches[br][:, g * tq:(g + 1) * tq]
        outs.append(acc)
    for pr in range(NSA_GROUP // 2):
        pair = jnp.concatenate([outs[2 * pr], outs[2 * pr + 1]], axis=0)
        o_ref[0, :, pr * LANES:(pr + 1) * LANES] = pair.T.astype(BF16)


def attention(proj, cmp_kv, table, cmpb_t):
    bsz, l, _ = proj.shape
    tq = ATT_TILE
    sk = ATT_SUPER
    assert l % sk == 0 and l >= WINDOW + tq
    n_kt = l // tq
    n_st = l // sk
    n_cmp = cmp_kv.shape[2]
    n_sel = l // SEL_BLOCK
    mt = jnp.asarray(np.pad(_cmp_to_sel_t(n_cmp - 1, n_sel), ((0, 0), (0, 1))))
    gw = NSA_GROUP * HEAD_DIM
    kvw = 2 * KV_WIDTH
    q_t = proj[:, :, COL_Q:COL_Q + NSA_WIDTH].transpose(0, 2, 1)
    tiles_t = lambda c0, nt: proj[:, :, c0:c0 + kvw].reshape(bsz, nt, l // nt, NSA_KV_HEADS, LANES).transpose(0, 3, 1, 4, 2)
    sel_t = tiles_t(COL_SEL, n_st)
    win_t = tiles_t(COL_WIN, n_kt)
    cmp_t = cmp_kv.transpose(0, 1, 3, 2)
    gates_t = proj[:, :, COL_GATES:COL_GATES + LANES].astype(F32).transpose(0, 2, 1)
    return pl.pallas_call(
        _attn_kernel,
        grid=(bsz, NSA_KV_HEADS, n_kt),
        in_specs=[pl.BlockSpec((1, gw, tq), lambda b, h, i: (b, h, i)),
                  pl.BlockSpec((1, l, LANES), lambda b, h, i: (b, 0, COL_SEL // LANES + h)),
                  pl.BlockSpec((1, 1, n_st, LANES, sk), lambda b, h, i: (b, h, 0, 0, 0)),
                  pl.BlockSpec((1, l, LANES), lambda b, h, i: (b, 0, COL_WIN // LANES + h)),
                  pl.BlockSpec((1, 1, n_kt, LANES, tq), lambda b, h, i: (b, h, 0, 0, 0)),
                  pl.BlockSpec((1, 1, n_cmp, LANES), lambda b, h, i: (b, h, 0, 0)),
                  pl.BlockSpec((1, 1, LANES, n_cmp), lambda b, h, i: (b, h, 0, 0)),
                  pl.BlockSpec((1, LANES, tq), lambda b, h, i: (b, 0, i)),
                  pl.BlockSpec((1, ATT_TABLE_ROWS, NSA_GROUP * tq), lambda b, h, i: (h, 0, 0)),
                  pl.BlockSpec((1, 16, NSA_GROUP * tq), lambda b, h, i: (h, 0, 0)),
                  pl.BlockSpec((n_sel, n_cmp), lambda b, h, i: (0, 0))],
        out_specs=pl.BlockSpec((1, tq, gw), lambda b, h, i: (b, i, h)),
        out_shape=jax.ShapeDtypeStruct((bsz, l, NSA_WIDTH), BF16),
        scratch_shapes=[pltpu.VMEM((1, NSA_GROUP * tq), F32),
                        pltpu.VMEM((1, NSA_GROUP * tq), F32),
                        pltpu.VMEM((LANES, NSA_GROUP * tq), F32),
                        pltpu.VMEM((n_sel, tq), F32)],
        compiler_params=_params("parallel", "parallel", "arbitrary"),
        name="nsa_attention",
    )(q_t, proj, sel_t, proj, win_t, cmp_kv, cmp_t, gates_t, table, cmpb_t, mt)


S5_BLOCK = 256
S5_CHUNK = S5_BLOCK // SUBLANES
S5_SLAB = 1024


def _s5_kernel(u_ref, perm_ref, permt_ref, bre_ref, bim_ref, are_ref, aim_ref, ace_ref, aci_ref, cre_ref, cim_ref,
               d_ref, gw_ref, gb_ref, o_ref, bu_re, bu_im, xs_re, xs_im, carry_re, carry_im, init_re, init_im):
    nr = S5_GROUPS // 16
    rw = 16 * S5_GROUP
    sw = 16 * S5_STATE

    @pl.when(pl.program_id(1) == 0)
    def _():
        carry_re[...] = jnp.zeros(carry_re.shape, F32)
        carry_im[...] = jnp.zeros(carry_im.shape, F32)

    perm = perm_ref[...]
    up = jnp.dot(perm, u_ref[0], preferred_element_type=F32).astype(BF16)
    for r in range(nr):
        ur = up[:, r * rw:(r + 1) * rw]
        bu_re[:, r * sw:(r + 1) * sw] = jnp.dot(ur, bre_ref[r], preferred_element_type=F32)
        bu_im[:, r * sw:(r + 1) * sw] = jnp.dot(ur, bim_ref[r], preferred_element_type=F32)

    nslab = S5_FLAT // S5_SLAB
    row8 = lax.broadcasted_iota(jnp.int32, (SUBLANES, S5_SLAB), 0)
    for sl in range(nslab):
        cs = slice(sl * S5_SLAB, (sl + 1) * S5_SLAB)
        a_re = jnp.broadcast_to(are_ref[:, cs], (SUBLANES, S5_SLAB))
        a_im = jnp.broadcast_to(aim_ref[:, cs], (SUBLANES, S5_SLAB))

        def step1(s, st):
            s_re, s_im = st
            off = pl.multiple_of(s * SUBLANES, SUBLANES)
            n_re = a_re * s_re - a_im * s_im + bu_re[pl.ds(off, SUBLANES), cs]
            n_im = a_re * s_im + a_im * s_re + bu_im[pl.ds(off, SUBLANES), cs]
            return n_re, n_im

        zero = jnp.zeros((SUBLANES, S5_SLAB), F32)
        e_re, e_im = lax.fori_loop(0, S5_CHUNK, step1, (zero, zero))

        ac_re = ace_ref[:, cs]
        ac_im = aci_ref[:, cs]
        x_re = carry_re[:, cs]
        x_im = carry_im[:, cs]
        st_re = jnp.zeros((SUBLANES, S5_SLAB), F32)
        st_im = jnp.zeros((SUBLANES, S5_SLAB), F32)
        for c in range(SUBLANES):
            st_re = jnp.where(row8 == c, x_re, st_re)
            st_im = jnp.where(row8 == c, x_im, st_im)
            n_re = ac_re * x_re - ac_im * x_im + e_re[c:c + 1]
            n_im = ac_re * x_im + ac_im * x_re + e_im[c:c + 1]
            x_re, x_im = n_re, n_im
        carry_re[:, cs] = x_re
        carry_im[:, cs] = x_im
        init_re[:, cs] = st_re
        init_im[:, cs] = st_im

        def step2(s, st):
            s_re, s_im = st
            off = pl.multiple_of(s * SUBLANES, SUBLANES)
            n_re = a_re * s_re - a_im * s_im + bu_re[pl.ds(off, SUBLANES), cs]
            n_im = a_re * s_im + a_im * s_re + bu_im[pl.ds(off, SUBLANES), cs]
            xs_re[pl.ds(off, SUBLANES), cs] = n_re
            xs_im[pl.ds(off, SUBLANES), cs] = n_im
            return n_re, n_im

        lax.fori_loop(0, S5_CHUNK, step2, (st_re, st_im))

    ys = []
    for r in range(nr):
        xr = xs_re[:, r * sw:(r + 1) * sw].astype(BF16)
        xi = xs_im[:, r * sw:(r + 1) * sw].astype(BF16)
        ys.append(jnp.dot(xr, cre_ref[r], preferred_element_type=F32)
                  - jnp.dot(xi, cim_ref[r], preferred_element_type=F32))
    y = jnp.concatenate(ys, axis=1) + d_ref[...] * up.astype(F32)
    y = _gelu(y)
    z = jnp.dot(y.astype(BF16), gw_ref[...], preferred_element_type=F32) + gb_ref[...]
    y = (y * _sigmoid(z)).astype(BF16)
    o_ref[0] = jnp.dot(permt_ref[...], y, preferred_element_type=F32).astype(BF16)


def s5_tables(lam_re, lam_im, log_step, b_re, b_im, c_re, c_im):
    f32 = F32
    lam_re, lam_im = lam_re.astype(f32), lam_im.astype(f32)
    step = jnp.exp(log_step.astype(f32))[:, None]
    mag = jnp.exp(lam_re * step)
    lb_re, lb_im = mag * jnp.cos(lam_im * step), mag * jnp.sin(lam_im * step)
    den = lam_re * lam_re + lam_im * lam_im
    n_re = lb_re - 1.0
    f_re = (n_re * lam_re + lb_im * lam_im) / den
    f_im = (lb_im * lam_re - n_re * lam_im) / den
    b_re, b_im = b_re.astype(f32), b_im.astype(f32)
    bb_re = f_re[..., None] * b_re - f_im[..., None] * b_im
    bb_im = f_re[..., None] * b_im + f_im[..., None] * b_re
    magc = jnp.exp(lam_re * step * S5_CHUNK)
    ac_re, ac_im = magc * jnp.cos(lam_im * step * S5_CHUNK), magc * jnp.sin(lam_im * step * S5_CHUNK)

    nr = S5_GROUPS // 16
    eye = jnp.eye(16, dtype=f32)

    def b_blocks(bb):
        bb = bb.reshape(nr, 16, S5_STATE, S5_GROUP)
        return jnp.einsum('rgph,gk->rghkp', bb, eye).reshape(nr, 16 * S5_GROUP, 16 * S5_STATE).astype(BF16)

    def c_blocks(cc):
        cc = cc.astype(f32).reshape(nr, 16, S5_GROUP, S5_STATE)
        return jnp.einsum('rghp,gk->rgpkh', cc, eye).reshape(nr, 16 * S5_STATE, 16 * S5_GROUP).astype(BF16)

    flat = lambda a: a.reshape(1, S5_FLAT)
    return (b_blocks(bb_re), b_blocks(bb_im), flat(lb_re), flat(lb_im), flat(ac_re), flat(ac_im),
            c_blocks(c_re), c_blocks(c_im))


def _s5_perm():
    p = np.zeros((S5_BLOCK, S5_BLOCK), np.float32)
    for c in range(SUBLANES):
        for s in range(S5_CHUNK):
            p[s * SUBLANES + c, c * S5_CHUNK + s] = 1.0
    return p


def s5_glu(proj, tables, d_skip, glu_w, glu_b):
    bsz, l, _ = proj.shape
    tb = S5_BLOCK
    bre, bim, are, aim, ace, aci, cre, cim = tables
    perm = jnp.asarray(_s5_perm(), BF16)
    permt = jnp.asarray(_s5_perm().T, BF16)
    full = lambda a: pl.BlockSpec(a.shape, lambda b, i: (0,) * a.ndim)
    d2 = d_skip.reshape(1, S5_WIDTH).astype(F32)
    gb2 = glu_b.reshape(1, S5_WIDTH).astype(F32)
    gw = glu_w.astype(BF16)
    return pl.pallas_call(
        _s5_kernel,
        grid=(bsz, l // tb),
        in_specs=[pl.BlockSpec((1, tb, S5_WIDTH), lambda b, i: (b, i, COL_U // S5_WIDTH)),
                  full(perm), full(permt), full(bre), full(bim), full(are), full(aim), full(ace), full(aci),
                  full(cre), full(cim), full(d2), full(gw), full(gb2)],
        out_specs=pl.BlockSpec((1, tb, S5_WIDTH), lambda b, i: (b, i, 0)),
        out_shape=jax.ShapeDtypeStruct((bsz, l, S5_WIDTH), BF16),
        scratch_shapes=[pltpu.VMEM((tb, S5_FLAT), F32), pltpu.VMEM((tb, S5_FLAT), F32),
                        pltpu.VMEM((tb, S5_FLAT), F32), pltpu.VMEM((tb, S5_FLAT), F32),
                        pltpu.VMEM((1, S5_FLAT), F32), pltpu.VMEM((1, S5_FLAT), F32),
                        pltpu.VMEM((SUBLANES, S5_FLAT), F32), pltpu.VMEM((SUBLANES, S5_FLAT), F32)],
        compiler_params=_params("parallel", "arbitrary"),
        name="s5_glu",
    )(proj, perm, permt, bre, bim, are, aim, ace, aci, cre, cim, d2, gw, gb2)


def _merge_kernel(x_ref, oa_ref, ys_ref, ga_ref, gb_ref, g1_ref, wa_ref, ws_ref, wo_ref,
                  ng_ref, sc_ref, sh_ref, x1_ref, h2_ref):
    a = jnp.dot(oa_ref[0], wa_ref[...], preferred_element_type=F32)
    s = jnp.dot(ys_ref[0], ws_ref[...], preferred_element_type=F32)
    mixed = _sigmoid(ga_ref[0].astype(F32)) * a + _sigmoid(gb_ref[0].astype(F32)) * s
    out = jnp.dot(mixed.astype(BF16), wo_ref[...], preferred_element_type=F32)
    x1 = x_ref[0] + g1_ref[0] * out
    x1_ref[0] = x1
    h2_ref[0] = _modulated_norm(x1, ng_ref[...], sc_ref[0], sh_ref[0]).astype(BF16)


def merge(x, o_attn, y_ssm, proj, g1, w_up_attn, w_up_ssm, w_out, norm_g, sc2, sh2, tm=256):
    bsz, l, d = x.shape
    tok = lambda w, cb=0: pl.BlockSpec((1, tm, w), lambda b, i: (b, i, cb))
    vec = pl.BlockSpec((1, 1, d), lambda b, i: (b, 0, 0))
    full = lambda a: pl.BlockSpec(a.shape, lambda b, i: (0,) * a.ndim)
    wa, ws, wo = w_up_attn.astype(BF16), w_up_ssm.astype(BF16), w_out.astype(BF16)
    ng = norm_g.reshape(1, d)
    return pl.pallas_call(
        _merge_kernel,
        grid=(bsz, l // tm),
        in_specs=[tok(d), tok(NSA_WIDTH), tok(S5_WIDTH), tok(d, COL_MERGE // d), tok(d, COL_MERGE // d + 1),
                  vec, full(wa), full(ws), full(wo), full(ng), vec, vec],
        out_specs=[tok(d), tok(d)],
        out_shape=[jax.ShapeDtypeStruct((bsz, l, d), F32), jax.ShapeDtypeStruct((bsz, l, d), BF16)],
        compiler_params=_params("parallel", "parallel"),
        name="merge",
    )(x, o_attn, y_ssm, proj, proj, g1, wa, ws, wo, ng, sc2, sh2)


def _extract_topk(vals, k):
    out = []
    rem = vals
    for _ in range(k):
        m = jnp.max(rem, axis=0, keepdims=True)
        out.append(m)
        rem = jnp.where(rem == m, -jnp.inf, rem)
    return out


_PAIR_COUNTS = [PEER_TOPK // (a + 1) for a in range(PEER_TOPK)]


def _candidate_sums(v1, v2, width):
    v1m = jnp.concatenate(v1, axis=0)
    v2m = jnp.concatenate(v2, axis=0)
    row = lax.broadcasted_iota(jnp.int32, (SUBLANES, width), 0)
    pieces = [v1[0] + v2m]
    for a in range(1, SUBLANES):
        pieces.append(jnp.where(row < _PAIR_COUNTS[a], v1[a] + v2m[:SUBLANES], -jnp.inf))
    pieces.append(v1m[SUBLANES:] + v2[0])
    return jnp.concatenate(pieces, axis=0)


def _peer_score_kernel(ht_ref, wqt_ref, keys_ref, s_ref, st_ref):
    tm = ht_ref.shape[1]
    half = PEER_QDIM // 2
    for h in range(PEER_HEADS):
        scs = []
        for c in range(2):
            hc = 2 * h + c
            qt = jnp.dot(wqt_ref[hc * half:(hc + 1) * half, :], ht_ref[...], preferred_element_type=F32)
            sc = jnp.dot(keys_ref[h, c], qt, precision=HIGHEST, preferred_element_type=F32)
            s_ref[hc] = sc
            scs.append(sc)
        for ch in range(tm // LANES):
            cs = slice(ch * LANES, (ch + 1) * LANES)
            v1 = _extract_topk(scs[0][:, cs], PEER_TOPK)
            v2 = _extract_topk(scs[1][:, cs], PEER_TOPK)
            sums = _candidate_sums(v1, v2, LANES)
            best = _extract_topk(sums, PEER_TOPK)
            theta = best[-1]
            top = best[0]
            z = jnp.sum(jnp.where(sums >= theta, jnp.exp(sums - top), 0.0), axis=0, keepdims=True)
            zero = jnp.zeros((SUBLANES - 4, LANES), F32)
            st_ref[h, :, cs] = jnp.concatenate([theta, v1[0], v2[0], 1.0 / z, zero], axis=0)


def peer_scores(h2t, w_q, sub_keys, tm=256):
    d, t = h2t.shape
    wqt = w_q.T.astype(BF16)
    nk = PEER_KEYS
    return pl.pallas_call(
        _peer_score_kernel,
        grid=(t // tm,),
        in_specs=[pl.BlockSpec((d, tm), lambda i: (0, i)),
                  pl.BlockSpec(wqt.shape, lambda i: (0, 0)),
                  pl.BlockSpec(sub_keys.shape, lambda i: (0, 0, 0, 0))],
        out_specs=[pl.BlockSpec((2 * PEER_HEADS, nk, tm), lambda i: (0, 0, i)),
                   pl.BlockSpec((PEER_HEADS, SUBLANES, tm), lambda i: (0, 0, i))],
        out_shape=[jax.ShapeDtypeStruct((2 * PEER_HEADS, nk, t), F32),
                   jax.ShapeDtypeStruct((PEER_HEADS, SUBLANES, t), F32)],
        compiler_params=_params("parallel"),
        name="peer_scores",
    )(h2t, wqt, sub_keys)


PEER_TM = 512
PEER_TE = 1024


def _peer_kernel(ht_ref, u_ref, vt_ref, s_ref, st_ref, o_ref, e_scr, at_scr, gw_scr):
    j = pl.program_id(1)
    n_tiles = pl.num_programs(1) - 2
    nk = PEER_KEYS
    tm = ht_ref.shape[1]
    te = u_ref.shape[0]
    rpt = te // nk

    @pl.when(j == 0)
    def _():
        o_ref[...] = jnp.zeros(o_ref.shape, F32)
        at_scr[...] = jnp.zeros(at_scr.shape, F32)
        gw_scr[...] = jnp.zeros(gw_scr.shape, BF16)
        for h in range(PEER_HEADS):
            st = st_ref[h]
            e_scr[2 * h] = jnp.exp(s_ref[2 * h] - st[1:2]) * st[3:4]
            e_scr[2 * h + 1] = jnp.exp(s_ref[2 * h + 1] - st[2:3])

    cur = j % 2
    prv = (j + 1) % 2

    at_scr[cur] = jnp.dot(u_ref[...], ht_ref[...], preferred_element_type=F32)

    live = jnp.where((j >= 1) & (j <= n_tiles), 1.0, 0.0)
    jb = jnp.clip(j - 1, 0, n_tiles - 1)
    for rr in range(rpt):
        r = jb * rpt + rr
        s1_rows = [s_ref[2 * h, pl.ds(r, 1), :] for h in range(PEER_HEADS)]
        e1_rows = [e_scr[2 * h, pl.ds(r, 1), :] * live for h in range(PEER_HEADS)]
        for ch in range(tm // LANES):
            cs = slice(ch * LANES, (ch + 1) * LANES)
            w = jnp.zeros((nk, LANES), F32)
            for h in range(PEER_HEADS):
                keep = (s1_rows[h][:, cs] + s_ref[2 * h + 1, :, cs]) >= st_ref[h, 0:1, cs]
                w = w + jnp.where(keep, e1_rows[h][:, cs] * e_scr[2 * h + 1, :, cs], 0.0)
            a = at_scr[prv, rr * nk:(rr + 1) * nk, cs]
            gw_scr[cur, rr * nk:(rr + 1) * nk, cs] = (_gelu(a) * w).astype(BF16)

    o_ref[...] += jnp.dot(vt_ref[...], gw_scr[prv], preferred_element_type=F32)


def peer_experts(h2t, u_bf, vt_bf, scores, stats, tm=PEER_TM, te=PEER_TE):
    d, t = h2t.shape
    ne = u_bf.shape[0]
    n_tiles = ne // te
    return pl.pallas_call(
        _peer_kernel,
        grid=(t // tm, n_tiles + 2),
        in_specs=[pl.BlockSpec((d, tm), lambda i, j: (0, i)),
                  pl.BlockSpec((te, d), lambda i, j: (jnp.minimum(j, n_tiles - 1), 0)),
                  pl.BlockSpec((d, te), lambda i, j: (0, jnp.clip(j - 2, 0, n_tiles - 1))),
                  pl.BlockSpec((2 * PEER_HEADS, PEER_KEYS, tm), lambda i, j: (0, 0, i)),
                  pl.BlockSpec((PEER_HEADS, SUBLANES, tm), lambda i, j: (0, 0, i))],
        out_specs=pl.BlockSpec((d, tm), lambda i, j: (0, i)),
        out_shape=jax.ShapeDtypeStruct((d, t), F32),
        scratch_shapes=[pltpu.VMEM((2 * PEER_HEADS, PEER_KEYS, tm), F32),
                        pltpu.VMEM((2, te, tm), F32),
                        pltpu.VMEM((2, te, tm), BF16)],
        compiler_params=_params("parallel", "arbitrary"),
        name="peer_experts",
    )(h2t, u_bf, vt_bf, scores, stats)


def _final_kernel(x1_ref, pt_ref, g2_ref, fg_ref, o_ref):
    x2 = x1_ref[...] + g2_ref[0] * pt_ref[...].T
    ms = jnp.mean(x2 * x2, axis=-1, keepdims=True)
    o_ref[...] = (x2 * lax.rsqrt(ms + EPS)) * fg_ref[...]


def final_norm(x1, peer_t, g2, final_g, l, tm=512):
    t, d = x1.shape
    tpb = l // tm
    return pl.pallas_call(
        _final_kernel,
        grid=(t // tm,),
        in_specs=[pl.BlockSpec((tm, d), lambda i: (i, 0)),
                  pl.BlockSpec((d, tm), lambda i: (0, i)),
                  pl.BlockSpec((1, 1, d), lambda i: (i // tpb, 0, 0)),
                  pl.BlockSpec((1, d), lambda i: (0, 0))],
        out_specs=pl.BlockSpec((tm, d), lambda i: (i, 0)),
        out_shape=jax.ShapeDtypeStruct((t, d), F32),
        compiler_params=_params("parallel"),
        name="final_norm",
    )(x1, peer_t, g2, final_g.reshape(1, d))


def kernel(x, c, ada_w, ada_b, norm_mix_g, norm_ffn_g, w_in, cmp_pos, cmp_w1, cmp_b1, cmp_w2, cmp_b2, rel_bias,
           s5_lam_re, s5_lam_im, s5_log_step, s5_b_re, s5_b_im, s5_c_re, s5_c_im, s5_d, glu_w, glu_b,
           w_up_attn, w_up_ssm, w_out, peer_w_q, peer_sub_keys, peer_u, peer_v, final_g):
    bsz, l, d = x.shape
    assert ada_w.shape[0] == 1, "single-layer block"
    mod = adaln(c, ada_w[0], ada_b[0])
    sh1, sc1, g1, sh2, sc2, g2 = [m[:, None, :] for m in jnp.split(mod, 6, axis=-1)]

    proj = inproj(x, norm_mix_g[0], sc1, sh1, build_wcat(w_in[0]))

    n16 = l // CMP_STRIDE

    def blocks16(cols):
        return cols.reshape(bsz, l, NSA_KV_HEADS, HEAD_DIM).transpose(0, 2, 1, 3).reshape(
            bsz, NSA_KV_HEADS, n16, CMP_STRIDE * HEAD_DIM)

    cmp_kv = compress(blocks16(proj[:, :, COL_CMP:COL_CMP + KV_WIDTH]),
                      blocks16(proj[:, :, COL_CMP + KV_WIDTH:COL_CMP + 2 * KV_WIDTH]),
                      cmp_w1[0], cmp_pos[0], cmp_b1[0], cmp_w2[0], cmp_b2[0])
    diag, cmpb = attention_tables(rel_bias)
    o_attn = attention(proj, cmp_kv, diag, cmpb)

    tables = s5_tables(s5_lam_re[0], s5_lam_im[0], s5_log_step[0], s5_b_re[0], s5_b_im[0],
                       s5_c_re[0], s5_c_im[0])
    y_ssm = s5_glu(proj, tables, s5_d[0], glu_w[0], glu_b[0])

    x1, h2 = merge(x, o_attn, y_ssm, proj, g1, w_up_attn[0], w_up_ssm[0], w_out[0],
                   norm_ffn_g[0], sc2, sh2)

    t = bsz * l
    h2t = h2.reshape(t, d).T
    scores, stats = peer_scores(h2t, peer_w_q[0], peer_sub_keys[0])
    peer_t = peer_experts(h2t, peer_u[0].astype(BF16), peer_v[0].T.astype(BF16), scores, stats)
    out = final_norm(x1.reshape(t, d), peer_t, g2, final_g, l)
    return out.reshape(bsz, l, d)
```

```python
import functools
import math

import numpy as np
import jax
import jax.numpy as jnp
from jax import lax
from jax.experimental import pallas as pl
from jax.experimental.pallas import tpu as pltpu

F32 = jnp.float32
BF16 = jnp.bfloat16
HIGHEST = lax.Precision.HIGHEST

D_MODEL = 2048
NSA_HEADS = 16
NSA_KV_HEADS = 4
NSA_GROUP = NSA_HEADS // NSA_KV_HEADS
HEAD_DIM = 64
NSA_WIDTH = NSA_HEADS * HEAD_DIM
KV_WIDTH = NSA_KV_HEADS * HEAD_DIM
CMP_LEN = 32
CMP_STRIDE = 16
CMP_HIDDEN = 256
SEL_BLOCK = 64
SEL_TOP = 16
WINDOW = 512
S5_WIDTH = 1024
S5_GROUP = 16
S5_GROUPS = S5_WIDTH // S5_GROUP
S5_STATE = 64
S5_FLAT = S5_GROUPS * S5_STATE
REL_BUCKETS = 32
REL_MAX_DIST = 128
PEER_KEYS = 128
PEER_EXPERTS = PEER_KEYS * PEER_KEYS
PEER_HEADS = 8
PEER_TOPK = 16
PEER_QDIM = 256
EPS = 1e-6
NEG_INF = -1e30
FORCE_BONUS = 1e4

VMEM_LIMIT_BYTES = 56 * 1024 * 1024
LANES = 128
SUBLANES = 8

PROJ_WIDTH = 8192
COL_Q = 0
COL_U = 1024
COL_MERGE = 2048
COL_SEL = 6144
COL_WIN = 6656
COL_CMP = 7168
COL_GATES = 7680

ATT_TILE = 128
BIAS_SPAN = 256
ATT_SUPER = 512
ATT_ORIGIN = 2 * ATT_SUPER - ATT_TILE
ATT_TABLE_ROWS = ATT_ORIGIN + WINDOW + ATT_TILE


def _params(*sem):
    return pltpu.CompilerParams(dimension_semantics=sem, vmem_limit_bytes=VMEM_LIMIT_BYTES)


def _gelu(x):
    return 0.5 * x * (1.0 + jnp.tanh(math.sqrt(2.0 / math.pi) * (x + 0.044715 * (x * x * x))))


def _sigmoid(x):
    return 1.0 / (1.0 + jnp.exp(-x))


def _adaln_kernel(c_ref, w_ref, b_ref, o_ref):
    c = c_ref[...]
    a = c * _sigmoid(c)
    o_ref[...] = jnp.dot(a, w_ref[...], precision=HIGHEST, preferred_element_type=F32) + b_ref[...]


def adaln(c, w, b):
    bsz, d = c.shape
    n = w.shape[1]
    tn = 1024
    cp = jnp.zeros((SUBLANES, d), F32).at[:bsz].set(c)
    out = pl.pallas_call(
        _adaln_kernel,
        grid=(n // tn,),
        in_specs=[pl.BlockSpec((SUBLANES, d), lambda j: (0, 0)),
                  pl.BlockSpec((d, tn), lambda j: (0, j)),
                  pl.BlockSpec((1, tn), lambda j: (0, j))],
        out_specs=pl.BlockSpec((SUBLANES, tn), lambda j: (0, j)),
        out_shape=jax.ShapeDtypeStruct((SUBLANES, n), F32),
        compiler_params=_params("parallel"),
        name="adaln",
    )(cp, w, b.reshape(1, n))
    return out[:bsz]


def _modulated_norm(x, g, sc, sh):
    ms = jnp.mean(x * x, axis=-1, keepdims=True)
    return (x * lax.rsqrt(ms + EPS)) * g * (1.0 + sc) + sh


def _inproj_kernel(x_ref, g_ref, sc_ref, sh_ref, w_ref, o_ref, h_scr):
    @pl.when(pl.program_id(2) == 0)
    def _():
        h_scr[...] = _modulated_norm(x_ref[0], g_ref[...], sc_ref[0], sh_ref[0]).astype(BF16)

    o_ref[0] = jnp.dot(h_scr[...], w_ref[...], preferred_element_type=F32).astype(BF16)


def inproj(x, g, sc, sh, wcat, tm=1024, tn=1024):
    bsz, l, d = x.shape
    n = wcat.shape[1]
    tm = min(tm, l)
    return pl.pallas_call(
        _inproj_kernel,
        grid=(bsz, l // tm, n // tn),
        in_specs=[pl.BlockSpec((1, tm, d), lambda b, i, j: (b, i, 0)),
                  pl.BlockSpec((1, d), lambda b, i, j: (0, 0)),
                  pl.BlockSpec((1, 1, d), lambda b, i, j: (b, 0, 0)),
                  pl.BlockSpec((1, 1, d), lambda b, i, j: (b, 0, 0)),
                  pl.BlockSpec((d, tn), lambda b, i, j: (0, j))],
        out_specs=pl.BlockSpec((1, tm, tn), lambda b, i, j: (b, i, j)),
        out_shape=jax.ShapeDtypeStruct((bsz, l, n), BF16),
        scratch_shapes=[pltpu.VMEM((tm, d), BF16)],
        compiler_params=_params("parallel", "parallel", "arbitrary"),
        name="inproj",
    )(x, g.reshape(1, d), sc, sh, wcat)


def build_wcat(w_in):
    d = w_in.shape[0]
    pts = np.cumsum([NSA_WIDTH] + [KV_WIDTH] * 6 + [3 * NSA_HEADS, S5_WIDTH])
    q, kc, vc, ks, vs, kw, vw, gates, u, merge = jnp.split(w_in, pts, axis=1)

    def pair(k, v):
        return jnp.stack([k.reshape(d, NSA_KV_HEADS, HEAD_DIM), v.reshape(d, NSA_KV_HEADS, HEAD_DIM)],
                         axis=2).reshape(d, 2 * KV_WIDTH)

    gates_p = jnp.pad(gates, ((0, 0), (0, LANES - 3 * NSA_HEADS)))
    pad = jnp.zeros((d, PROJ_WIDTH - COL_GATES - LANES), w_in.dtype)
    wcat = jnp.concatenate([q * (HEAD_DIM ** -0.5), u, merge, pair(ks, vs), pair(kw, vw),
                            kc, vc, gates_p, pad], axis=1)
    return wcat.astype(BF16)


def _compress_kernel(gk_ref, gv_ref, w1_ref, pos_ref, b1_ref, w2_ref, b2_ref, o_ref):
    half = CMP_STRIDE * HEAD_DIM
    nrow = gk_ref.shape[2]
    out = b2_ref[...]
    for j, g_ref in enumerate((gk_ref, gv_ref)):
        g = g_ref[0, 0]
        w1 = w1_ref[j]
        top = jnp.dot(g, w1[:half].astype(BF16), preferred_element_type=F32)
        bot = jnp.dot(g, w1[half:].astype(BF16), preferred_element_type=F32)
        cst = jnp.dot(pos_ref[j], w1, precision=HIGHEST, preferred_element_type=F32)[0:1] + b1_ref[j]
        hid = top + pltpu.roll(bot, nrow - 1, axis=0) + cst
        out = out + jnp.dot(_gelu(hid).astype(BF16), w2_ref[j].astype(BF16), preferred_element_type=F32)
    o_ref[0, 0] = out.astype(BF16)


def compress(gk, gv, w1, pos, b1, w2, b2):
    bsz, hkv, n16, wd = gk.shape
    posf = jnp.zeros((2, SUBLANES, CMP_LEN * HEAD_DIM), F32).at[:, 0].set(pos.reshape(2, -1))
    w2p = jnp.zeros((2, CMP_HIDDEN, 2 * HEAD_DIM), F32)
    w2p = w2p.at[0, :, :HEAD_DIM].set(w2[0]).at[1, :, HEAD_DIM:].set(w2[1])
    b2p = jnp.concatenate([b2[0], b2[1]]).reshape(1, 2 * HEAD_DIM)
    gspec = pl.BlockSpec((1, 1, n16, wd), lambda b, h: (b, h, 0, 0))
    full = lambda a: pl.BlockSpec(a.shape, lambda b, h: (0,) * a.ndim)
    b1r = b1.reshape(2, 1, CMP_HIDDEN)
    return pl.pallas_call(
        _compress_kernel,
        grid=(bsz, hkv),
        in_specs=[gspec, gspec, full(w1), full(posf), full(b1r), full(w2p), full(b2p)],
        out_specs=pl.BlockSpec((1, 1, n16, 2 * HEAD_DIM), lambda b, h: (b, h, 0, 0)),
        out_shape=jax.ShapeDtypeStruct((bsz, hkv, n16, 2 * HEAD_DIM), BF16),
        compiler_params=_params("parallel", "parallel"),
        name="compress",
    )(gk, gv, w1, posf, b1r, w2p, b2p)


def _t5_bucket_table(n):
    dist = np.arange(n)
    max_exact = REL_BUCKETS // 2
    ratio = np.log(np.maximum(dist, 1).astype(np.float32) / np.float32(max_exact)) / np.float32(
        math.log(REL_MAX_DIST / max_exact))
    large = np.minimum(max_exact + (ratio * np.float32(REL_BUCKETS - max_exact)).astype(np.int32), REL_BUCKETS - 1)
    return np.where(dist < max_exact, dist, large)


def _cmp_to_sel_t(n_cmp, n_sel):
    cs = np.arange(n_cmp)[:, None] * CMP_STRIDE
    ss = np.arange(n_sel)[None, :] * SEL_BLOCK
    ov = np.clip(np.minimum(cs + CMP_LEN, ss + SEL_BLOCK) - np.maximum(cs, ss), 0, None)
    return np.ascontiguousarray((ov / CMP_LEN).T.astype(np.float32))


def attention_tables(rel_bias):
    tq = ATT_TILE
    bucket = _t5_bucket_table(BIAS_SPAN)
    assert np.all(bucket[113:] == REL_BUCKETS - 1)
    bv = (rel_bias[bucket] - rel_bias[REL_BUCKETS - 1][None, :]).T
    heads = bv.shape[0]
    band0 = ATT_ORIGIN - BIAS_SPAN
    band = BIAS_SPAN + tq
    period = band + tq
    vext = jnp.concatenate([jnp.zeros((heads, period - BIAS_SPAN), F32), bv.astype(F32)], axis=1)
    toep = jnp.tile(vext, (1, band))[:, :band * (period - 1)].reshape(heads, band, period - 1)[:, :, :tq]
    tab = jnp.pad(toep, ((0, 0), (band0, ATT_TABLE_ROWS - band0 - band), (0, 0)))
    tab = tab.reshape(NSA_KV_HEADS, NSA_GROUP, ATT_TABLE_ROWS, tq).transpose(0, 2, 1, 3)
    tab = tab.reshape(NSA_KV_HEADS, ATT_TABLE_ROWS, NSA_GROUP * tq)
    lo = 16 * CMP_STRIDE - (8 * CMP_STRIDE - CMP_LEN + 1)
    fpad = jnp.pad(bv.astype(F32), ((0, 0), (lo, tq)))
    cmpb = jnp.stack([fpad[:, lo + (8 * CMP_STRIDE - CMP_LEN + 1) - CMP_STRIDE * r:][:, :tq] for r in range(16)],
                     axis=1)
    cmpb = cmpb.reshape(NSA_KV_HEADS, NSA_GROUP, 16, tq).transpose(0, 2, 1, 3).reshape(NSA_KV_HEADS, 16, NSA_GROUP * tq)
    return tab, cmpb


def _attn_kernel(q_ref, sel_ref, selt_ref, win_ref, wint_ref, cmp_ref, cmpt_ref, gate_ref, tab_ref, cmpb_ref,
                 mt_ref, o_ref, m_scr, l_scr, acc_scr, sel_scr):
    tq = ATT_TILE
    sk = ATT_SUPER
    cols = NSA_GROUP * tq
    kvh = pl.program_id(1)
    qt = pl.program_id(2)
    n_cmp = cmp_ref.shape[2]
    n_sel = mt_ref.shape[0]
    per = sk // tq

    qT = q_ref[0]
    zpad = jnp.zeros((HEAD_DIM, tq), BF16)
    qpad = jnp.concatenate(
        [jnp.concatenate([qT[g * HEAD_DIM:(g + 1) * HEAD_DIM], zpad], axis=0) for g in range(NSA_GROUP)],
        axis=1)
    col_i = lax.broadcasted_iota(jnp.int32, (1, cols), 1) & (tq - 1)
    t_col = qt * tq + col_i

    ckv = cmp_ref[0, 0]
    s = jnp.dot(ckv, qpad, preferred_element_type=F32)
    row_n = lax.broadcasted_iota(jnp.int32, (n_cmp, 16), 0)
    col_r = lax.broadcasted_iota(jnp.int32, (n_cmp, 16), 1)
    shift = (row_n == qt * (tq // CMP_STRIDE) - 8 + col_r).astype(F32)
    s = s + jnp.dot(shift, cmpb_ref[0], precision=HIGHEST, preferred_element_type=F32)
    n_idx = lax.broadcasted_iota(jnp.int32, (n_cmp, cols), 0)
    valid = (n_idx * CMP_STRIDE + (CMP_LEN - 1)) <= t_col
    s = jnp.where(valid, s, NEG_INF)
    mx = jnp.max(s, axis=0, keepdims=True)
    p = jnp.where(valid, jnp.exp(s - mx), 0.0)
    den = jnp.sum(p, axis=0, keepdims=True)
    p = p * jnp.where(den > 0.0, 1.0 / den, 0.0)
    o_c = jnp.dot(cmpt_ref[0, 0], p.astype(BF16), preferred_element_type=F32)

    psum = p[:, 0:tq] + p[:, tq:2 * tq] + p[:, 2 * tq:3 * tq] + p[:, 3 * tq:4 * tq]
    imp = jnp.dot(mt_ref[...], psum, precision=HIGHEST, preferred_element_type=F32)
    blk = lax.broadcasted_iota(jnp.int32, (n_sel, tq), 0)
    qi = lax.broadcasted_iota(jnp.int32, (n_sel, tq), 1)
    cur = jnp.right_shift(qt * tq + qi, 6)
    forced = (blk == 0) | (blk == cur) | (blk == cur - 1)
    imp = jnp.where(blk <= cur, imp + FORCE_BONUS * forced.astype(F32), NEG_INF)
    rank = jnp.zeros((n_sel, tq), F32)
    for k in range(n_sel):
        rk = imp[k:k + 1, :]
        beats = (rk > imp) | ((rk == imp) & (blk > k))
        rank = rank + beats.astype(F32)
    sel_scr[...] = jnp.where(rank < float(SEL_TOP), 1.0, 0.0)

    m_scr[...] = jnp.full(m_scr.shape, NEG_INF, F32)
    l_scr[...] = jnp.zeros(l_scr.shape, F32)
    acc_scr[...] = jnp.zeros(acc_scr.shape, F32)
    st_last = qt // per
    dq = qt - st_last * per

    def sel_tile(st, table_row, causal_shift):
        kv = sel_ref[0, pl.ds(pl.multiple_of(st * sk, sk), sk), :]
        sc = jnp.dot(kv, qpad, preferred_element_type=F32)
        if table_row is not None:
            sc = sc + tab_ref[0, pl.ds(pl.multiple_of(table_row, tq), sk), :]
        blocks = sk // SEL_BLOCK
        rows = [jnp.broadcast_to(sel_scr[pl.ds(st * blocks + b, 1), :], (SEL_BLOCK, tq)) for b in range(blocks)]
        one = jnp.concatenate(rows, axis=0) > 0.5
        keep = jnp.concatenate([one] * NSA_GROUP, axis=1)
        if causal_shift is not None:
            rel = col_i - lax.broadcasted_iota(jnp.int32, (sk, cols), 0)
            keep = keep & (rel + causal_shift >= 0)
        sc = jnp.where(keep, sc, NEG_INF)
        m_old = m_scr[...]
        m_new = jnp.maximum(m_old, jnp.max(sc, axis=0, keepdims=True))
        alpha = jnp.exp(m_old - m_new)
        pe = jnp.exp(sc - m_new)
        l_scr[...] = alpha * l_scr[...] + jnp.sum(pe, axis=0, keepdims=True)
        acc_scr[...] = alpha * acc_scr[...] + jnp.dot(selt_ref[0, 0, st], pe.astype(BF16),
                                                      preferred_element_type=F32)
        m_scr[...] = m_new

    def sel_body(st, carry):
        sel_tile(st, None, None)
        return carry

    lax.fori_loop(0, jnp.maximum(st_last - 1, 0), sel_body, 0)

    @pl.when(st_last >= 1)
    def _():
        sel_tile(st_last - 1, ATT_ORIGIN - sk - dq * tq, None)

    sel_tile(st_last, ATT_ORIGIN - dq * tq, dq * tq)
    o_s = acc_scr[HEAD_DIM:, :] * (1.0 / l_scr[...])

    wk = WINDOW + tq
    n_w = wk // tq
    k0 = jnp.maximum(qt - WINDOW // tq, 0)
    dw = (qt - k0) * tq
    kv = win_ref[0, pl.ds(pl.multiple_of(k0 * tq, tq), wk), :]
    sc = jnp.dot(kv, qpad, preferred_element_type=F32)
    sc = sc + tab_ref[0, pl.ds(pl.multiple_of(ATT_ORIGIN - dw, tq), wk), :]
    dist = col_i - lax.broadcasted_iota(jnp.int32, (wk, cols), 0) + dw
    sc = jnp.where((dist >= 0) & (dist < WINDOW), sc, NEG_INF)
    mw = jnp.max(sc, axis=0, keepdims=True)
    pw = jnp.exp(sc - mw)
    lw = jnp.sum(pw, axis=0, keepdims=True)
    pw = pw.astype(BF16)
    ow = jnp.dot(wint_ref[0, 0, k0], pw[0:tq], preferred_element_type=F32)
    for m in range(1, n_w):
        ow = ow + jnp.dot(wint_ref[0, 0, k0 + m], pw[m * tq:(m + 1) * tq], preferred_element_type=F32)
    o_w = ow[HEAD_DIM:, :] * (1.0 / lw)

    branches = (o_c[HEAD_DIM:, :], o_s, o_w)
    outs = []
    for g in range(NSA_GROUP):
        acc = jnp.zeros((HEAD_DIM, tq), F32)
        for br in range(3):
            grow = gate_ref[0, pl.ds((kvh * NSA_GROUP + g) * 3 + br, 1), :]
            acc = acc + _sigmoid(grow) * branches[br][:, g * tq:(g + 1) * tq]
        outs.append(acc)
    for pr in range(NSA_GROUP // 2):
        pair = jnp.concatenate([outs[2 * pr], outs[2 * pr + 1]], axis=0)
        o_ref[0, :, pr * LANES:(pr + 1) * LANES] = pair.T.astype(BF16)


def attention(proj, cmp_kv, table, cmpb_t):
    bsz, l, _ = proj.shape
    tq = ATT_TILE
    sk = ATT_SUPER
    assert l % sk == 0 and l >= WINDOW + tq
    n_kt = l // tq
    n_st = l // sk
    n_cmp = cmp_kv.shape[2]
    n_sel = l // SEL_BLOCK
    mt = jnp.asarray(np.pad(_cmp_to_sel_t(n_cmp - 1, n_sel), ((0, 0), (0, 1))))
    gw = NSA_GROUP * HEAD_DIM
    kvw = 2 * KV_WIDTH
    q_t = proj[:, :, COL_Q:COL_Q + NSA_WIDTH].transpose(0, 2, 1)
    tiles_t = lambda c0, nt: proj[:, :, c0:c0 + kvw].reshape(bsz, nt, l // nt, NSA_KV_HEADS, LANES).transpose(0, 3, 1, 4, 2)
    sel_t = tiles_t(COL_SEL, n_st)
    win_t = tiles_t(COL_WIN, n_kt)
    cmp_t = cmp_kv.transpose(0, 1, 3, 2)
    gates_t = proj[:, :, COL_GATES:COL_GATES + LANES].astype(F32).transpose(0, 2, 1)
    return pl.pallas_call(
        _attn_kernel,
        grid=(bsz, NSA_KV_HEADS, n_kt),
        in_specs=[pl.BlockSpec((1, gw, tq), lambda b, h, i: (b, h, i)),
                  pl.BlockSpec((1, l, LANES), lambda b, h, i: (b, 0, COL_SEL // LANES + h)),
                  pl.BlockSpec((1, 1, n_st, LANES, sk), lambda b, h, i: (b, h, 0, 0, 0)),
                  pl.BlockSpec((1, l, LANES), lambda b, h, i: (b, 0, COL_WIN // LANES + h)),
                  pl.BlockSpec((1, 1, n_kt, LANES, tq), lambda b, h, i: (b, h, 0, 0, 0)),
                  pl.BlockSpec((1, 1, n_cmp, LANES), lambda b, h, i: (b, h, 0, 0)),
                  pl.BlockSpec((1, 1, LANES, n_cmp), lambda b, h, i: (b, h, 0, 0)),
                  pl.BlockSpec((1, LANES, tq), lambda b, h, i: (b, 0, i)),
                  pl.BlockSpec((1, ATT_TABLE_ROWS, NSA_GROUP * tq), lambda b, h, i: (h, 0, 0)),
                  pl.BlockSpec((1, 16, NSA_GROUP * tq), lambda b, h, i: (h, 0, 0)),
                  pl.BlockSpec((n_sel, n_cmp), lambda b, h, i: (0, 0))],
        out_specs=pl.BlockSpec((1, tq, gw), lambda b, h, i: (b, i, h)),
        out_shape=jax.ShapeDtypeStruct((bsz, l, NSA_WIDTH), BF16),
        scratch_shapes=[pltpu.VMEM((1, NSA_GROUP * tq), F32),
                        pltpu.VMEM((1, NSA_GROUP * tq), F32),
                        pltpu.VMEM((LANES, NSA_GROUP * tq), F32),
                        pltpu.VMEM((n_sel, tq), F32)],
        compiler_params=_params("parallel", "parallel", "arbitrary"),
        name="nsa_attention",
    )(q_t, proj, sel_t, proj, win_t, cmp_kv, cmp_t, gates_t, table, cmpb_t, mt)


S5_BLOCK = 256
S5_CHUNK = S5_BLOCK // SUBLANES
S5_SLAB = 1024


def _s5_kernel(u_ref, perm_ref, permt_ref, bre_ref, bim_ref, are_ref, aim_ref, ace_ref, aci_ref, cre_ref, cim_ref,
               d_ref, gw_ref, gb_ref, o_ref, bu_re, bu_im, xs_re, xs_im, carry_re, carry_im, init_re, init_im):
    nr = S5_GROUPS // 16
    rw = 16 * S5_GROUP
    sw = 16 * S5_STATE

    @pl.when(pl.program_id(1) == 0)
    def _():
        carry_re[...] = jnp.zeros(carry_re.shape, F32)
        carry_im[...] = jnp.zeros(carry_im.shape, F32)

    perm = perm_ref[...]
    up = jnp.dot(perm, u_ref[0], preferred_element_type=F32).astype(BF16)
    for r in range(nr):
        ur = up[:, r * rw:(r + 1) * rw]
        bu_re[:, r * sw:(r + 1) * sw] = jnp.dot(ur, bre_ref[r], preferred_element_type=F32)
        bu_im[:, r * sw:(r + 1) * sw] = jnp.dot(ur, bim_ref[r], preferred_element_type=F32)

    nslab = S5_FLAT // S5_SLAB
    row8 = lax.broadcasted_iota(jnp.int32, (SUBLANES, S5_SLAB), 0)
    for sl in range(nslab):
        cs = slice(sl * S5_SLAB, (sl + 1) * S5_SLAB)
        a_re = jnp.broadcast_to(are_ref[:, cs], (SUBLANES, S5_SLAB))
        a_im = jnp.broadcast_to(aim_ref[:, cs], (SUBLANES, S5_SLAB))

        def step1(s, st):
            s_re, s_im = st
            off = pl.multiple_of(s * SUBLANES, SUBLANES)
            n_re = a_re * s_re - a_im * s_im + bu_re[pl.ds(off, SUBLANES), cs]
            n_im = a_re * s_im + a_im * s_re + bu_im[pl.ds(off, SUBLANES), cs]
            return n_re, n_im

        zero = jnp.zeros((SUBLANES, S5_SLAB), F32)
        e_re, e_im = lax.fori_loop(0, S5_CHUNK, step1, (zero, zero))

        ac_re = ace_ref[:, cs]
        ac_im = aci_ref[:, cs]
        x_re = carry_re[:, cs]
        x_im = carry_im[:, cs]
        st_re = jnp.zeros((SUBLANES, S5_SLAB), F32)
        st_im = jnp.zeros((SUBLANES, S5_SLAB), F32)
        for c in range(SUBLANES):
            st_re = jnp.where(row8 == c, x_re, st_re)
            st_im = jnp.where(row8 == c, x_im, st_im)
            n_re = ac_re * x_re - ac_im * x_im + e_re[c:c + 1]
            n_im = ac_re * x_im + ac_im * x_re + e_im[c:c + 1]
            x_re, x_im = n_re, n_im
        carry_re[:, cs] = x_re
        carry_im[:, cs] = x_im
        init_re[:, cs] = st_re
        init_im[:, cs] = st_im

        def step2(s, st):
            s_re, s_im = st
            off = pl.multiple_of(s * SUBLANES, SUBLANES)
            n_re = a_re * s_re - a_im * s_im + bu_re[pl.ds(off, SUBLANES), cs]
            n_im = a_re * s_im + a_im * s_re + bu_im[pl.ds(off, SUBLANES), cs]
            xs_re[pl.ds(off, SUBLANES), cs] = n_re
            xs_im[pl.ds(off, SUBLANES), cs] = n_im
            return n_re, n_im

        lax.fori_loop(0, S5_CHUNK, step2, (st_re, st_im))

    ys = []
    for r in range(nr):
        xr = xs_re[:, r * sw:(r + 1) * sw].astype(BF16)
        xi = xs_im[:, r * sw:(r + 1) * sw].astype(BF16)
        ys.append(jnp.dot(xr, cre_ref[r], preferred_element_type=F32)
                  - jnp.dot(xi, cim_ref[r], preferred_element_type=F32))
    y = jnp.concatenate(ys, axis=1) + d_ref[...] * up.astype(F32)
    y = _gelu(y)
    z = jnp.dot(y.astype(BF16), gw_ref[...], preferred_element_type=F32) + gb_ref[...]
    y = (y * _sigmoid(z)).astype(BF16)
    o_ref[0] = jnp.dot(permt_ref[...], y, preferred_element_type=F32).astype(BF16)


def s5_tables(lam_re, lam_im, log_step, b_re, b_im, c_re, c_im):
    f32 = F32
    lam_re, lam_im = lam_re.astype(f32), lam_im.astype(f32)
    step = jnp.exp(log_step.astype(f32))[:, None]
    mag = jnp.exp(lam_re * step)
    lb_re, lb_im = mag * jnp.cos(lam_im * step), mag * jnp.sin(lam_im * step)
    den = lam_re * lam_re + lam_im * lam_im
    n_re = lb_re - 1.0
    f_re = (n_re * lam_re + lb_im * lam_im) / den
    f_im = (lb_im * lam_re - n_re * lam_im) / den
    b_re, b_im = b_re.astype(f32), b_im.astype(f32)
    bb_re = f_re[..., None] * b_re - f_im[..., None] * b_im
    bb_im = f_re[..., None] * b_im + f_im[..., None] * b_re
    magc = jnp.exp(lam_re * step * S5_CHUNK)
    ac_re, ac_im = magc * jnp.cos(lam_im * step * S5_CHUNK), magc * jnp.sin(lam_im * step * S5_CHUNK)

    nr = S5_GROUPS // 16
    eye = jnp.eye(16, dtype=f32)

    def b_blocks(bb):
        bb = bb.reshape(nr, 16, S5_STATE, S5_GROUP)
        return jnp.einsum('rgph,gk->rghkp', bb, eye).reshape(nr, 16 * S5_GROUP, 16 * S5_STATE).astype(BF16)

    def c_blocks(cc):
        cc = cc.astype(f32).reshape(nr, 16, S5_GROUP, S5_STATE)
        return jnp.einsum('rghp,gk->rgpkh', cc, eye).reshape(nr, 16 * S5_STATE, 16 * S5_GROUP).astype(BF16)

    flat = lambda a: a.reshape(1, S5_FLAT)
    return (b_blocks(bb_re), b_blocks(bb_im), flat(lb_re), flat(lb_im), flat(ac_re), flat(ac_im),
            c_blocks(c_re), c_blocks(c_im))


def _s5_perm():
    p = np.zeros((S5_BLOCK, S5_BLOCK), np.float32)
    for c in range(SUBLANES):
        for s in range(S5_CHUNK):
            p[s * SUBLANES + c, c * S5_CHUNK + s] = 1.0
    return p


def s5_glu(proj, tables, d_skip, glu_w, glu_b):
    bsz, l, _ = proj.shape
    tb = S5_BLOCK
    bre, bim, are, aim, ace, aci, cre, cim = tables
    perm = jnp.asarray(_s5_perm(), BF16)
    permt = jnp.asarray(_s5_perm().T, BF16)
    full = lambda a: pl.BlockSpec(a.shape, lambda b, i: (0,) * a.ndim)
    d2 = d_skip.reshape(1, S5_WIDTH).astype(F32)
    gb2 = glu_b.reshape(1, S5_WIDTH).astype(F32)
    gw = glu_w.astype(BF16)
    return pl.pallas_call(
        _s5_kernel,
        grid=(bsz, l // tb),
        in_specs=[pl.BlockSpec((1, tb, S5_WIDTH), lambda b, i: (b, i, COL_U // S5_WIDTH)),
                  full(perm), full(permt), full(bre), full(bim), full(are), full(aim), full(ace), full(aci),
                  full(cre), full(cim), full(d2), full(gw), full(gb2)],
        out_specs=pl.BlockSpec((1, tb, S5_WIDTH), lambda b, i: (b, i, 0)),
        out_shape=jax.ShapeDtypeStruct((bsz, l, S5_WIDTH), BF16),
        scratch_shapes=[pltpu.VMEM((tb, S5_FLAT), F32), pltpu.VMEM((tb, S5_FLAT), F32),
                        pltpu.VMEM((tb, S5_FLAT), F32), pltpu.VMEM((tb, S5_FLAT), F32),
                        pltpu.VMEM((1, S5_FLAT), F32), pltpu.VMEM((1, S5_FLAT), F32),
                        pltpu.VMEM((SUBLANES, S5_FLAT), F32), pltpu.VMEM((SUBLANES, S5_FLAT), F32)],
        compiler_params=_params("parallel", "arbitrary"),
        name="s5_glu",
    )(proj, perm, permt, bre, bim, are, aim, ace, aci, cre, cim, d2, gw, gb2)


def _merge_kernel(x_ref, oa_ref, ys_ref, ga_ref, gb_ref, g1_ref, wa_ref, ws_ref, wo_ref,
                  ng_ref, sc_ref, sh_ref, x1_ref, h2_ref):
    a = jnp.dot(oa_ref[0], wa_ref[...], preferred_element_type=F32)
    s = jnp.dot(ys_ref[0], ws_ref[...], preferred_element_type=F32)
    mixed = _sigmoid(ga_ref[0].astype(F32)) * a + _sigmoid(gb_ref[0].astype(F32)) * s
    out = jnp.dot(mixed.astype(BF16), wo_ref[...], preferred_element_type=F32)
    x1 = x_ref[0] + g1_ref[0] * out
    x1_ref[0] = x1
    h2_ref[0] = _modulated_norm(x1, ng_ref[...], sc_ref[0], sh_ref[0]).astype(BF16)


def merge(x, o_attn, y_ssm, proj, g1, w_up_attn, w_up_ssm, w_out, norm_g, sc2, sh2, tm=256):
    bsz, l, d = x.shape
    tok = lambda w, cb=0: pl.BlockSpec((1, tm, w), lambda b, i: (b, i, cb))
    vec = pl.BlockSpec((1, 1, d), lambda b, i: (b, 0, 0))
    full = lambda a: pl.BlockSpec(a.shape, lambda b, i: (0,) * a.ndim)
    wa, ws, wo = w_up_attn.astype(BF16), w_up_ssm.astype(BF16), w_out.astype(BF16)
    ng = norm_g.reshape(1, d)
    return pl.pallas_call(
        _merge_kernel,
        grid=(bsz, l // tm),
        in_specs=[tok(d), tok(NSA_WIDTH), tok(S5_WIDTH), tok(d, COL_MERGE // d), tok(d, COL_MERGE // d + 1),
                  vec, full(wa), full(ws), full(wo), full(ng), vec, vec],
        out_specs=[tok(d), tok(d)],
        out_shape=[jax.ShapeDtypeStruct((bsz, l, d), F32), jax.ShapeDtypeStruct((bsz, l, d), BF16)],
        compiler_params=_params("parallel", "parallel"),
        name="merge",
    )(x, o_attn, y_ssm, proj, proj, g1, wa, ws, wo, ng, sc2, sh2)


def _extract_topk(vals, k):
    out = []
    rem = vals
    for _ in range(k):
        m = jnp.max(rem, axis=0, keepdims=True)
        out.append(m)
        rem = jnp.where(rem == m, -jnp.inf, rem)
    return out


PEER_RANKS = PEER_TOPK + 1
_PAIR_COUNTS = [PEER_RANKS // (a + 1) for a in range(PEER_RANKS)]


def _pad_rows(x, rows):
    return jnp.concatenate([x, jnp.full((rows - x.shape[0], x.shape[1]), -jnp.inf, F32)], axis=0)


def _candidate_sums(v1, v2, width):
    v1m = jnp.concatenate(v1, axis=0)
    v2m = jnp.concatenate(v2, axis=0)
    row = lax.broadcasted_iota(jnp.int32, (SUBLANES, width), 0)
    pieces = [_pad_rows(v1[0] + v2m, 3 * SUBLANES)]
    for a in range(1, SUBLANES):
        pieces.append(jnp.where(row < _PAIR_COUNTS[a], v1[a] + v2m[:SUBLANES], -jnp.inf))
    pieces.append(_pad_rows(v1m[SUBLANES:] + v2[0], 2 * SUBLANES))
    return jnp.concatenate(pieces, axis=0)


def _peer_score_kernel(ht_ref, wqt_ref, keys_ref, s_ref, st_ref):
    tm = ht_ref.shape[1]
    half = PEER_QDIM // 2
    for h in range(PEER_HEADS):
        scs = []
        for c in range(2):
            hc = 2 * h + c
            qt = jnp.dot(wqt_ref[hc * half:(hc + 1) * half, :], ht_ref[...], preferred_element_type=F32)
            sc = jnp.dot(keys_ref[h, c], qt, precision=HIGHEST, preferred_element_type=F32)
            s_ref[hc] = sc
            scs.append(sc)
        for ch in range(tm // LANES):
            cs = slice(ch * LANES, (ch + 1) * LANES)
            v1 = _extract_topk(scs[0][:, cs], PEER_RANKS)
            v2 = _extract_topk(scs[1][:, cs], PEER_RANKS)
            sums = _candidate_sums(v1, v2, LANES)
            best = _extract_topk(sums, PEER_RANKS)
            theta = 0.5 * (best[PEER_TOPK - 1] + best[PEER_TOPK])
            top = best[0]
            z = jnp.sum(jnp.where(sums >= theta, jnp.exp(sums - top), 0.0), axis=0, keepdims=True)
            zero = jnp.zeros((SUBLANES - 4, LANES), F32)
            st_ref[h, :, cs] = jnp.concatenate([theta, v1[0], v2[0], 1.0 / z, zero], axis=0)


def peer_scores(h2t, w_q, sub_keys, tm=256):
    d, t = h2t.shape
    wqt = w_q.T.astype(BF16)
    nk = PEER_KEYS
    return pl.pallas_call(
        _peer_score_kernel,
        grid=(t // tm,),
        in_specs=[pl.BlockSpec((d, tm), lambda i: (0, i)),
                  pl.BlockSpec(wqt.shape, lambda i: (0, 0)),
                  pl.BlockSpec(sub_keys.shape, lambda i: (0, 0, 0, 0))],
        out_specs=[pl.BlockSpec((2 * PEER_HEADS, nk, tm), lambda i: (0, 0, i)),
                   pl.BlockSpec((PEER_HEADS, SUBLANES, tm), lambda i: (0, 0, i))],
        out_shape=[jax.ShapeDtypeStruct((2 * PEER_HEADS, nk, t), F32),
                   jax.ShapeDtypeStruct((PEER_HEADS, SUBLANES, t), F32)],
        compiler_params=_params("parallel"),
        name="peer_scores",
    )(h2t, wqt, sub_keys)


PEER_TM = 512
PEER_TE = 1024
PEER_MXU_TILE = 256


def _peer_kernel(ht_ref, u_ref, vt_ref, s_ref, st_ref, o_ref, e_scr, thr_scr, e1_scr, at_a, at_b, gw_a, gw_b):
    j = pl.program_id(1)
    n_tiles = pl.num_programs(1) - 2
    nk = PEER_KEYS
    d, tm = ht_ref.shape
    te = u_ref.shape[0]
    rpt = te // nk
    mx = PEER_MXU_TILE

    @pl.when(j == 0)
    def _():
        o_ref[...] = jnp.zeros(o_ref.shape, F32)
        for buf in (at_a, at_b, gw_a, gw_b):
            buf[...] = jnp.zeros(buf.shape, buf.dtype)
        for h in range(PEER_HEADS):
            st = st_ref[h]
            e_scr[2 * h] = jnp.exp(s_ref[2 * h] - st[1:2]) * st[3:4]
            e_scr[2 * h + 1] = jnp.exp(s_ref[2 * h + 1] - st[2:3])

    def step(at_w, at_r, gw_w, gw_r):
        def stage1(k):
            r0, c0 = (k // (tm // mx)) * mx, (k % (tm // mx)) * mx
            at_w[r0:r0 + mx, c0:c0 + mx] = jnp.dot(u_ref[r0:r0 + mx, :], ht_ref[:, c0:c0 + mx],
                                                   preferred_element_type=F32)

        def stage3(k):
            r0, c0 = (k // (tm // mx)) * mx, (k % (tm // mx)) * mx
            o_ref[r0:r0 + mx, c0:c0 + mx] += jnp.dot(vt_ref[r0:r0 + mx, :], gw_r[:, c0:c0 + mx],
                                                     preferred_element_type=F32)

        live = jnp.where((j >= 1) & (j <= n_tiles), 1.0, 0.0)
        jb = jnp.clip(j - 1, 0, n_tiles - 1)
        for h in range(PEER_HEADS):
            for rr in range(rpt):
                row = h * rpt + rr
                r = jb * rpt + rr
                thr_scr[row:row + 1, :] = st_ref[h, 0:1, :] - s_ref[2 * h, pl.ds(r, 1), :]
                e1_scr[row:row + 1, :] = e_scr[2 * h, pl.ds(r, 1), :] * live

        n1 = (te // mx) * (tm // mx)
        n3 = (d // mx) * (tm // mx)
        kh = nk // 2
        grp = 4
        n2 = (tm // LANES) * 2 * (rpt // grp)
        done1 = done3 = 0
        blk = 0
        for ch in range(tm // LANES):
            cs = slice(ch * LANES, (ch + 1) * LANES)
            for hf in range(2):
                ks = slice(hf * kh, (hf + 1) * kh)
                for g0 in range(0, rpt, grp):
                    ws = [jnp.zeros((kh, LANES), F32) for _ in range(grp)]
                    for h in range(PEER_HEADS):
                        s2c = s_ref[2 * h + 1, ks, cs]
                        e2c = e_scr[2 * h + 1, ks, cs]
                        for q in range(grp):
                            row = h * rpt + g0 + q
                            sel = jnp.where(s2c >= thr_scr[row:row + 1, cs], e2c, 0.0)
                            ws[q] = ws[q] + sel * e1_scr[row:row + 1, cs]
                    for q in range(grp):
                        r0 = (g0 + q) * nk + hf * kh
                        a = at_r[r0:r0 + kh, cs]
                        gw_w[r0:r0 + kh, cs] = (_gelu(a) * ws[q]).astype(BF16)
                    blk += 1
                    while done1 * n2 < blk * n1:
                        stage1(done1)
                        done1 += 1
                    while done3 * n2 < blk * n3:
                        stage3(done3)
                        done3 += 1

    @pl.when(j % 2 == 0)
    def _():
        step(at_a, at_b, gw_a, gw_b)

    @pl.when(j % 2 == 1)
    def _():
        step(at_b, at_a, gw_b, gw_a)


def peer_experts(h2t, u_bf, vt_bf, scores, stats, tm=PEER_TM, te=PEER_TE):
    d, t = h2t.shape
    ne = u_bf.shape[0]
    n_tiles = ne // te
    return pl.pallas_call(
        _peer_kernel,
        grid=(t // tm, n_tiles + 2),
        in_specs=[pl.BlockSpec((d, tm), lambda i, j: (0, i)),
                  pl.BlockSpec((te, d), lambda i, j: (jnp.minimum(j, n_tiles - 1), 0)),
                  pl.BlockSpec((d, te), lambda i, j: (0, jnp.clip(j - 2, 0, n_tiles - 1))),
                  pl.BlockSpec((2 * PEER_HEADS, PEER_KEYS, tm), lambda i, j: (0, 0, i)),
                  pl.BlockSpec((PEER_HEADS, SUBLANES, tm), lambda i, j: (0, 0, i))],
        out_specs=pl.BlockSpec((d, tm), lambda i, j: (0, i)),
        out_shape=jax.ShapeDtypeStruct((d, t), F32),
        scratch_shapes=[pltpu.VMEM((2 * PEER_HEADS, PEER_KEYS, tm), F32),
                        pltpu.VMEM((PEER_HEADS * te // PEER_KEYS, tm), F32),
                        pltpu.VMEM((PEER_HEADS * te // PEER_KEYS, tm), F32),
                        pltpu.VMEM((te, tm), F32), pltpu.VMEM((te, tm), F32),
                        pltpu.VMEM((te, tm), BF16), pltpu.VMEM((te, tm), BF16)],
        compiler_params=_params("parallel", "arbitrary"),
        name="peer_experts",
    )(h2t, u_bf, vt_bf, scores, stats)


def _final_kernel(x1_ref, pt_ref, g2_ref, fg_ref, o_ref):
    x2 = x1_ref[...] + g2_ref[0] * pt_ref[...].T
    ms = jnp.mean(x2 * x2, axis=-1, keepdims=True)
    o_ref[...] = (x2 * lax.rsqrt(ms + EPS)) * fg_ref[...]


def final_norm(x1, peer_t, g2, final_g, l, tm=512):
    t, d = x1.shape
    tpb = l // tm
    return pl.pallas_call(
        _final_kernel,
        grid=(t // tm,),
        in_specs=[pl.BlockSpec((tm, d), lambda i: (i, 0)),
                  pl.BlockSpec((d, tm), lambda i: (0, i)),
                  pl.BlockSpec((1, 1, d), lambda i: (i // tpb, 0, 0)),
                  pl.BlockSpec((1, d), lambda i: (0, 0))],
        out_specs=pl.BlockSpec((tm, d), lambda i: (i, 0)),
        out_shape=jax.ShapeDtypeStruct((t, d), F32),
        compiler_params=_params("parallel"),
        name="final_norm",
    )(x1, peer_t, g2, final_g.reshape(1, d))


def kernel(x, c, ada_w, ada_b, norm_mix_g, norm_ffn_g, w_in, cmp_pos, cmp_w1, cmp_b1, cmp_w2, cmp_b2, rel_bias,
           s5_lam_re, s5_lam_im, s5_log_step, s5_b_re, s5_b_im, s5_c_re, s5_c_im, s5_d, glu_w, glu_b,
           w_up_attn, w_up_ssm, w_out, peer_w_q, peer_sub_keys, peer_u, peer_v, final_g):
    bsz, l, d = x.shape
    assert ada_w.shape[0] == 1, "single-layer block"
    mod = adaln(c, ada_w[0], ada_b[0])
    sh1, sc1, g1, sh2, sc2, g2 = [m[:, None, :] for m in jnp.split(mod, 6, axis=-1)]

    proj = inproj(x, norm_mix_g[0], sc1, sh1, build_wcat(w_in[0]))

    n16 = l // CMP_STRIDE

    def blocks16(cols):
        return cols.reshape(bsz, l, NSA_KV_HEADS, HEAD_DIM).transpose(0, 2, 1, 3).reshape(
            bsz, NSA_KV_HEADS, n16, CMP_STRIDE * HEAD_DIM)

    cmp_kv = compress(blocks16(proj[:, :, COL_CMP:COL_CMP + KV_WIDTH]),
                      blocks16(proj[:, :, COL_CMP + KV_WIDTH:COL_CMP + 2 * KV_WIDTH]),
                      cmp_w1[0], cmp_pos[0], cmp_b1[0], cmp_w2[0], cmp_b2[0])
    diag, cmpb = attention_tables(rel_bias)
    o_attn = attention(proj, cmp_kv, diag, cmpb)

    tables = s5_tables(s5_lam_re[0], s5_lam_im[0], s5_log_step[0], s5_b_re[0], s5_b_im[0],
                       s5_c_re[0], s5_c_im[0])
    y_ssm = s5_glu(proj, tables, s5_d[0], glu_w[0], glu_b[0])

    x1, h2 = merge(x, o_attn, y_ssm, proj, g1, w_up_attn[0], w_up_ssm[0], w_out[0],
                   norm_ffn_g[0], sc2, sh2)

    t = bsz * l
    h2t = h2.reshape(t, d).T
    scores, stats = peer_scores(h2t, peer_w_q[0], peer_sub_keys[0])
    peer_t = peer_experts(h2t, peer_u[0].astype(BF16), peer_v[0].T.astype(BF16), scores, stats)
    out = final_norm(x1.reshape(t, d), peer_t, g2, final_g, l)
    return out.reshape(bsz, l, d)
```

```python
import functools
import math

import numpy as np
import jax
import jax.numpy as jnp
from jax import lax
from jax.experimental import pallas as pl
from jax.experimental.pallas import tpu as pltpu

F32 = jnp.float32
BF16 = jnp.bfloat16
HIGHEST = lax.Precision.HIGHEST

D_MODEL = 2048
NSA_HEADS = 16
NSA_KV_HEADS = 4
NSA_GROUP = NSA_HEADS // NSA_KV_HEADS
HEAD_DIM = 64
NSA_WIDTH = NSA_HEADS * HEAD_DIM
KV_WIDTH = NSA_KV_HEADS * HEAD_DIM
CMP_LEN = 32
CMP_STRIDE = 16
CMP_HIDDEN = 256
SEL_BLOCK = 64
SEL_TOP = 16
WINDOW = 512
S5_WIDTH = 1024
S5_GROUP = 16
S5_GROUPS = S5_WIDTH // S5_GROUP
S5_STATE = 64
S5_FLAT = S5_GROUPS * S5_STATE
REL_BUCKETS = 32
REL_MAX_DIST = 128
PEER_KEYS = 128
PEER_EXPERTS = PEER_KEYS * PEER_KEYS
PEER_HEADS = 8
PEER_TOPK = 16
PEER_QDIM = 256
EPS = 1e-6
NEG_INF = -1e30
FORCE_BONUS = 1e4
LOG2E = math.log2(math.e)
MASK_BIG = 1e30

VMEM_LIMIT_BYTES = 56 * 1024 * 1024
LANES = 128
SUBLANES = 8

PROJ_WIDTH = 8192
COL_Q = 0
COL_U = 1024
COL_MERGE = 2048
COL_SEL = 6144
COL_WIN = 6656
COL_CMP = 7168
COL_GATES = 7680

ATT_TILE = 128
BIAS_SPAN = 256
ATT_SUPER = 512
ATT_ORIGIN = 2 * ATT_SUPER - ATT_TILE
ATT_TABLE_ROWS = ATT_ORIGIN + WINDOW + ATT_TILE


def _params(*sem):
    return pltpu.CompilerParams(dimension_semantics=sem, vmem_limit_bytes=VMEM_LIMIT_BYTES)


def _gelu(x):
    return 0.5 * x * (1.0 + jnp.tanh(math.sqrt(2.0 / math.pi) * (x + 0.044715 * (x * x * x))))


def _sigmoid(x):
    return 1.0 / (1.0 + jnp.exp(-x))


def _adaln_kernel(c_ref, w_ref, b_ref, o_ref):
    c = c_ref[...]
    a = c * _sigmoid(c)
    o_ref[...] = jnp.dot(a, w_ref[...], precision=HIGHEST, preferred_element_type=F32) + b_ref[...]


def adaln(c, w, b):
    bsz, d = c.shape
    n = w.shape[1]
    tn = 1024
    cp = jnp.zeros((SUBLANES, d), F32).at[:bsz].set(c)
    out = pl.pallas_call(
        _adaln_kernel,
        grid=(n // tn,),
        in_specs=[pl.BlockSpec((SUBLANES, d), lambda j: (0, 0)),
                  pl.BlockSpec((d, tn), lambda j: (0, j)),
                  pl.BlockSpec((1, tn), lambda j: (0, j))],
        out_specs=pl.BlockSpec((SUBLANES, tn), lambda j: (0, j)),
        out_shape=jax.ShapeDtypeStruct((SUBLANES, n), F32),
        compiler_params=_params("parallel"),
        name="adaln",
    )(cp, w, b.reshape(1, n))
    return out[:bsz]


def _modulated_norm(x, g, sc, sh):
    ms = jnp.mean(x * x, axis=-1, keepdims=True)
    return (x * lax.rsqrt(ms + EPS)) * g * (1.0 + sc) + sh


def _inproj_kernel(x_ref, g_ref, sc_ref, sh_ref, w_ref, o_ref, h_scr):
    @pl.when(pl.program_id(2) == 0)
    def _():
        h_scr[...] = _modulated_norm(x_ref[0], g_ref[...], sc_ref[0], sh_ref[0]).astype(BF16)

    o_ref[0] = jnp.dot(h_scr[...], w_ref[...], preferred_element_type=F32).astype(BF16)


def inproj(x, g, sc, sh, wcat, tm=1024, tn=1024):
    bsz, l, d = x.shape
    n = wcat.shape[1]
    tm = min(tm, l)
    return pl.pallas_call(
        _inproj_kernel,
        grid=(bsz, l // tm, n // tn),
        in_specs=[pl.BlockSpec((1, tm, d), lambda b, i, j: (b, i, 0)),
                  pl.BlockSpec((1, d), lambda b, i, j: (0, 0)),
                  pl.BlockSpec((1, 1, d), lambda b, i, j: (b, 0, 0)),
                  pl.BlockSpec((1, 1, d), lambda b, i, j: (b, 0, 0)),
                  pl.BlockSpec((d, tn), lambda b, i, j: (0, j))],
        out_specs=pl.BlockSpec((1, tm, tn), lambda b, i, j: (b, i, j)),
        out_shape=jax.ShapeDtypeStruct((bsz, l, n), BF16),
        scratch_shapes=[pltpu.VMEM((tm, d), BF16)],
        compiler_params=_params("parallel", "parallel", "arbitrary"),
        name="inproj",
    )(x, g.reshape(1, d), sc, sh, wcat)


def build_wcat(w_in):
    d = w_in.shape[0]
    pts = np.cumsum([NSA_WIDTH] + [KV_WIDTH] * 6 + [3 * NSA_HEADS, S5_WIDTH])
    q, kc, vc, ks, vs, kw, vw, gates, u, merge = jnp.split(w_in, pts, axis=1)

    def pair(k, v):
        return jnp.stack([k.reshape(d, NSA_KV_HEADS, HEAD_DIM), v.reshape(d, NSA_KV_HEADS, HEAD_DIM)],
                         axis=2).reshape(d, 2 * KV_WIDTH)

    gates_p = jnp.pad(gates, ((0, 0), (0, LANES - 3 * NSA_HEADS)))
    pad = jnp.zeros((d, PROJ_WIDTH - COL_GATES - LANES), w_in.dtype)
    wcat = jnp.concatenate([q * (HEAD_DIM ** -0.5 * LOG2E), u, merge, pair(ks, vs), pair(kw, vw),
                            kc, vc, gates_p, pad], axis=1)
    return wcat.astype(BF16)


def _compress_kernel(gk_ref, gv_ref, w1_ref, pos_ref, b1_ref, w2_ref, b2_ref, o_ref):
    half = CMP_STRIDE * HEAD_DIM
    nrow = gk_ref.shape[2]
    out = b2_ref[...]
    for j, g_ref in enumerate((gk_ref, gv_ref)):
        g = g_ref[0, 0]
        w1 = w1_ref[j]
        top = jnp.dot(g, w1[:half].astype(BF16), preferred_element_type=F32)
        bot = jnp.dot(g, w1[half:].astype(BF16), preferred_element_type=F32)
        cst = jnp.dot(pos_ref[j], w1, precision=HIGHEST, preferred_element_type=F32)[0:1] + b1_ref[j]
        hid = top + pltpu.roll(bot, nrow - 1, axis=0) + cst
        out = out + jnp.dot(_gelu(hid).astype(BF16), w2_ref[j].astype(BF16), preferred_element_type=F32)
    o_ref[0, 0] = out.astype(BF16)


def compress(gk, gv, w1, pos, b1, w2, b2):
    bsz, hkv, n16, wd = gk.shape
    posf = jnp.zeros((2, SUBLANES, CMP_LEN * HEAD_DIM), F32).at[:, 0].set(pos.reshape(2, -1))
    w2p = jnp.zeros((2, CMP_HIDDEN, 2 * HEAD_DIM), F32)
    w2p = w2p.at[0, :, :HEAD_DIM].set(w2[0]).at[1, :, HEAD_DIM:].set(w2[1])
    b2p = jnp.concatenate([b2[0], b2[1]]).reshape(1, 2 * HEAD_DIM)
    gspec = pl.BlockSpec((1, 1, n16, wd), lambda b, h: (b, h, 0, 0))
    full = lambda a: pl.BlockSpec(a.shape, lambda b, h: (0,) * a.ndim)
    b1r = b1.reshape(2, 1, CMP_HIDDEN)
    return pl.pallas_call(
        _compress_kernel,
        grid=(bsz, hkv),
        in_specs=[gspec, gspec, full(w1), full(posf), full(b1r), full(w2p), full(b2p)],
        out_specs=pl.BlockSpec((1, 1, n16, 2 * HEAD_DIM), lambda b, h: (b, h, 0, 0)),
        out_shape=jax.ShapeDtypeStruct((bsz, hkv, n16, 2 * HEAD_DIM), BF16),
        compiler_params=_params("parallel", "parallel"),
        name="compress",
    )(gk, gv, w1, posf, b1r, w2p, b2p)


def _t5_bucket_table(n):
    dist = np.arange(n)
    max_exact = REL_BUCKETS // 2
    ratio = np.log(np.maximum(dist, 1).astype(np.float32) / np.float32(max_exact)) / np.float32(
        math.log(REL_MAX_DIST / max_exact))
    large = np.minimum(max_exact + (ratio * np.float32(REL_BUCKETS - max_exact)).astype(np.int32), REL_BUCKETS - 1)
    return np.where(dist < max_exact, dist, large)


def _cmp_to_sel_t(n_cmp, n_sel):
    cs = np.arange(n_cmp)[:, None] * CMP_STRIDE
    ss = np.arange(n_sel)[None, :] * SEL_BLOCK
    ov = np.clip(np.minimum(cs + CMP_LEN, ss + SEL_BLOCK) - np.maximum(cs, ss), 0, None)
    return np.ascontiguousarray((ov / CMP_LEN).T.astype(np.float32))


def attention_tables(rel_bias):
    tq = ATT_TILE
    bucket = _t5_bucket_table(BIAS_SPAN)
    assert np.all(bucket[113:] == REL_BUCKETS - 1)
    bv = (rel_bias[bucket] - rel_bias[REL_BUCKETS - 1][None, :]).T * LOG2E
    heads = bv.shape[0]
    band0 = ATT_ORIGIN - BIAS_SPAN
    band = BIAS_SPAN + tq
    period = band + tq
    vext = jnp.concatenate([jnp.zeros((heads, period - BIAS_SPAN), F32), bv.astype(F32)], axis=1)
    toep = jnp.tile(vext, (1, band))[:, :band * (period - 1)].reshape(heads, band, period - 1)[:, :, :tq]
    tab = jnp.pad(toep, ((0, 0), (band0, ATT_TABLE_ROWS - band0 - band), (0, 0)))
    tab = tab.reshape(NSA_KV_HEADS, NSA_GROUP, ATT_TABLE_ROWS, tq).transpose(0, 2, 1, 3)
    tab = tab.reshape(NSA_KV_HEADS, ATT_TABLE_ROWS, NSA_GROUP * tq)
    lo = 16 * CMP_STRIDE - (8 * CMP_STRIDE - CMP_LEN + 1)
    fpad = jnp.pad(bv.astype(F32), ((0, 0), (lo, tq)))
    cmpb = jnp.stack([fpad[:, lo + (8 * CMP_STRIDE - CMP_LEN + 1) - CMP_STRIDE * r:][:, :tq] for r in range(16)],
                     axis=1)
    cmpb = cmpb.reshape(NSA_KV_HEADS, NSA_GROUP, 16, tq).transpose(0, 2, 1, 3).reshape(NSA_KV_HEADS, 16, NSA_GROUP * tq)
    return tab, cmpb


def _importance_taps(n_cmp, n_sel):
    m = _cmp_to_sel_t(n_cmp, n_sel)
    ratio = SEL_BLOCK // CMP_STRIDE
    cols = np.nonzero(m[1])[0]
    taps = [(int(c - ratio), float(m[1, c])) for c in cols]
    for j in range(n_sel):
        want = np.zeros(n_cmp, np.float32)
        for k, w in taps:
            if 0 <= ratio * j + k < n_cmp:
                want[ratio * j + k] = w
        assert np.array_equal(want, m[j])
    return ratio, taps


def _attn_kernel(q_ref, sel_ref, selt_ref, win_ref, wint_ref, cmp_ref, cmpt_ref, gate_ref, tab_ref, cmpb_ref,
                 ind_ref, kmask_ref, o_ref, m_scr, acc_scr, sel_scr, s_scr, p_scr, *, taps, ratio):
    tq = ATT_TILE
    sk = ATT_SUPER
    cols = NSA_GROUP * tq
    kvh = pl.program_id(1)
    qt = pl.program_id(2)
    n_cmp = cmp_ref.shape[2]
    n_sel = sel_scr.shape[0]
    per = sk // tq
    pad = SUBLANES

    qT = q_ref[0]
    zpad = jnp.zeros((HEAD_DIM, tq), BF16)
    qpad = jnp.concatenate(
        [jnp.concatenate([qT[g * HEAD_DIM:(g + 1) * HEAD_DIM], zpad], axis=0) for g in range(NSA_GROUP)],
        axis=1)
    col_i = lax.broadcasted_iota(jnp.int32, (1, cols), 1) & (tq - 1)
    t_col = qt * tq + col_i

    ckv = cmp_ref[0, 0]
    s_scr[0:pad, :] = jnp.zeros((pad, cols), F32)
    s_scr[pad + n_cmp:, :] = jnp.zeros((pad, cols), F32)
    s_scr[pad:pad + n_cmp, :] = jnp.dot(ckv, qpad, preferred_element_type=F32)
    slot0 = pl.multiple_of(qt * (tq // CMP_STRIDE), SUBLANES)
    s_scr[pl.ds(slot0, 16), :] += cmpb_ref[0]
    s = s_scr[pad:pad + n_cmp, :]
    n_idx = lax.broadcasted_iota(jnp.int32, (n_cmp, cols), 0)
    valid = (n_idx * CMP_STRIDE + (CMP_LEN - 1)) <= t_col
    s = jnp.where(valid, s, NEG_INF)
    mx = jnp.max(s, axis=0, keepdims=True)
    p = jnp.where(valid, jnp.exp2(s - mx), 0.0)
    den = jnp.sum(p, axis=0, keepdims=True)
    p = p * jnp.where(den > 0.0, 1.0 / den, 0.0)
    o_c = jnp.dot(cmpt_ref[0, 0], p.astype(BF16), preferred_element_type=F32)

    p_scr[0:pad, :] = jnp.zeros((pad, tq), F32)
    p_scr[pad:, :] = p[:, 0:tq] + p[:, tq:2 * tq] + p[:, 2 * tq:3 * tq] + p[:, 3 * tq:4 * tq]
    imp = jnp.zeros((n_sel, tq), F32)
    for k, w in taps:
        imp = imp + w * p_scr[pl.ds(pad + k, n_sel, stride=ratio), :]
    blk = lax.broadcasted_iota(jnp.int32, (n_sel, tq), 0)
    qi = lax.broadcasted_iota(jnp.int32, (n_sel, tq), 1)
    cur = jnp.right_shift(qt * tq + qi, 6)

    @pl.when((qt + 1) * tq <= SEL_TOP * SEL_BLOCK)
    def _():
        sel_scr[...] = jnp.where(blk <= cur, 1.0, 0.0)

    @pl.when((qt + 1) * tq > SEL_TOP * SEL_BLOCK)
    def _():
        forced = (blk == 0) | (blk == cur) | (blk == cur - 1)
        val = jnp.where(blk <= cur, imp + FORCE_BONUS * forced.astype(F32), NEG_INF)
        rank = jnp.zeros((n_sel, tq), F32)
        for k in range(n_sel):
            rk = val[k:k + 1, :]
            beats = (rk > val) | ((rk == val) & (blk > k))
            rank = rank + beats.astype(F32)
        sel_scr[...] = jnp.where(rank < float(SEL_TOP), 1.0, 0.0)

    m_scr[...] = jnp.full(m_scr.shape, NEG_INF, F32)
    acc_scr[...] = jnp.zeros(acc_scr.shape, F32)
    st_last = qt // per
    dq = qt - st_last * per
    blocks = sk // SEL_BLOCK
    q_top = qpad[0:HEAD_DIM]
    q_fill = jnp.zeros((HEAD_DIM - 2 * SUBLANES, cols), BF16)

    def sel_tile(st, table_row, causal_shift):
        kv = sel_ref[0, pl.ds(pl.multiple_of(st * sk, sk), sk), :]
        kvm = kv * kmask_ref[...] + ind_ref[...]
        off = (sel_scr[pl.ds(pl.multiple_of(st * blocks, blocks), blocks), :] - 1.0) * MASK_BIG
        off = jnp.concatenate([off, jnp.zeros_like(off)], axis=0)
        off = jnp.concatenate([off] * NSA_GROUP, axis=1).astype(BF16)
        sc = jnp.dot(kvm, jnp.concatenate([q_top, off, q_fill], axis=0), preferred_element_type=F32)
        if table_row is not None:
            sc = sc + tab_ref[0, pl.ds(pl.multiple_of(table_row, tq), sk), :]
        if causal_shift is not None:
            rel = col_i - lax.broadcasted_iota(jnp.int32, (sk, cols), 0)
            sc = jnp.where(rel + causal_shift >= 0, sc, NEG_INF)
        m_old = m_scr[...]
        m_new = jnp.maximum(m_old, jnp.max(sc, axis=0, keepdims=True))
        alpha = jnp.exp2(m_old - m_new)
        pe = jnp.exp2(sc - m_new).astype(BF16)
        acc_scr[...] = alpha * acc_scr[...] + jnp.dot(selt_ref[0, 0, st], pe, preferred_element_type=F32)
        m_scr[...] = m_new

    def sel_body(st, carry):
        sel_tile(st, None, None)
        return carry

    lax.fori_loop(0, jnp.maximum(st_last - 1, 0), sel_body, 0)

    @pl.when(st_last >= 1)
    def _():
        sel_tile(st_last - 1, ATT_ORIGIN - sk - dq * tq, None)

    sel_tile(st_last, ATT_ORIGIN - dq * tq, dq * tq)
    o_s = acc_scr[HEAD_DIM:, :] * (1.0 / acc_scr[0:1, :])

    wk = WINDOW + tq
    n_w = wk // tq
    k0 = jnp.maximum(qt - WINDOW // tq, 0)
    dw = (qt - k0) * tq
    kv = win_ref[0, pl.ds(pl.multiple_of(k0 * tq, tq), wk), :]
    sc = jnp.dot(kv, qpad, preferred_element_type=F32)
    sc = sc + tab_ref[0, pl.ds(pl.multiple_of(ATT_ORIGIN - dw, tq), wk), :]
    dist = col_i - lax.broadcasted_iota(jnp.int32, (wk, cols), 0) + dw
    sc = jnp.where((dist >= 0) & (dist < WINDOW), sc, NEG_INF)
    mw = jnp.max(sc, axis=0, keepdims=True)
    pw = jnp.exp2(sc - mw).astype(BF16)
    ow = jnp.dot(wint_ref[0, 0, k0], pw[0:tq], preferred_element_type=F32)
    for m in range(1, n_w):
        ow = ow + jnp.dot(wint_ref[0, 0, k0 + m], pw[m * tq:(m + 1) * tq], preferred_element_type=F32)
    o_w = ow[HEAD_DIM:, :] * (1.0 / ow[0:1, :])

    branches = (o_c[HEAD_DIM:, :], o_s, o_w)
    outs = []
    for g in range(NSA_GROUP):
        acc = jnp.zeros((HEAD_DIM, tq), F32)
        for br in range(3):
            grow = gate_ref[0, pl.ds((kvh * NSA_GROUP + g) * 3 + br, 1), :]
            acc = acc + _sigmoid(grow) * branches[br][:, g * tq:(g + 1) * tq]
        outs.append(acc)
    for pr in range(NSA_GROUP // 2):
        pair = jnp.concatenate([outs[2 * pr], outs[2 * pr + 1]], axis=0)
        o_ref[0, :, pr * LANES:(pr + 1) * LANES] = pair.T.astype(BF16)


def attention(proj, cmp_kv, table, cmpb_t):
    bsz, l, _ = proj.shape
    tq = ATT_TILE
    sk = ATT_SUPER
    assert l % sk == 0 and l >= WINDOW + tq
    n_kt = l // tq
    n_st = l // sk
    n_cmp = cmp_kv.shape[2]
    n_sel = l // SEL_BLOCK
    ratio, taps = _importance_taps(n_cmp, n_sel)
    gw = NSA_GROUP * HEAD_DIM
    kvw = 2 * KV_WIDTH
    q_t = proj[:, :, COL_Q:COL_Q + NSA_WIDTH].transpose(0, 2, 1)

    def tiles_t(c0, nt):
        t = proj[:, :, c0:c0 + kvw].reshape(bsz, nt, l // nt, NSA_KV_HEADS, LANES).transpose(0, 3, 1, 4, 2)
        first = lax.broadcasted_iota(jnp.int32, t.shape, 3) == 0
        return jnp.where(first, jnp.ones((), BF16), t)

    sel_t = tiles_t(COL_SEL, n_st)
    win_t = tiles_t(COL_WIN, n_kt)
    cmp_t = cmp_kv.transpose(0, 1, 3, 2)
    gates_t = proj[:, :, COL_GATES:COL_GATES + LANES].astype(F32).transpose(0, 2, 1)
    key = np.arange(sk)[:, None]
    lane = np.arange(LANES)[None, :]
    ind = jnp.asarray((lane == HEAD_DIM + key // SEL_BLOCK).astype(np.float32), BF16)
    kmask = jnp.asarray(np.broadcast_to(lane < HEAD_DIM, (sk, LANES)).astype(np.float32), BF16)
    const = lambda a: pl.BlockSpec(a.shape, lambda b, h, i: (0,) * a.ndim)
    return pl.pallas_call(
        functools.partial(_attn_kernel, taps=taps, ratio=ratio),
        grid=(bsz, NSA_KV_HEADS, n_kt),
        in_specs=[pl.BlockSpec((1, gw, tq), lambda b, h, i: (b, h, i)),
                  pl.BlockSpec((1, l, LANES), lambda b, h, i: (b, 0, COL_SEL // LANES + h)),
                  pl.BlockSpec((1, 1, n_st, LANES, sk), lambda b, h, i: (b, h, 0, 0, 0)),
                  pl.BlockSpec((1, l, LANES), lambda b, h, i: (b, 0, COL_WIN // LANES + h)),
                  pl.BlockSpec((1, 1, n_kt, LANES, tq), lambda b, h, i: (b, h, 0, 0, 0)),
                  pl.BlockSpec((1, 1, n_cmp, LANES), lambda b, h, i: (b, h, 0, 0)),
                  pl.BlockSpec((1, 1, LANES, n_cmp), lambda b, h, i: (b, h, 0, 0)),
                  pl.BlockSpec((1, LANES, tq), lambda b, h, i: (b, 0, i)),
                  pl.BlockSpec((1, ATT_TABLE_ROWS, NSA_GROUP * tq), lambda b, h, i: (h, 0, 0)),
                  pl.BlockSpec((1, 16, NSA_GROUP * tq), lambda b, h, i: (h, 0, 0)),
                  const(ind), const(kmask)],
        out_specs=pl.BlockSpec((1, tq, gw), lambda b, h, i: (b, i, h)),
        out_shape=jax.ShapeDtypeStruct((bsz, l, NSA_WIDTH), BF16),
        scratch_shapes=[pltpu.VMEM((1, NSA_GROUP * tq), F32),
                        pltpu.VMEM((LANES, NSA_GROUP * tq), F32),
                        pltpu.VMEM((n_sel, tq), F32),
                        pltpu.VMEM((n_cmp + 2 * SUBLANES, NSA_GROUP * tq), F32),
                        pltpu.VMEM((n_cmp + SUBLANES, tq), F32)],
        compiler_params=_params("parallel", "parallel", "arbitrary"),
        name="nsa_attention",
    )(q_t, proj, sel_t, proj, win_t, cmp_kv, cmp_t, gates_t, table, cmpb_t, ind, kmask)


S5_BLOCK = 256
S5_CHUNK = S5_BLOCK // SUBLANES
S5_SLAB = 1024


def _s5_kernel(u_ref, perm_ref, permt_ref, bre_ref, bim_ref, are_ref, aim_ref, ace_ref, aci_ref, cre_ref, cim_ref,
               d_ref, gw_ref, gb_ref, o_ref, bu_re, bu_im, xs_re, xs_im, carry_re, carry_im, init_re, init_im):
    nr = S5_GROUPS // 16
    rw = 16 * S5_GROUP
    sw = 16 * S5_STATE

    @pl.when(pl.program_id(1) == 0)
    def _():
        carry_re[...] = jnp.zeros(carry_re.shape, F32)
        carry_im[...] = jnp.zeros(carry_im.shape, F32)

    perm = perm_ref[...]
    up = jnp.dot(perm, u_ref[0], preferred_element_type=F32).astype(BF16)
    for r in range(nr):
        ur = up[:, r * rw:(r + 1) * rw]
        bu_re[:, r * sw:(r + 1) * sw] = jnp.dot(ur, bre_ref[r], preferred_element_type=F32)
        bu_im[:, r * sw:(r + 1) * sw] = jnp.dot(ur, bim_ref[r], preferred_element_type=F32)

    nslab = S5_FLAT // S5_SLAB
    row8 = lax.broadcasted_iota(jnp.int32, (SUBLANES, S5_SLAB), 0)
    for sl in range(nslab):
        cs = slice(sl * S5_SLAB, (sl + 1) * S5_SLAB)
        a_re = jnp.broadcast_to(are_ref[:, cs], (SUBLANES, S5_SLAB))
        a_im = jnp.broadcast_to(aim_ref[:, cs], (SUBLANES, S5_SLAB))

        def step1(s, st):
            s_re, s_im = st
            off = pl.multiple_of(s * SUBLANES, SUBLANES)
            n_re = a_re * s_re - a_im * s_im + bu_re[pl.ds(off, SUBLANES), cs]
            n_im = a_re * s_im + a_im * s_re + bu_im[pl.ds(off, SUBLANES), cs]
            return n_re, n_im

        zero = jnp.zeros((SUBLANES, S5_SLAB), F32)
        e_re, e_im = lax.fori_loop(0, S5_CHUNK, step1, (zero, zero))

        ac_re = ace_ref[:, cs]
        ac_im = aci_ref[:, cs]
        x_re = carry_re[:, cs]
        x_im = carry_im[:, cs]
        st_re = jnp.zeros((SUBLANES, S5_SLAB), F32)
        st_im = jnp.zeros((SUBLANES, S5_SLAB), F32)
        for c in range(SUBLANES):
            st_re = jnp.where(row8 == c, x_re, st_re)
            st_im = jnp.where(row8 == c, x_im, st_im)
            n_re = ac_re * x_re - ac_im * x_im + e_re[c:c + 1]
            n_im = ac_re * x_im + ac_im * x_re + e_im[c:c + 1]
            x_re, x_im = n_re, n_im
        carry_re[:, cs] = x_re
        carry_im[:, cs] = x_im
        init_re[:, cs] = st_re
        init_im[:, cs] = st_im

        def step2(s, st):
            s_re, s_im = st
            off = pl.multiple_of(s * SUBLANES, SUBLANES)
            n_re = a_re * s_re - a_im * s_im + bu_re[pl.ds(off, SUBLANES), cs]
            n_im = a_re * s_im + a_im * s_re + bu_im[pl.ds(off, SUBLANES), cs]
            xs_re[pl.ds(off, SUBLANES), cs] = n_re
            xs_im[pl.ds(off, SUBLANES), cs] = n_im
            return n_re, n_im

        lax.fori_loop(0, S5_CHUNK, step2, (st_re, st_im))

    ys = []
    for r in range(nr):
        xr = xs_re[:, r * sw:(r + 1) * sw].astype(BF16)
        xi = xs_im[:, r * sw:(r + 1) * sw].astype(BF16)
        ys.append(jnp.dot(xr, cre_ref[r], preferred_element_type=F32)
                  - jnp.dot(xi, cim_ref[r], preferred_element_type=F32))
    y = jnp.concatenate(ys, axis=1) + d_ref[...] * up.astype(F32)
    y = _gelu(y)
    z = jnp.dot(y.astype(BF16), gw_ref[...], preferred_element_type=F32) + gb_ref[...]
    y = (y * _sigmoid(z)).astype(BF16)
    o_ref[0] = jnp.dot(permt_ref[...], y, preferred_element_type=F32).astype(BF16)


def s5_tables(lam_re, lam_im, log_step, b_re, b_im, c_re, c_im):
    f32 = F32
    lam_re, lam_im = lam_re.astype(f32), lam_im.astype(f32)
    step = jnp.exp(log_step.astype(f32))[:, None]
    mag = jnp.exp(lam_re * step)
    lb_re, lb_im = mag * jnp.cos(lam_im * step), mag * jnp.sin(lam_im * step)
    den = lam_re * lam_re + lam_im * lam_im
    n_re = lb_re - 1.0
    f_re = (n_re * lam_re + lb_im * lam_im) / den
    f_im = (lb_im * lam_re - n_re * lam_im) / den
    b_re, b_im = b_re.astype(f32), b_im.astype(f32)
    bb_re = f_re[..., None] * b_re - f_im[..., None] * b_im
    bb_im = f_re[..., None] * b_im + f_im[..., None] * b_re
    magc = jnp.exp(lam_re * step * S5_CHUNK)
    ac_re, ac_im = magc * jnp.cos(lam_im * step * S5_CHUNK), magc * jnp.sin(lam_im * step * S5_CHUNK)

    nr = S5_GROUPS // 16
    eye = jnp.eye(16, dtype=f32)

    def b_blocks(bb):
        bb = bb.reshape(nr, 16, S5_STATE, S5_GROUP)
        return jnp.einsum('rgph,gk->rghkp', bb, eye).reshape(nr, 16 * S5_GROUP, 16 * S5_STATE).astype(BF16)

    def c_blocks(cc):
        cc = cc.astype(f32).reshape(nr, 16, S5_GROUP, S5_STATE)
        return jnp.einsum('rghp,gk->rgpkh', cc, eye).reshape(nr, 16 * S5_STATE, 16 * S5_GROUP).astype(BF16)

    flat = lambda a: a.reshape(1, S5_FLAT)
    return (b_blocks(bb_re), b_blocks(bb_im), flat(lb_re), flat(lb_im), flat(ac_re), flat(ac_im),
            c_blocks(c_re), c_blocks(c_im))


def _s5_perm():
    p = np.zeros((S5_BLOCK, S5_BLOCK), np.float32)
    for c in range(SUBLANES):
        for s in range(S5_CHUNK):
            p[s * SUBLANES + c, c * S5_CHUNK + s] = 1.0
    return p


def s5_glu(proj, tables, d_skip, glu_w, glu_b):
    bsz, l, _ = proj.shape
    tb = S5_BLOCK
    bre, bim, are, aim, ace, aci, cre, cim = tables
    perm = jnp.asarray(_s5_perm(), BF16)
    permt = jnp.asarray(_s5_perm().T, BF16)
    full = lambda a: pl.BlockSpec(a.shape, lambda b, i: (0,) * a.ndim)
    d2 = d_skip.reshape(1, S5_WIDTH).astype(F32)
    gb2 = glu_b.reshape(1, S5_WIDTH).astype(F32)
    gw = glu_w.astype(BF16)
    return pl.pallas_call(
        _s5_kernel,
        grid=(bsz, l // tb),
        in_specs=[pl.BlockSpec((1, tb, S5_WIDTH), lambda b, i: (b, i, COL_U // S5_WIDTH)),
                  full(perm), full(permt), full(bre), full(bim), full(are), full(aim), full(ace), full(aci),
                  full(cre), full(cim), full(d2), full(gw), full(gb2)],
        out_specs=pl.BlockSpec((1, tb, S5_WIDTH), lambda b, i: (b, i, 0)),
        out_shape=jax.ShapeDtypeStruct((bsz, l, S5_WIDTH), BF16),
        scratch_shapes=[pltpu.VMEM((tb, S5_FLAT), F32), pltpu.VMEM((tb, S5_FLAT), F32),
                        pltpu.VMEM((tb, S5_FLAT), F32), pltpu.VMEM((tb, S5_FLAT), F32),
                        pltpu.VMEM((1, S5_FLAT), F32), pltpu.VMEM((1, S5_FLAT), F32),
                        pltpu.VMEM((SUBLANES, S5_FLAT), F32), pltpu.VMEM((SUBLANES, S5_FLAT), F32)],
        compiler_params=_params("parallel", "arbitrary"),
        name="s5_glu",
    )(proj, perm, permt, bre, bim, are, aim, ace, aci, cre, cim, d2, gw, gb2)


def _merge_kernel(x_ref, oa_ref, ys_ref, ga_ref, gb_ref, g1_ref, wa_ref, ws_ref, wo_ref,
                  ng_ref, sc_ref, sh_ref, x1_ref, h2_ref):
    a = jnp.dot(oa_ref[0], wa_ref[...], preferred_element_type=F32)
    s = jnp.dot(ys_ref[0], ws_ref[...], preferred_element_type=F32)
    mixed = _sigmoid(ga_ref[0].astype(F32)) * a + _sigmoid(gb_ref[0].astype(F32)) * s
    out = jnp.dot(mixed.astype(BF16), wo_ref[...], preferred_element_type=F32)
    x1 = x_ref[0] + g1_ref[0] * out
    x1_ref[0] = x1
    h2_ref[...] = _modulated_norm(x1, ng_ref[...], sc_ref[0], sh_ref[0]).T.astype(BF16)


def merge(x, o_attn, y_ssm, proj, g1, w_up_attn, w_up_ssm, w_out, norm_g, sc2, sh2, tm=256):
    bsz, l, d = x.shape
    tok = lambda w, cb=0: pl.BlockSpec((1, tm, w), lambda b, i: (b, i, cb))
    vec = pl.BlockSpec((1, 1, d), lambda b, i: (b, 0, 0))
    full = lambda a: pl.BlockSpec(a.shape, lambda b, i: (0,) * a.ndim)
    wa, ws, wo = w_up_attn.astype(BF16), w_up_ssm.astype(BF16), w_out.astype(BF16)
    ng = norm_g.reshape(1, d)
    return pl.pallas_call(
        _merge_kernel,
        grid=(bsz, l // tm),
        in_specs=[tok(d), tok(NSA_WIDTH), tok(S5_WIDTH), tok(d, COL_MERGE // d), tok(d, COL_MERGE // d + 1),
                  vec, full(wa), full(ws), full(wo), full(ng), vec, vec],
        out_specs=[tok(d), pl.BlockSpec((d, tm), lambda b, i: (0, b * (l // tm) + i))],
        out_shape=[jax.ShapeDtypeStruct((bsz, l, d), F32), jax.ShapeDtypeStruct((d, bsz * l), BF16)],
        compiler_params=_params("parallel", "parallel"),
        name="merge",
    )(x, o_attn, y_ssm, proj, proj, g1, wa, ws, wo, ng, sc2, sh2)


def _extract_topk(vals, k):
    outs = [[] for _ in vals]
    rems = list(vals)
    for _ in range(k):
        ms = [jnp.max(r, axis=0, keepdims=True) for r in rems]
        for o, m in zip(outs, ms):
            o.append(m)
        rems = [jnp.where(r == m, -jnp.inf, r) for r, m in zip(rems, ms)]
    return outs


PEER_RANKS = PEER_TOPK + 1
_PAIR_COUNTS = [PEER_RANKS // (a + 1) for a in range(PEER_RANKS)]


def _pad_rows(x, rows):
    return jnp.concatenate([x, jnp.full((rows - x.shape[0], x.shape[1]), -jnp.inf, F32)], axis=0)


def _candidate_sums(v1, v2, width):
    v1m = jnp.concatenate(v1, axis=0)
    v2m = jnp.concatenate(v2, axis=0)
    row = lax.broadcasted_iota(jnp.int32, (SUBLANES, width), 0)
    pieces = [_pad_rows(v1[0] + v2m, 3 * SUBLANES)]
    for a in range(1, SUBLANES):
        pieces.append(jnp.where(row < _PAIR_COUNTS[a], v1[a] + v2m[:SUBLANES], -jnp.inf))
    pieces.append(_pad_rows(v1m[SUBLANES:] + v2[0], 2 * SUBLANES))
    return jnp.concatenate(pieces, axis=0)


def _peer_score_kernel(ht_ref, wqt_ref, keys_ref, s_ref, st_ref):
    tm = ht_ref.shape[1]
    half = PEER_QDIM // 2
    for h in range(PEER_HEADS):
        scs = []
        for c in range(2):
            hc = 2 * h + c
            qt = jnp.dot(wqt_ref[hc * half:(hc + 1) * half, :], ht_ref[...], preferred_element_type=F32)
            sc = jnp.dot(keys_ref[h, c], qt, precision=HIGHEST, preferred_element_type=F32)
            s_ref[hc] = sc
            scs.append(sc)
        chunks = [slice(ch * LANES, (ch + 1) * LANES) for ch in range(tm // LANES)]
        sums, firsts = [], []
        for cs in chunks:
            v1, v2 = _extract_topk([scs[0][:, cs], scs[1][:, cs]], PEER_RANKS)
            sums.append(_candidate_sums(v1, v2, LANES))
            firsts.append((v1[0], v2[0]))
        bests = _extract_topk(sums, PEER_RANKS)
        for cs, sm, best, (m1, m2) in zip(chunks, sums, bests, firsts):
            theta = 0.5 * (best[PEER_TOPK - 1] + best[PEER_TOPK])
            z = jnp.sum(jnp.where(sm >= theta, jnp.exp(sm - best[0]), 0.0), axis=0, keepdims=True)
            zero = jnp.zeros((SUBLANES - 4, LANES), F32)
            st_ref[h, :, cs] = jnp.concatenate([theta, m1, m2, 1.0 / z, zero], axis=0)


def peer_scores(h2t, w_q, sub_keys, tm=256):
    d, t = h2t.shape
    wqt = w_q.T.astype(BF16)
    nk = PEER_KEYS
    return pl.pallas_call(
        _peer_score_kernel,
        grid=(t // tm,),
        in_specs=[pl.BlockSpec((d, tm), lambda i: (0, i)),
                  pl.BlockSpec(wqt.shape, lambda i: (0, 0)),
                  pl.BlockSpec(sub_keys.shape, lambda i: (0, 0, 0, 0))],
        out_specs=[pl.BlockSpec((2 * PEER_HEADS, nk, tm), lambda i: (0, 0, i)),
                   pl.BlockSpec((PEER_HEADS, SUBLANES, tm), lambda i: (0, 0, i))],
        out_shape=[jax.ShapeDtypeStruct((2 * PEER_HEADS, nk, t), F32),
                   jax.ShapeDtypeStruct((PEER_HEADS, SUBLANES, t), F32)],
        compiler_params=_params("parallel"),
        name="peer_scores",
    )(h2t, wqt, sub_keys)


PEER_TM = 512
PEER_TE = 1024
PEER_MXU_TILE = 256


def _peer_kernel(ht_ref, u_ref, vt_ref, s_ref, st_ref, o_ref, e_scr, thr_scr, e1_scr, at_a, at_b, gw_a, gw_b):
    j = pl.program_id(1)
    n_tiles = pl.num_programs(1) - 2
    nk = PEER_KEYS
    d, tm = ht_ref.shape
    te = u_ref.shape[0]
    rpt = te // nk
    mx = PEER_MXU_TILE

    @pl.when(j == 0)
    def _():
        o_ref[...] = jnp.zeros(o_ref.shape, F32)
        for buf in (at_a, at_b, gw_a, gw_b):
            buf[...] = jnp.zeros(buf.shape, buf.dtype)
        for h in range(PEER_HEADS):
            st = st_ref[h]
            e_scr[2 * h] = jnp.exp(s_ref[2 * h] - st[1:2]) * st[3:4]
            e_scr[2 * h + 1] = jnp.exp(s_ref[2 * h + 1] - st[2:3])

    def step(at_w, at_r, gw_w, gw_r):
        def stage1(k):
            r0, c0 = (k // (tm // mx)) * mx, (k % (tm // mx)) * mx
            at_w[r0:r0 + mx, c0:c0 + mx] = jnp.dot(u_ref[r0:r0 + mx, :], ht_ref[:, c0:c0 + mx],
                                                   preferred_element_type=F32)

        def stage3(k):
            r0, c0 = (k // (tm // mx)) * mx, (k % (tm // mx)) * mx
            o_ref[r0:r0 + mx, c0:c0 + mx] += jnp.dot(vt_ref[0, r0:r0 + mx, :], gw_r[:, c0:c0 + mx],
                                                     preferred_element_type=F32)

        live = jnp.where((j >= 1) & (j <= n_tiles), 1.0, 0.0)
        jb = jnp.clip(j - 1, 0, n_tiles - 1)
        for h in range(PEER_HEADS):
            for rr in range(rpt):
                row = h * rpt + rr
                r = jb * rpt + rr
                thr_scr[row:row + 1, :] = st_ref[h, 0:1, :] - s_ref[2 * h, pl.ds(r, 1), :]
                e1_scr[row:row + 1, :] = e_scr[2 * h, pl.ds(r, 1), :] * live

        n1 = (te // mx) * (tm // mx)
        n3 = (d // mx) * (tm // mx)
        kh = nk // 4
        grp = 4
        n2 = (tm // LANES) * (nk // kh) * (rpt // grp)
        done1 = done3 = 0
        blk = 0
        for ch in range(tm // LANES):
            cs = slice(ch * LANES, (ch + 1) * LANES)
            for hf in range(nk // kh):
                ks = slice(hf * kh, (hf + 1) * kh)
                for g0 in range(0, rpt, grp):
                    ws = [jnp.zeros((kh, LANES), F32) for _ in range(grp)]
                    for h in range(PEER_HEADS):
                        s2c = s_ref[2 * h + 1, ks, cs]
                        e2c = e_scr[2 * h + 1, ks, cs]
                        for q in range(grp):
                            row = h * rpt + g0 + q
                            sel = jnp.where(s2c >= thr_scr[row:row + 1, cs], e2c, 0.0)
                            ws[q] = ws[q] + sel * e1_scr[row:row + 1, cs]
                    for q in range(grp):
                        r0 = (g0 + q) * nk + hf * kh
                        a = at_r[r0:r0 + kh, cs]
                        gw_w[r0:r0 + kh, cs] = (_gelu(a) * ws[q]).astype(BF16)
                    blk += 1
                    while done1 * n2 < blk * n1:
                        stage1(done1)
                        done1 += 1
                    while done3 * n2 < blk * n3:
                        stage3(done3)
                        done3 += 1

    @pl.when(j % 2 == 0)
    def _():
        step(at_a, at_b, gw_a, gw_b)

    @pl.when(j % 2 == 1)
    def _():
        step(at_b, at_a, gw_b, gw_a)


def peer_experts(h2t, u_bf, vt_bf, scores, stats, tm=PEER_TM, te=PEER_TE):
    d, t = h2t.shape
    n_tiles = vt_bf.shape[0]
    return pl.pallas_call(
        _peer_kernel,
        grid=(t // tm, n_tiles + 2),
        in_specs=[pl.BlockSpec((d, tm), lambda i, j: (0, i)),
                  pl.BlockSpec((te, d), lambda i, j: (jnp.minimum(j, n_tiles - 1), 0)),
                  pl.BlockSpec((1, d, te), lambda i, j: (jnp.clip(j - 2, 0, n_tiles - 1), 0, 0)),
                  pl.BlockSpec((2 * PEER_HEADS, PEER_KEYS, tm), lambda i, j: (0, 0, i)),
                  pl.BlockSpec((PEER_HEADS, SUBLANES, tm), lambda i, j: (0, 0, i))],
        out_specs=pl.BlockSpec((d, tm), lambda i, j: (0, i)),
        out_shape=jax.ShapeDtypeStruct((d, t), F32),
        scratch_shapes=[pltpu.VMEM((2 * PEER_HEADS, PEER_KEYS, tm), F32),
                        pltpu.VMEM((PEER_HEADS * te // PEER_KEYS, tm), F32),
                        pltpu.VMEM((PEER_HEADS * te // PEER_KEYS, tm), F32),
                        pltpu.VMEM((te, tm), F32), pltpu.VMEM((te, tm), F32),
                        pltpu.VMEM((te, tm), BF16), pltpu.VMEM((te, tm), BF16)],
        compiler_params=_params("parallel", "arbitrary"),
        name="peer_experts",
    )(h2t, u_bf, vt_bf, scores, stats)


def _final_kernel(x1_ref, pt_ref, g2_ref, fg_ref, o_ref):
    x2 = x1_ref[...] + g2_ref[0] * pt_ref[...].T
    ms = jnp.mean(x2 * x2, axis=-1, keepdims=True)
    o_ref[...] = (x2 * lax.rsqrt(ms + EPS)) * fg_ref[...]


def final_norm(x1, peer_t, g2, final_g, l, tm=512):
    t, d = x1.shape
    tpb = l // tm
    return pl.pallas_call(
        _final_kernel,
        grid=(t // tm,),
        in_specs=[pl.BlockSpec((tm, d), lambda i: (i, 0)),
                  pl.BlockSpec((d, tm), lambda i: (0, i)),
                  pl.BlockSpec((1, 1, d), lambda i: (i // tpb, 0, 0)),
                  pl.BlockSpec((1, d), lambda i: (0, 0))],
        out_specs=pl.BlockSpec((tm, d), lambda i: (i, 0)),
        out_shape=jax.ShapeDtypeStruct((t, d), F32),
        compiler_params=_params("parallel"),
        name="final_norm",
    )(x1, peer_t, g2, final_g.reshape(1, d))


def kernel(x, c, ada_w, ada_b, norm_mix_g, norm_ffn_g, w_in, cmp_pos, cmp_w1, cmp_b1, cmp_w2, cmp_b2, rel_bias,
           s5_lam_re, s5_lam_im, s5_log_step, s5_b_re, s5_b_im, s5_c_re, s5_c_im, s5_d, glu_w, glu_b,
           w_up_attn, w_up_ssm, w_out, peer_w_q, peer_sub_keys, peer_u, peer_v, final_g):
    bsz, l, d = x.shape
    assert ada_w.shape[0] == 1, "single-layer block"
    mod = adaln(c, ada_w[0], ada_b[0])
    sh1, sc1, g1, sh2, sc2, g2 = [m[:, None, :] for m in jnp.split(mod, 6, axis=-1)]

    proj = inproj(x, norm_mix_g[0], sc1, sh1, build_wcat(w_in[0]))

    n16 = l // CMP_STRIDE

    def blocks16(cols):
        return cols.reshape(bsz, l, NSA_KV_HEADS, HEAD_DIM).transpose(0, 2, 1, 3).reshape(
            bsz, NSA_KV_HEADS, n16, CMP_STRIDE * HEAD_DIM)

    cmp_kv = compress(blocks16(proj[:, :, COL_CMP:COL_CMP + KV_WIDTH]),
                      blocks16(proj[:, :, COL_CMP + KV_WIDTH:COL_CMP + 2 * KV_WIDTH]),
                      cmp_w1[0], cmp_pos[0], cmp_b1[0], cmp_w2[0], cmp_b2[0])
    diag, cmpb = attention_tables(rel_bias)
    o_attn = attention(proj, cmp_kv, diag, cmpb)

    tables = s5_tables(s5_lam_re[0], s5_lam_im[0], s5_log_step[0], s5_b_re[0], s5_b_im[0],
                       s5_c_re[0], s5_c_im[0])
    y_ssm = s5_glu(proj, tables, s5_d[0], glu_w[0], glu_b[0])

    x1, h2t = merge(x, o_attn, y_ssm, proj, g1, w_up_attn[0], w_up_ssm[0], w_out[0],
                   norm_ffn_g[0], sc2, sh2)

    t = bsz * l
    scores, stats = peer_scores(h2t, peer_w_q[0], peer_sub_keys[0])
    v_tiles = peer_v[0].reshape(PEER_EXPERTS // PEER_TE, PEER_TE, d).transpose(0, 2, 1).astype(BF16)
    peer_t = peer_experts(h2t, peer_u[0].astype(BF16), v_tiles, scores, stats)
    out = final_norm(x1.reshape(t, d), peer_t, g2, final_g, l)
    return out.reshape(bsz, l, d)
```

```python
import functools
import math

import numpy as np
import jax
import jax.numpy as jnp
from jax import lax
from jax.experimental import pallas as pl
from jax.experimental.pallas import tpu as pltpu

F32 = jnp.float32
BF16 = jnp.bfloat16
HIGHEST = lax.Precision.HIGHEST

D_MODEL = 2048
NSA_HEADS = 16
NSA_KV_HEADS = 4
NSA_GROUP = NSA_HEADS // NSA_KV_HEADS
HEAD_DIM = 64
NSA_WIDTH = NSA_HEADS * HEAD_DIM
KV_WIDTH = NSA_KV_HEADS * HEAD_DIM
CMP_LEN = 32
CMP_STRIDE = 16
CMP_HIDDEN = 256
SEL_BLOCK = 64
SEL_TOP = 16
WINDOW = 512
S5_WIDTH = 1024
S5_GROUP = 16
S5_GROUPS = S5_WIDTH // S5_GROUP
S5_STATE = 64
S5_FLAT = S5_GROUPS * S5_STATE
REL_BUCKETS = 32
REL_MAX_DIST = 128
PEER_KEYS = 128
PEER_EXPERTS = PEER_KEYS * PEER_KEYS
PEER_HEADS = 8
PEER_TOPK = 16
PEER_QDIM = 256
EPS = 1e-6
NEG_INF = -1e30
FORCE_BONUS = 1e4
LOG2E = math.log2(math.e)
MASK_BIG = 1e30

VMEM_LIMIT_BYTES = 56 * 1024 * 1024
LANES = 128
SUBLANES = 8

PROJ_WIDTH = 8192
COL_Q = 0
COL_U = 1024
COL_MERGE = 2048
COL_SEL = 6144
COL_WIN = 6656
COL_CMP = 7168
COL_GATES = 7680

ATT_TILE = 128
BIAS_SPAN = 256
ATT_SUPER = 512
ATT_ORIGIN = 2 * ATT_SUPER - ATT_TILE
ATT_TABLE_ROWS = ATT_ORIGIN + WINDOW + ATT_TILE


def _params(*sem):
    return pltpu.CompilerParams(dimension_semantics=sem, vmem_limit_bytes=VMEM_LIMIT_BYTES)


def _gelu(x):
    return 0.5 * x * (1.0 + jnp.tanh(math.sqrt(2.0 / math.pi) * (x + 0.044715 * (x * x * x))))


def _sigmoid(x):
    return 1.0 / (1.0 + jnp.exp(-x))


def _adaln_kernel(c_ref, w_ref, b_ref, o_ref):
    c = c_ref[...]
    a = c * _sigmoid(c)
    o_ref[...] = jnp.dot(a, w_ref[...], precision=HIGHEST, preferred_element_type=F32) + b_ref[...]


def adaln(c, w, b):
    bsz, d = c.shape
    n = w.shape[1]
    tn = 1024
    cp = jnp.zeros((SUBLANES, d), F32).at[:bsz].set(c)
    out = pl.pallas_call(
        _adaln_kernel,
        grid=(n // tn,),
        in_specs=[pl.BlockSpec((SUBLANES, d), lambda j: (0, 0)),
                  pl.BlockSpec((d, tn), lambda j: (0, j)),
                  pl.BlockSpec((1, tn), lambda j: (0, j))],
        out_specs=pl.BlockSpec((SUBLANES, tn), lambda j: (0, j)),
        out_shape=jax.ShapeDtypeStruct((SUBLANES, n), F32),
        compiler_params=_params("parallel"),
        name="adaln",
    )(cp, w, b.reshape(1, n))
    return out[:bsz]


def _modulated_norm(x, g, sc, sh):
    ms = jnp.mean(x * x, axis=-1, keepdims=True)
    return (x * lax.rsqrt(ms + EPS)) * g * (1.0 + sc) + sh


def _inproj_kernel(x_ref, g_ref, sc_ref, sh_ref, w_ref, o_ref, h_scr):
    @pl.when(pl.program_id(2) == 0)
    def _():
        h_scr[...] = _modulated_norm(x_ref[0], g_ref[...], sc_ref[0], sh_ref[0]).astype(BF16)

    o_ref[0] = jnp.dot(h_scr[...], w_ref[...], preferred_element_type=F32).astype(BF16)


def inproj(x, g, sc, sh, wcat, tm=1024, tn=1024):
    bsz, l, d = x.shape
    n = wcat.shape[1]
    tm = min(tm, l)
    return pl.pallas_call(
        _inproj_kernel,
        grid=(bsz, l // tm, n // tn),
        in_specs=[pl.BlockSpec((1, tm, d), lambda b, i, j: (b, i, 0)),
                  pl.BlockSpec((1, d), lambda b, i, j: (0, 0)),
                  pl.BlockSpec((1, 1, d), lambda b, i, j: (b, 0, 0)),
                  pl.BlockSpec((1, 1, d), lambda b, i, j: (b, 0, 0)),
                  pl.BlockSpec((d, tn), lambda b, i, j: (0, j))],
        out_specs=pl.BlockSpec((1, tm, tn), lambda b, i, j: (b, i, j)),
        out_shape=jax.ShapeDtypeStruct((bsz, l, n), BF16),
        scratch_shapes=[pltpu.VMEM((tm, d), BF16)],
        compiler_params=_params("parallel", "parallel", "arbitrary"),
        name="inproj",
    )(x, g.reshape(1, d), sc, sh, wcat)


def build_wcat(w_in):
    d = w_in.shape[0]
    pts = np.cumsum([NSA_WIDTH] + [KV_WIDTH] * 6 + [3 * NSA_HEADS, S5_WIDTH])
    q, kc, vc, ks, vs, kw, vw, gates, u, merge = jnp.split(w_in, pts, axis=1)

    def pair(k, v):
        return jnp.stack([k.reshape(d, NSA_KV_HEADS, HEAD_DIM), v.reshape(d, NSA_KV_HEADS, HEAD_DIM)],
                         axis=2).reshape(d, 2 * KV_WIDTH)

    gates_p = jnp.pad(gates, ((0, 0), (0, LANES - 3 * NSA_HEADS)))
    pad = jnp.zeros((d, PROJ_WIDTH - COL_GATES - LANES), w_in.dtype)
    wcat = jnp.concatenate([q * (HEAD_DIM ** -0.5 * LOG2E), u, merge, pair(ks, vs), pair(kw, vw),
                            kc, vc, gates_p, pad], axis=1)
    return wcat.astype(BF16)


def _compress_kernel(gk_ref, gv_ref, w1_ref, pos_ref, b1_ref, w2_ref, b2_ref, o_ref):
    half = CMP_STRIDE * HEAD_DIM
    nrow = gk_ref.shape[2]
    out = b2_ref[...]
    for j, g_ref in enumerate((gk_ref, gv_ref)):
        g = g_ref[0, 0]
        w1 = w1_ref[j]
        top = jnp.dot(g, w1[:half].astype(BF16), preferred_element_type=F32)
        bot = jnp.dot(g, w1[half:].astype(BF16), preferred_element_type=F32)
        cst = jnp.dot(pos_ref[j], w1, precision=HIGHEST, preferred_element_type=F32)[0:1] + b1_ref[j]
        hid = top + pltpu.roll(bot, nrow - 1, axis=0) + cst
        out = out + jnp.dot(_gelu(hid).astype(BF16), w2_ref[j].astype(BF16), preferred_element_type=F32)
    o_ref[0, 0] = out.astype(BF16)


def compress(gk, gv, w1, pos, b1, w2, b2):
    bsz, hkv, n16, wd = gk.shape
    posf = jnp.zeros((2, SUBLANES, CMP_LEN * HEAD_DIM), F32).at[:, 0].set(pos.reshape(2, -1))
    w2p = jnp.zeros((2, CMP_HIDDEN, 2 * HEAD_DIM), F32)
    w2p = w2p.at[0, :, :HEAD_DIM].set(w2[0]).at[1, :, HEAD_DIM:].set(w2[1])
    b2p = jnp.concatenate([b2[0], b2[1]]).reshape(1, 2 * HEAD_DIM)
    gspec = pl.BlockSpec((1, 1, n16, wd), lambda b, h: (b, h, 0, 0))
    full = lambda a: pl.BlockSpec(a.shape, lambda b, h: (0,) * a.ndim)
    b1r = b1.reshape(2, 1, CMP_HIDDEN)
    return pl.pallas_call(
        _compress_kernel,
        grid=(bsz, hkv),
        in_specs=[gspec, gspec, full(w1), full(posf), full(b1r), full(w2p), full(b2p)],
        out_specs=pl.BlockSpec((1, 1, n16, 2 * HEAD_DIM), lambda b, h: (b, h, 0, 0)),
        out_shape=jax.ShapeDtypeStruct((bsz, hkv, n16, 2 * HEAD_DIM), BF16),
        compiler_params=_params("parallel", "parallel"),
        name="compress",
    )(gk, gv, w1, posf, b1r, w2p, b2p)


def _t5_bucket_table(n):
    dist = np.arange(n)
    max_exact = REL_BUCKETS // 2
    ratio = np.log(np.maximum(dist, 1).astype(np.float32) / np.float32(max_exact)) / np.float32(
        math.log(REL_MAX_DIST / max_exact))
    large = np.minimum(max_exact + (ratio * np.float32(REL_BUCKETS - max_exact)).astype(np.int32), REL_BUCKETS - 1)
    return np.where(dist < max_exact, dist, large)


def _cmp_to_sel_t(n_cmp, n_sel):
    cs = np.arange(n_cmp)[:, None] * CMP_STRIDE
    ss = np.arange(n_sel)[None, :] * SEL_BLOCK
    ov = np.clip(np.minimum(cs + CMP_LEN, ss + SEL_BLOCK) - np.maximum(cs, ss), 0, None)
    return np.ascontiguousarray((ov / CMP_LEN).T.astype(np.float32))


def attention_tables(rel_bias):
    tq = ATT_TILE
    bucket = _t5_bucket_table(BIAS_SPAN)
    assert np.all(bucket[113:] == REL_BUCKETS - 1)
    bv = (rel_bias[bucket] - rel_bias[REL_BUCKETS - 1][None, :]).T * LOG2E
    heads = bv.shape[0]
    band0 = ATT_ORIGIN - BIAS_SPAN
    band = BIAS_SPAN + tq
    period = band + tq
    vext = jnp.concatenate([jnp.zeros((heads, period - BIAS_SPAN), F32), bv.astype(F32)], axis=1)
    toep = jnp.tile(vext, (1, band))[:, :band * (period - 1)].reshape(heads, band, period - 1)[:, :, :tq]
    tab = jnp.pad(toep, ((0, 0), (band0, ATT_TABLE_ROWS - band0 - band), (0, 0)))
    tab = tab.reshape(NSA_KV_HEADS, NSA_GROUP, ATT_TABLE_ROWS, tq).transpose(0, 2, 1, 3)
    tab = tab.reshape(NSA_KV_HEADS, ATT_TABLE_ROWS, NSA_GROUP * tq)
    lo = 16 * CMP_STRIDE - (8 * CMP_STRIDE - CMP_LEN + 1)
    fpad = jnp.pad(bv.astype(F32), ((0, 0), (lo, tq)))
    cmpb = jnp.stack([fpad[:, lo + (8 * CMP_STRIDE - CMP_LEN + 1) - CMP_STRIDE * r:][:, :tq] for r in range(16)],
                     axis=1)
    cmpb = cmpb.reshape(NSA_KV_HEADS, NSA_GROUP, 16, tq).transpose(0, 2, 1, 3).reshape(NSA_KV_HEADS, 16, NSA_GROUP * tq)
    return tab, cmpb


def _importance_taps(n_cmp, n_sel):
    m = _cmp_to_sel_t(n_cmp, n_sel)
    ratio = SEL_BLOCK // CMP_STRIDE
    cols = np.nonzero(m[1])[0]
    taps = [(int(c - ratio), float(m[1, c])) for c in cols]
    for j in range(n_sel):
        want = np.zeros(n_cmp, np.float32)
        for k, w in taps:
            if 0 <= ratio * j + k < n_cmp:
                want[ratio * j + k] = w
        assert np.array_equal(want, m[j])
    return ratio, taps


def _attn_kernel(q_ref, sel_ref, selt_ref, win_ref, wint_ref, cmp_ref, cmpt_ref, gate_ref, tab_ref, cmpb_ref,
                 ind_ref, kmask_ref, o_ref, m_scr, acc_scr, sel_scr, s_scr, p_scr, *, taps, ratio):
    tq = ATT_TILE
    sk = ATT_SUPER
    cols = NSA_GROUP * tq
    kvh = pl.program_id(1)
    qt = pl.program_id(2)
    n_cmp = cmp_ref.shape[2]
    n_sel = sel_scr.shape[0]
    per = sk // tq
    pad = SUBLANES

    qT = q_ref[0]
    zpad = jnp.zeros((HEAD_DIM, tq), BF16)
    qpad = jnp.concatenate(
        [jnp.concatenate([qT[g * HEAD_DIM:(g + 1) * HEAD_DIM], zpad], axis=0) for g in range(NSA_GROUP)],
        axis=1)
    col_i = lax.broadcasted_iota(jnp.int32, (1, cols), 1) & (tq - 1)
    t_col = qt * tq + col_i

    ckv = cmp_ref[0, 0]
    s_scr[0:pad, :] = jnp.zeros((pad, cols), F32)
    s_scr[pad + n_cmp:, :] = jnp.zeros((pad, cols), F32)
    s_scr[pad:pad + n_cmp, :] = jnp.dot(ckv, qpad, preferred_element_type=F32)
    slot0 = pl.multiple_of(qt * (tq // CMP_STRIDE), SUBLANES)
    s_scr[pl.ds(slot0, 16), :] += cmpb_ref[0]
    s = s_scr[pad:pad + n_cmp, :]
    n_idx = lax.broadcasted_iota(jnp.int32, (n_cmp, cols), 0)
    valid = (n_idx * CMP_STRIDE + (CMP_LEN - 1)) <= t_col
    s = jnp.where(valid, s, NEG_INF)
    mx = jnp.max(s, axis=0, keepdims=True)
    p = jnp.where(valid, jnp.exp2(s - mx), 0.0)
    den = jnp.sum(p, axis=0, keepdims=True)
    p = p * jnp.where(den > 0.0, 1.0 / den, 0.0)
    o_c = jnp.dot(cmpt_ref[0, 0], p.astype(BF16), preferred_element_type=F32)

    p_scr[0:pad, :] = jnp.zeros((pad, tq), F32)
    p_scr[pad:, :] = p[:, 0:tq] + p[:, tq:2 * tq] + p[:, 2 * tq:3 * tq] + p[:, 3 * tq:4 * tq]
    imp = jnp.zeros((n_sel, tq), F32)
    for k, w in taps:
        imp = imp + w * p_scr[pl.ds(pad + k, n_sel, stride=ratio), :]
    blk = lax.broadcasted_iota(jnp.int32, (n_sel, tq), 0)
    qi = lax.broadcasted_iota(jnp.int32, (n_sel, tq), 1)
    cur = jnp.right_shift(qt * tq + qi, 6)

    @pl.when((qt + 1) * tq <= SEL_TOP * SEL_BLOCK)
    def _():
        sel_scr[...] = jnp.where(blk <= cur, 1.0, 0.0)

    @pl.when((qt + 1) * tq > SEL_TOP * SEL_BLOCK)
    def _():
        forced = (blk == 0) | (blk == cur) | (blk == cur - 1)
        val = jnp.where(blk <= cur, imp + FORCE_BONUS * forced.astype(F32), NEG_INF)
        rank = jnp.zeros((n_sel, tq), F32)
        for k in range(n_sel):
            rk = val[k:k + 1, :]
            beats = (rk > val) | ((rk == val) & (blk > k))
            rank = rank + beats.astype(F32)
        sel_scr[...] = jnp.where(rank < float(SEL_TOP), 1.0, 0.0)

    m_scr[...] = jnp.full(m_scr.shape, NEG_INF, F32)
    acc_scr[...] = jnp.zeros(acc_scr.shape, F32)
    st_last = qt // per
    dq = qt - st_last * per
    blocks = sk // SEL_BLOCK
    q_top = qpad[0:HEAD_DIM]
    q_fill = jnp.zeros((HEAD_DIM - 2 * SUBLANES, cols), BF16)

    def sel_tile(st, table_row, causal_shift, live=None):
        kv = sel_ref[0, pl.ds(pl.multiple_of(st * sk, sk), sk), :]
        kvm = kv * kmask_ref[...] + ind_ref[...]
        picked = sel_scr[pl.ds(pl.multiple_of(st * blocks, blocks), blocks), :]
        if live is not None:
            picked = picked * live
        off = (picked - 1.0) * MASK_BIG
        off = jnp.concatenate([off, jnp.zeros_like(off)], axis=0)
        off = jnp.concatenate([off] * NSA_GROUP, axis=1).astype(BF16)
        sc = jnp.dot(kvm, jnp.concatenate([q_top, off, q_fill], axis=0), preferred_element_type=F32)
        if table_row is not None:
            sc = sc + tab_ref[0, pl.ds(pl.multiple_of(table_row, tq), sk), :]
        if causal_shift is not None:
            rel = col_i - lax.broadcasted_iota(jnp.int32, (sk, cols), 0)
            sc = jnp.where(rel + causal_shift >= 0, sc, NEG_INF)
        m_old = m_scr[...]
        m_new = jnp.maximum(m_old, jnp.max(sc, axis=0, keepdims=True))
        alpha = jnp.exp2(m_old - m_new)
        pe = jnp.exp2(sc - m_new).astype(BF16)
        acc_scr[...] = alpha * acc_scr[...] + jnp.dot(selt_ref[0, 0, st], pe, preferred_element_type=F32)
        m_scr[...] = m_new

    def sel_body(st, carry):
        sel_tile(st, None, None)
        return carry

    lax.fori_loop(0, jnp.maximum(st_last - 1, 0), sel_body, 0)

    has_prev = jnp.where(st_last >= 1, 1.0, 0.0)
    sel_tile(jnp.maximum(st_last - 1, 0), ATT_ORIGIN - sk - dq * tq, None, live=has_prev)
    sel_tile(st_last, ATT_ORIGIN - dq * tq, dq * tq)
    o_s = acc_scr[HEAD_DIM:, :] * (1.0 / acc_scr[0:1, :])

    wk = WINDOW + tq
    n_w = wk // tq
    k0 = jnp.maximum(qt - WINDOW // tq, 0)
    dw = (qt - k0) * tq
    kv = win_ref[0, pl.ds(pl.multiple_of(k0 * tq, tq), wk), :]
    sc = jnp.dot(kv, qpad, preferred_element_type=F32)
    sc = sc + tab_ref[0, pl.ds(pl.multiple_of(ATT_ORIGIN - dw, tq), wk), :]
    dist = col_i - lax.broadcasted_iota(jnp.int32, (wk, cols), 0) + dw
    sc = jnp.where((dist >= 0) & (dist < WINDOW), sc, NEG_INF)
    mw = jnp.max(sc, axis=0, keepdims=True)
    pw = jnp.exp2(sc - mw).astype(BF16)
    ow = jnp.dot(wint_ref[0, 0, k0], pw[0:tq], preferred_element_type=F32)
    for m in range(1, n_w):
        ow = ow + jnp.dot(wint_ref[0, 0, k0 + m], pw[m * tq:(m + 1) * tq], preferred_element_type=F32)
    o_w = ow[HEAD_DIM:, :] * (1.0 / ow[0:1, :])

    branches = (o_c[HEAD_DIM:, :], o_s, o_w)
    outs = []
    for g in range(NSA_GROUP):
        acc = jnp.zeros((HEAD_DIM, tq), F32)
        for br in range(3):
            grow = gate_ref[0, pl.ds((kvh * NSA_GROUP + g) * 3 + br, 1), :]
            acc = acc + _sigmoid(grow) * branches[br][:, g * tq:(g + 1) * tq]
        outs.append(acc)
    for pr in range(NSA_GROUP // 2):
        pair = jnp.concatenate([outs[2 * pr], outs[2 * pr + 1]], axis=0)
        o_ref[0, :, pr * LANES:(pr + 1) * LANES] = pair.T.astype(BF16)


def attention(proj, cmp_kv, table, cmpb_t):
    bsz, l, _ = proj.shape
    tq = ATT_TILE
    sk = ATT_SUPER
    assert l % sk == 0 and l >= WINDOW + tq
    n_kt = l // tq
    n_st = l // sk
    n_cmp = cmp_kv.shape[2]
    n_sel = l // SEL_BLOCK
    ratio, taps = _importance_taps(n_cmp, n_sel)
    gw = NSA_GROUP * HEAD_DIM
    kvw = 2 * KV_WIDTH
    q_t = proj[:, :, COL_Q:COL_Q + NSA_WIDTH].transpose(0, 2, 1)

    def tiles_t(c0, nt):
        t = proj[:, :, c0:c0 + kvw].reshape(bsz, nt, l // nt, NSA_KV_HEADS, LANES).transpose(0, 3, 1, 4, 2)
        first = lax.broadcasted_iota(jnp.int32, t.shape, 3) == 0
        return jnp.where(first, jnp.ones((), BF16), t)

    sel_t = tiles_t(COL_SEL, n_st)
    win_t = tiles_t(COL_WIN, n_kt)
    cmp_t = cmp_kv.transpose(0, 1, 3, 2)
    gates_t = proj[:, :, COL_GATES:COL_GATES + LANES].astype(F32).transpose(0, 2, 1)
    key = np.arange(sk)[:, None]
    lane = np.arange(LANES)[None, :]
    ind = jnp.asarray((lane == HEAD_DIM + key // SEL_BLOCK).astype(np.float32), BF16)
    kmask = jnp.asarray(np.broadcast_to(lane < HEAD_DIM, (sk, LANES)).astype(np.float32), BF16)
    const = lambda a: pl.BlockSpec(a.shape, lambda b, h, i: (0,) * a.ndim)
    return pl.pallas_call(
        functools.partial(_attn_kernel, taps=taps, ratio=ratio),
        grid=(bsz, NSA_KV_HEADS, n_kt),
        in_specs=[pl.BlockSpec((1, gw, tq), lambda b, h, i: (b, h, i)),
                  pl.BlockSpec((1, l, LANES), lambda b, h, i: (b, 0, COL_SEL // LANES + h)),
                  pl.BlockSpec((1, 1, n_st, LANES, sk), lambda b, h, i: (b, h, 0, 0, 0)),
                  pl.BlockSpec((1, l, LANES), lambda b, h, i: (b, 0, COL_WIN // LANES + h)),
                  pl.BlockSpec((1, 1, n_kt, LANES, tq), lambda b, h, i: (b, h, 0, 0, 0)),
                  pl.BlockSpec((1, 1, n_cmp, LANES), lambda b, h, i: (b, h, 0, 0)),
                  pl.BlockSpec((1, 1, LANES, n_cmp), lambda b, h, i: (b, h, 0, 0)),
                  pl.BlockSpec((1, LANES, tq), lambda b, h, i: (b, 0, i)),
                  pl.BlockSpec((1, ATT_TABLE_ROWS, NSA_GROUP * tq), lambda b, h, i: (h, 0, 0)),
                  pl.BlockSpec((1, 16, NSA_GROUP * tq), lambda b, h, i: (h, 0, 0)),
                  const(ind), const(kmask)],
        out_specs=pl.BlockSpec((1, tq, gw), lambda b, h, i: (b, i, h)),
        out_shape=jax.ShapeDtypeStruct((bsz, l, NSA_WIDTH), BF16),
        scratch_shapes=[pltpu.VMEM((1, NSA_GROUP * tq), F32),
                        pltpu.VMEM((LANES, NSA_GROUP * tq), F32),
                        pltpu.VMEM((n_sel, tq), F32),
                        pltpu.VMEM((n_cmp + 2 * SUBLANES, NSA_GROUP * tq), F32),
                        pltpu.VMEM((n_cmp + SUBLANES, tq), F32)],
        compiler_params=_params("parallel", "parallel", "arbitrary"),
        name="nsa_attention",
    )(q_t, proj, sel_t, proj, win_t, cmp_kv, cmp_t, gates_t, table, cmpb_t, ind, kmask)


S5_BLOCK = 256
S5_CHUNK = S5_BLOCK // SUBLANES
S5_SLAB = 1024


def _s5_kernel(u_ref, perm_ref, permt_ref, bre_ref, bim_ref, are_ref, aim_ref, ace_ref, aci_ref, cre_ref, cim_ref,
               d_ref, gw_ref, gb_ref, o_ref, bu_re, bu_im, xs_re, xs_im, carry_re, carry_im, init_re, init_im):
    nr = S5_GROUPS // 16
    rw = 16 * S5_GROUP
    sw = 16 * S5_STATE

    @pl.when(pl.program_id(1) == 0)
    def _():
        carry_re[...] = jnp.zeros(carry_re.shape, F32)
        carry_im[...] = jnp.zeros(carry_im.shape, F32)

    perm = perm_ref[...]
    up = jnp.dot(perm, u_ref[0], preferred_element_type=F32).astype(BF16)
    for r in range(nr):
        ur = up[:, r * rw:(r + 1) * rw]
        bu_re[:, r * sw:(r + 1) * sw] = jnp.dot(ur, bre_ref[r], preferred_element_type=F32)
        bu_im[:, r * sw:(r + 1) * sw] = jnp.dot(ur, bim_ref[r], preferred_element_type=F32)

    nslab = S5_FLAT // S5_SLAB
    row8 = lax.broadcasted_iota(jnp.int32, (SUBLANES, S5_SLAB), 0)
    for sl in range(nslab):
        cs = slice(sl * S5_SLAB, (sl + 1) * S5_SLAB)
        a_re = jnp.broadcast_to(are_ref[:, cs], (SUBLANES, S5_SLAB))
        a_im = jnp.broadcast_to(aim_ref[:, cs], (SUBLANES, S5_SLAB))

        def step1(s, st):
            s_re, s_im = st
            off = pl.multiple_of(s * SUBLANES, SUBLANES)
            n_re = a_re * s_re - a_im * s_im + bu_re[pl.ds(off, SUBLANES), cs]
            n_im = a_re * s_im + a_im * s_re + bu_im[pl.ds(off, SUBLANES), cs]
            return n_re, n_im

        zero = jnp.zeros((SUBLANES, S5_SLAB), F32)
        e_re, e_im = lax.fori_loop(0, S5_CHUNK, step1, (zero, zero))

        ac_re = ace_ref[:, cs]
        ac_im = aci_ref[:, cs]
        x_re = carry_re[:, cs]
        x_im = carry_im[:, cs]
        st_re = jnp.zeros((SUBLANES, S5_SLAB), F32)
        st_im = jnp.zeros((SUBLANES, S5_SLAB), F32)
        for c in range(SUBLANES):
            st_re = jnp.where(row8 == c, x_re, st_re)
            st_im = jnp.where(row8 == c, x_im, st_im)
            n_re = ac_re * x_re - ac_im * x_im + e_re[c:c + 1]
            n_im = ac_re * x_im + ac_im * x_re + e_im[c:c + 1]
            x_re, x_im = n_re, n_im
        carry_re[:, cs] = x_re
        carry_im[:, cs] = x_im
        init_re[:, cs] = st_re
        init_im[:, cs] = st_im

        def step2(s, st):
            s_re, s_im = st
            off = pl.multiple_of(s * SUBLANES, SUBLANES)
            n_re = a_re * s_re - a_im * s_im + bu_re[pl.ds(off, SUBLANES), cs]
            n_im = a_re * s_im + a_im * s_re + bu_im[pl.ds(off, SUBLANES), cs]
            xs_re[pl.ds(off, SUBLANES), cs] = n_re
            xs_im[pl.ds(off, SUBLANES), cs] = n_im
            return n_re, n_im

        lax.fori_loop(0, S5_CHUNK, step2, (st_re, st_im))

    ys = []
    for r in range(nr):
        xr = xs_re[:, r * sw:(r + 1) * sw].astype(BF16)
        xi = xs_im[:, r * sw:(r + 1) * sw].astype(BF16)
        ys.append(jnp.dot(xr, cre_ref[r], preferred_element_type=F32)
                  - jnp.dot(xi, cim_ref[r], preferred_element_type=F32))
    y = jnp.concatenate(ys, axis=1) + d_ref[...] * up.astype(F32)
    y = _gelu(y)
    z = jnp.dot(y.astype(BF16), gw_ref[...], preferred_element_type=F32) + gb_ref[...]
    y = (y * _sigmoid(z)).astype(BF16)
    o_ref[0] = jnp.dot(permt_ref[...], y, preferred_element_type=F32).astype(BF16)


def s5_tables(lam_re, lam_im, log_step, b_re, b_im, c_re, c_im):
    f32 = F32
    lam_re, lam_im = lam_re.astype(f32), lam_im.astype(f32)
    step = jnp.exp(log_step.astype(f32))[:, None]
    mag = jnp.exp(lam_re * step)
    lb_re, lb_im = mag * jnp.cos(lam_im * step), mag * jnp.sin(lam_im * step)
    den = lam_re * lam_re + lam_im * lam_im
    n_re = lb_re - 1.0
    f_re = (n_re * lam_re + lb_im * lam_im) / den
    f_im = (lb_im * lam_re - n_re * lam_im) / den
    b_re, b_im = b_re.astype(f32), b_im.astype(f32)
    bb_re = f_re[..., None] * b_re - f_im[..., None] * b_im
    bb_im = f_re[..., None] * b_im + f_im[..., None] * b_re
    magc = jnp.exp(lam_re * step * S5_CHUNK)
    ac_re, ac_im = magc * jnp.cos(lam_im * step * S5_CHUNK), magc * jnp.sin(lam_im * step * S5_CHUNK)

    nr = S5_GROUPS // 16
    eye = jnp.eye(16, dtype=f32)

    def b_blocks(bb):
        bb = bb.reshape(nr, 16, S5_STATE, S5_GROUP)
        return jnp.einsum('rgph,gk->rghkp', bb, eye).reshape(nr, 16 * S5_GROUP, 16 * S5_STATE).astype(BF16)

    def c_blocks(cc):
        cc = cc.astype(f32).reshape(nr, 16, S5_GROUP, S5_STATE)
        return jnp.einsum('rghp,gk->rgpkh', cc, eye).reshape(nr, 16 * S5_STATE, 16 * S5_GROUP).astype(BF16)

    flat = lambda a: a.reshape(1, S5_FLAT)
    return (b_blocks(bb_re), b_blocks(bb_im), flat(lb_re), flat(lb_im), flat(ac_re), flat(ac_im),
            c_blocks(c_re), c_blocks(c_im))


def _s5_perm():
    p = np.zeros((S5_BLOCK, S5_BLOCK), np.float32)
    for c in range(SUBLANES):
        for s in range(S5_CHUNK):
            p[s * SUBLANES + c, c * S5_CHUNK + s] = 1.0
    return p


def s5_glu(proj, tables, d_skip, glu_w, glu_b):
    bsz, l, _ = proj.shape
    tb = S5_BLOCK
    bre, bim, are, aim, ace, aci, cre, cim = tables
    perm = jnp.asarray(_s5_perm(), BF16)
    permt = jnp.asarray(_s5_perm().T, BF16)
    full = lambda a: pl.BlockSpec(a.shape, lambda b, i: (0,) * a.ndim)
    d2 = d_skip.reshape(1, S5_WIDTH).astype(F32)
    gb2 = glu_b.reshape(1, S5_WIDTH).astype(F32)
    gw = glu_w.astype(BF16)
    return pl.pallas_call(
        _s5_kernel,
        grid=(bsz, l // tb),
        in_specs=[pl.BlockSpec((1, tb, S5_WIDTH), lambda b, i: (b, i, COL_U // S5_WIDTH)),
                  full(perm), full(permt), full(bre), full(bim), full(are), full(aim), full(ace), full(aci),
                  full(cre), full(cim), full(d2), full(gw), full(gb2)],
        out_specs=pl.BlockSpec((1, tb, S5_WIDTH), lambda b, i: (b, i, 0)),
        out_shape=jax.ShapeDtypeStruct((bsz, l, S5_WIDTH), BF16),
        scratch_shapes=[pltpu.VMEM((tb, S5_FLAT), F32), pltpu.VMEM((tb, S5_FLAT), F32),
                        pltpu.VMEM((tb, S5_FLAT), F32), pltpu.VMEM((tb, S5_FLAT), F32),
                        pltpu.VMEM((1, S5_FLAT), F32), pltpu.VMEM((1, S5_FLAT), F32),
                        pltpu.VMEM((SUBLANES, S5_FLAT), F32), pltpu.VMEM((SUBLANES, S5_FLAT), F32)],
        compiler_params=_params("parallel", "arbitrary"),
        name="s5_glu",
    )(proj, perm, permt, bre, bim, are, aim, ace, aci, cre, cim, d2, gw, gb2)


def _merge_kernel(x_ref, oa_ref, ys_ref, ga_ref, gb_ref, g1_ref, wa_ref, ws_ref, wo_ref,
                  ng_ref, sc_ref, sh_ref, x1_ref, h2_ref):
    a = jnp.dot(oa_ref[0], wa_ref[...], preferred_element_type=F32)
    s = jnp.dot(ys_ref[0], ws_ref[...], preferred_element_type=F32)
    mixed = _sigmoid(ga_ref[0].astype(F32)) * a + _sigmoid(gb_ref[0].astype(F32)) * s
    out = jnp.dot(mixed.astype(BF16), wo_ref[...], preferred_element_type=F32)
    x1 = x_ref[0] + g1_ref[0] * out
    x1_ref[0] = x1
    h2_ref[...] = _modulated_norm(x1, ng_ref[...], sc_ref[0], sh_ref[0]).T.astype(BF16)


def merge(x, o_attn, y_ssm, proj, g1, w_up_attn, w_up_ssm, w_out, norm_g, sc2, sh2, tm=256):
    bsz, l, d = x.shape
    tok = lambda w, cb=0: pl.BlockSpec((1, tm, w), lambda b, i: (b, i, cb))
    vec = pl.BlockSpec((1, 1, d), lambda b, i: (b, 0, 0))
    full = lambda a: pl.BlockSpec(a.shape, lambda b, i: (0,) * a.ndim)
    wa, ws, wo = w_up_attn.astype(BF16), w_up_ssm.astype(BF16), w_out.astype(BF16)
    ng = norm_g.reshape(1, d)
    return pl.pallas_call(
        _merge_kernel,
        grid=(bsz, l // tm),
        in_specs=[tok(d), tok(NSA_WIDTH), tok(S5_WIDTH), tok(d, COL_MERGE // d), tok(d, COL_MERGE // d + 1),
                  vec, full(wa), full(ws), full(wo), full(ng), vec, vec],
        out_specs=[tok(d), pl.BlockSpec((d, tm), lambda b, i: (0, b * (l // tm) + i))],
        out_shape=[jax.ShapeDtypeStruct((bsz, l, d), F32), jax.ShapeDtypeStruct((d, bsz * l), BF16)],
        compiler_params=_params("parallel", "parallel"),
        name="merge",
    )(x, o_attn, y_ssm, proj, proj, g1, wa, ws, wo, ng, sc2, sh2)


PEER_SCORE_TM = SUBLANES * LANES
PEER_RANKS = PEER_TOPK + 1
_PAIR_COUNTS = [PEER_RANKS // (a + 1) for a in range(PEER_RANKS)]
_N_PAIRS = sum(_PAIR_COUNTS)


def _rows_to_vregs(x):
    v = [x[:, b * LANES:(b + 1) * LANES] for b in range(SUBLANES)]
    sub = lax.broadcasted_iota(jnp.int32, (SUBLANES, LANES), 0)
    for dist in (4, 2, 1):
        keep = (sub & dist) == 0
        nv = list(v)
        for i in range(SUBLANES):
            if i & dist:
                continue
            lo, hi = v[i], v[i + dist]
            nv[i] = jnp.where(keep, lo, pltpu.roll(hi, dist, axis=0))
            nv[i + dist] = jnp.where(keep, pltpu.roll(lo, SUBLANES - dist, axis=0), hi)
        v = nv
    return v


def _tree_max(x):
    while x.shape[0] > 1:
        n = x.shape[0]
        half = n // 2
        top = jnp.maximum(x[:half], x[half:2 * half])
        x = top if n % 2 == 0 else jnp.concatenate([top, x[2 * half:]], axis=0)
    return x[0]


def _peer_score_kernel(ht_ref, wqt_ref, keys_ref, s_ref, st_ref, q_scr, key_scr, top_scr, cand_scr, best_scr):
    half = PEER_QDIM // 2
    nk = PEER_KEYS
    chunk = 2 * SUBLANES

    def drain(ref, out_ref, n_out):
        n = ref.shape[0]

        def body(r, prev):
            acc = jnp.full((chunk, SUBLANES, LANES), -jnp.inf, F32)
            for c0 in range(0, n, chunk):
                part = ref[c0:min(c0 + chunk, n)]
                part = jnp.where(part < prev[None], part, -jnp.inf)
                if part.shape[0] < chunk:
                    part = jnp.concatenate(
                        [part, jnp.full((chunk - part.shape[0], SUBLANES, LANES), -jnp.inf, F32)], axis=0)
                acc = jnp.maximum(acc, part)
            m = _tree_max(acc)
            out_ref[r] = m
            return m

        lax.fori_loop(0, n_out, body, jnp.full((SUBLANES, LANES), jnp.inf, F32))

    q_scr[...] = jnp.dot(wqt_ref[...], ht_ref[...], preferred_element_type=F32)

    def head(h, carry):
        for c in range(2):
            hc = 2 * h + c
            qt = q_scr[pl.ds(pl.multiple_of(hc * half, half), half), :]
            sc = jnp.dot(keys_ref[h, c], qt, precision=HIGHEST, preferred_element_type=F32)
            s_ref[hc] = sc
            for kg in range(nk // SUBLANES):
                rows = _rows_to_vregs(sc[kg * SUBLANES:(kg + 1) * SUBLANES, :])
                for k in range(SUBLANES):
                    key_scr[c, kg * SUBLANES + k] = rows[k]
            drain(key_scr.at[c], top_scr.at[c], PEER_RANKS)
        i = 0
        for a in range(PEER_RANKS):
            for bb in range(_PAIR_COUNTS[a]):
                cand_scr[i] = top_scr[0, a] + top_scr[1, bb]
                i += 1
        drain(cand_scr, best_scr, PEER_RANKS)
        theta = 0.5 * (best_scr[PEER_TOPK - 1] + best_scr[PEER_TOPK])
        z = jnp.zeros((SUBLANES, LANES), F32)
        for r in range(PEER_TOPK):
            z = z + jnp.exp(best_scr[r] - best_scr[0])
        for row, val in enumerate((theta, top_scr[0, 0], top_scr[1, 0], 1.0 / z)):
            for blk in range(SUBLANES):
                st_ref[h, row:row + 1, blk * LANES:(blk + 1) * LANES] = val[blk:blk + 1, :]
        st_ref[h, 4:, :] = jnp.zeros((SUBLANES - 4, st_ref.shape[2]), F32)
        return carry

    lax.fori_loop(0, PEER_HEADS, head, 0)


def peer_scores(h2t, w_q, sub_keys, tm=PEER_SCORE_TM):
    d, t = h2t.shape
    wqt = w_q.T.astype(BF16)
    nk = PEER_KEYS
    return pl.pallas_call(
        _peer_score_kernel,
        grid=(t // tm,),
        in_specs=[pl.BlockSpec((d, tm), lambda i: (0, i)),
                  pl.BlockSpec(wqt.shape, lambda i: (0, 0)),
                  pl.BlockSpec(sub_keys.shape, lambda i: (0, 0, 0, 0))],
        out_specs=[pl.BlockSpec((2 * PEER_HEADS, nk, tm), lambda i: (0, 0, i)),
                   pl.BlockSpec((PEER_HEADS, SUBLANES, tm), lambda i: (0, 0, i))],
        out_shape=[jax.ShapeDtypeStruct((2 * PEER_HEADS, nk, t), F32),
                   jax.ShapeDtypeStruct((PEER_HEADS, SUBLANES, t), F32)],
        scratch_shapes=[pltpu.VMEM((wqt.shape[0], tm), F32),
                        pltpu.VMEM((2, nk, SUBLANES, LANES), F32),
                        pltpu.VMEM((2, PEER_RANKS, SUBLANES, LANES), F32),
                        pltpu.VMEM((_N_PAIRS, SUBLANES, LANES), F32),
                        pltpu.VMEM((PEER_RANKS, SUBLANES, LANES), F32)],
        compiler_params=_params("parallel"),
        name="peer_scores",
    )(h2t, wqt, sub_keys)


PEER_TM = 512
PEER_TE = 1024
PEER_MXU_TILE = 256


def _peer_kernel(ht_ref, u_ref, vt_ref, s_ref, st_ref, o_ref, e_scr, thr_scr, e1_scr, at_a, at_b, gw_a, gw_b,
                 *, n_static):
    j = pl.program_id(1)
    nk = PEER_KEYS
    d, tm = ht_ref.shape
    te = u_ref.shape[0]
    rpt = te // nk
    mx = PEER_MXU_TILE

    @pl.when(j == 0)
    def _():
        o_ref[...] = jnp.zeros(o_ref.shape, F32)
        for h in range(PEER_HEADS):
            st = st_ref[h]
            e_scr[2 * h] = jnp.exp(s_ref[2 * h] - st[1:2]) * st[3:4]
            e_scr[2 * h + 1] = jnp.exp(s_ref[2 * h + 1] - st[2:3])

    def step(at_w, at_r, gw_w, gw_r, do1, do2, do3):
        def stage1(k):
            r0, c0 = (k // (tm // mx)) * mx, (k % (tm // mx)) * mx
            at_w[r0:r0 + mx, c0:c0 + mx] = jnp.dot(u_ref[r0:r0 + mx, :], ht_ref[:, c0:c0 + mx],
                                                   preferred_element_type=F32)

        def stage3(k):
            r0, c0 = (k // (tm // mx)) * mx, (k % (tm // mx)) * mx
            o_ref[r0:r0 + mx, c0:c0 + mx] += jnp.dot(vt_ref[0, r0:r0 + mx, :], gw_r[:, c0:c0 + mx],
                                                     preferred_element_type=F32)

        n1 = (te // mx) * (tm // mx) if do1 else 0
        n3 = (d // mx) * (tm // mx) if do3 else 0
        if not do2:
            for k in range(n1):
                stage1(k)
            for k in range(n3):
                stage3(k)
            return

        for h in range(PEER_HEADS):
            for rr in range(rpt):
                row = h * rpt + rr
                r = (j - 1) * rpt + rr
                thr_scr[row:row + 1, :] = st_ref[h, 0:1, :] - s_ref[2 * h, pl.ds(r, 1), :]
                e1_scr[row:row + 1, :] = e_scr[2 * h, pl.ds(r, 1), :]

        kh = nk // 4
        grp = 4
        n2 = (tm // LANES) * (nk // kh) * (rpt // grp)
        done1 = done3 = 0
        blk = 0
        for ch in range(tm // LANES):
            cs = slice(ch * LANES, (ch + 1) * LANES)
            for hf in range(nk // kh):
                ks = slice(hf * kh, (hf + 1) * kh)
                for g0 in range(0, rpt, grp):
                    ws = [jnp.zeros((kh, LANES), F32) for _ in range(grp)]
                    for h in range(PEER_HEADS):
                        s2c = s_ref[2 * h + 1, ks, cs]
                        e2c = e_scr[2 * h + 1, ks, cs]
                        for q in range(grp):
                            row = h * rpt + g0 + q
                            sel = jnp.where(s2c >= thr_scr[row:row + 1, cs], e2c, 0.0)
                            ws[q] = ws[q] + sel * e1_scr[row:row + 1, cs]
                    for q in range(grp):
                        r0 = (g0 + q) * nk + hf * kh
                        a = at_r[r0:r0 + kh, cs]
                        gw_w[r0:r0 + kh, cs] = (_gelu(a) * ws[q]).astype(BF16)
                    blk += 1
                    while done1 * n2 < blk * n1:
                        stage1(done1)
                        done1 += 1
                    while done3 * n2 < blk * n3:
                        stage3(done3)
                        done3 += 1

    bufs = ((at_a, at_b, gw_a, gw_b), (at_b, at_a, gw_b, gw_a))
    steady = (j >= 2) & (j < n_static)
    for jj, flags in ((0, (True, False, False)), (1, (True, True, False)),
                      (n_static, (False, True, True)), (n_static + 1, (False, False, True))):
        pl.when(j == jj)(functools.partial(step, *bufs[jj % 2], *flags))
    for par in range(2):
        pl.when(steady & (j % 2 == par))(functools.partial(step, *bufs[par], True, True, True))


def peer_experts(h2t, u_bf, vt_bf, scores, stats, tm=PEER_TM, te=PEER_TE):
    d, t = h2t.shape
    n_tiles = vt_bf.shape[0]
    return pl.pallas_call(
        functools.partial(_peer_kernel, n_static=n_tiles),
        grid=(t // tm, n_tiles + 2),
        in_specs=[pl.BlockSpec((d, tm), lambda i, j: (0, i)),
                  pl.BlockSpec((te, d), lambda i, j: (jnp.minimum(j, n_tiles - 1), 0)),
                  pl.BlockSpec((1, d, te), lambda i, j: (jnp.clip(j - 2, 0, n_tiles - 1), 0, 0)),
                  pl.BlockSpec((2 * PEER_HEADS, PEER_KEYS, tm), lambda i, j: (0, 0, i)),
                  pl.BlockSpec((PEER_HEADS, SUBLANES, tm), lambda i, j: (0, 0, i))],
        out_specs=pl.BlockSpec((d, tm), lambda i, j: (0, i)),
        out_shape=jax.ShapeDtypeStruct((d, t), F32),
        scratch_shapes=[pltpu.VMEM((2 * PEER_HEADS, PEER_KEYS, tm), F32),
                        pltpu.VMEM((PEER_HEADS * te // PEER_KEYS, tm), F32),
                        pltpu.VMEM((PEER_HEADS * te // PEER_KEYS, tm), F32),
                        pltpu.VMEM((te, tm), F32), pltpu.VMEM((te, tm), F32),
                        pltpu.VMEM((te, tm), BF16), pltpu.VMEM((te, tm), BF16)],
        compiler_params=_params("parallel", "arbitrary"),
        name="peer_experts",
    )(h2t, u_bf, vt_bf, scores, stats)


def _final_kernel(x1_ref, pt_ref, g2_ref, fg_ref, o_ref):
    x2 = x1_ref[...] + g2_ref[0] * pt_ref[...].T
    ms = jnp.mean(x2 * x2, axis=-1, keepdims=True)
    o_ref[...] = (x2 * lax.rsqrt(ms + EPS)) * fg_ref[...]


def final_norm(x1, peer_t, g2, final_g, l, tm=512):
    t, d = x1.shape
    tpb = l // tm
    return pl.pallas_call(
        _final_kernel,
        grid=(t // tm,),
        in_specs=[pl.BlockSpec((tm, d), lambda i: (i, 0)),
                  pl.BlockSpec((d, tm), lambda i: (0, i)),
                  pl.BlockSpec((1, 1, d), lambda i: (i // tpb, 0, 0)),
                  pl.BlockSpec((1, d), lambda i: (0, 0))],
        out_specs=pl.BlockSpec((tm, d), lambda i: (i, 0)),
        out_shape=jax.ShapeDtypeStruct((t, d), F32),
        compiler_params=_params("parallel"),
        name="final_norm",
    )(x1, peer_t, g2, final_g.reshape(1, d))


def kernel(x, c, ada_w, ada_b, norm_mix_g, norm_ffn_g, w_in, cmp_pos, cmp_w1, cmp_b1, cmp_w2, cmp_b2, rel_bias,
           s5_lam_re, s5_lam_im, s5_log_step, s5_b_re, s5_b_im, s5_c_re, s5_c_im, s5_d, glu_w, glu_b,
           w_up_attn, w_up_ssm, w_out, peer_w_q, peer_sub_keys, peer_u, peer_v, final_g):
    bsz, l, d = x.shape
    assert ada_w.shape[0] == 1, "single-layer block"
    mod = adaln(c, ada_w[0], ada_b[0])
    sh1, sc1, g1, sh2, sc2, g2 = [m[:, None, :] for m in jnp.split(mod, 6, axis=-1)]

    proj = inproj(x, norm_mix_g[0], sc1, sh1, build_wcat(w_in[0]))

    n16 = l // CMP_STRIDE

    def blocks16(cols):
        return cols.reshape(bsz, l, NSA_KV_HEADS, HEAD_DIM).transpose(0, 2, 1, 3).reshape(
            bsz, NSA_KV_HEADS, n16, CMP_STRIDE * HEAD_DIM)

    cmp_kv = compress(blocks16(proj[:, :, COL_CMP:COL_CMP + KV_WIDTH]),
                      blocks16(proj[:, :, COL_CMP + KV_WIDTH:COL_CMP + 2 * KV_WIDTH]),
                      cmp_w1[0], cmp_pos[0], cmp_b1[0], cmp_w2[0], cmp_b2[0])
    diag, cmpb = attention_tables(rel_bias)
    o_attn = attention(proj, cmp_kv, diag, cmpb)

    tables = s5_tables(s5_lam_re[0], s5_lam_im[0], s5_log_step[0], s5_b_re[0], s5_b_im[0],
                       s5_c_re[0], s5_c_im[0])
    y_ssm = s5_glu(proj, tables, s5_d[0], glu_w[0], glu_b[0])

    x1, h2t = merge(x, o_attn, y_ssm, proj, g1, w_up_attn[0], w_up_ssm[0], w_out[0],
                   norm_ffn_g[0], sc2, sh2)

    t = bsz * l
    scores, stats = peer_scores(h2t, peer_w_q[0], peer_sub_keys[0])
    v_tiles = peer_v[0].reshape(PEER_EXPERTS // PEER_TE, PEER_TE, d).transpose(0, 2, 1).astype(BF16)
    peer_t = peer_experts(h2t, peer_u[0].astype(BF16), v_tiles, scores, stats)
    out = final_norm(x1.reshape(t, d), peer_t, g2, final_g, l)
    return out.reshape(bsz, l, d)
```

```python
import functools
import math

import numpy as np
import jax
import jax.numpy as jnp
from jax import lax
from jax.experimental import pallas as pl
from jax.experimental.pallas import tpu as pltpu

F32 = jnp.float32
BF16 = jnp.bfloat16
HIGHEST = lax.Precision.HIGHEST

D_MODEL = 2048
NSA_HEADS = 16
NSA_KV_HEADS = 4
NSA_GROUP = NSA_HEADS // NSA_KV_HEADS
HEAD_DIM = 64
NSA_WIDTH = NSA_HEADS * HEAD_DIM
KV_WIDTH = NSA_KV_HEADS * HEAD_DIM
CMP_LEN = 32
CMP_STRIDE = 16
CMP_HIDDEN = 256
SEL_BLOCK = 64
SEL_TOP = 16
WINDOW = 512
S5_WIDTH = 1024
S5_GROUP = 16
S5_GROUPS = S5_WIDTH // S5_GROUP
S5_STATE = 64
S5_FLAT = S5_GROUPS * S5_STATE
REL_BUCKETS = 32
REL_MAX_DIST = 128
PEER_KEYS = 128
PEER_EXPERTS = PEER_KEYS * PEER_KEYS
PEER_HEADS = 8
PEER_TOPK = 16
PEER_QDIM = 256
EPS = 1e-6
NEG_INF = -1e30
FORCE_BONUS = 1e4
LOG2E = math.log2(math.e)
MASK_BIG = 1e30

VMEM_LIMIT_BYTES = 56 * 1024 * 1024
LANES = 128
SUBLANES = 8

PROJ_WIDTH = 8192
COL_Q = 0
COL_U = 1024
COL_MERGE = 2048
COL_SEL = 6144
COL_WIN = 6656
COL_CMP = 7168
COL_GATES = 7680

ATT_TILE = 128
BIAS_SPAN = 256
ATT_SUPER = 512
ATT_ORIGIN = 2 * ATT_SUPER - ATT_TILE
ATT_TABLE_ROWS = ATT_ORIGIN + WINDOW + ATT_TILE


def _params(*sem):
    return pltpu.CompilerParams(dimension_semantics=sem, vmem_limit_bytes=VMEM_LIMIT_BYTES)


def _gelu(x):
    return 0.5 * x * (1.0 + jnp.tanh(math.sqrt(2.0 / math.pi) * (x + 0.044715 * (x * x * x))))


def _sigmoid(x):
    return 1.0 / (1.0 + jnp.exp(-x))


def _adaln_kernel(c_ref, w_ref, b_ref, o_ref):
    c = c_ref[...]
    a = c * _sigmoid(c)
    o_ref[...] = jnp.dot(a, w_ref[...], precision=HIGHEST, preferred_element_type=F32) + b_ref[...]


def adaln(c, w, b):
    bsz, d = c.shape
    n = w.shape[1]
    tn = 1024
    cp = jnp.zeros((SUBLANES, d), F32).at[:bsz].set(c)
    out = pl.pallas_call(
        _adaln_kernel,
        grid=(n // tn,),
        in_specs=[pl.BlockSpec((SUBLANES, d), lambda j: (0, 0)),
                  pl.BlockSpec((d, tn), lambda j: (0, j)),
                  pl.BlockSpec((1, tn), lambda j: (0, j))],
        out_specs=pl.BlockSpec((SUBLANES, tn), lambda j: (0, j)),
        out_shape=jax.ShapeDtypeStruct((SUBLANES, n), F32),
        compiler_params=_params("parallel"),
        name="adaln",
    )(cp, w, b.reshape(1, n))
    return out[:bsz]


def _modulated_norm(x, g, sc, sh):
    ms = jnp.mean(x * x, axis=-1, keepdims=True)
    return (x * lax.rsqrt(ms + EPS)) * g * (1.0 + sc) + sh


def _inproj_kernel(x_ref, g_ref, sc_ref, sh_ref, w_ref, o_ref, h_scr):
    @pl.when(pl.program_id(2) == 0)
    def _():
        h_scr[...] = _modulated_norm(x_ref[0], g_ref[...], sc_ref[0], sh_ref[0]).astype(BF16)

    o_ref[0] = jnp.dot(h_scr[...], w_ref[...], preferred_element_type=F32).astype(BF16)


def inproj(x, g, sc, sh, wcat, tm=1024, tn=1024):
    bsz, l, d = x.shape
    n = wcat.shape[1]
    tm = min(tm, l)
    return pl.pallas_call(
        _inproj_kernel,
        grid=(bsz, l // tm, n // tn),
        in_specs=[pl.BlockSpec((1, tm, d), lambda b, i, j: (b, i, 0)),
                  pl.BlockSpec((1, d), lambda b, i, j: (0, 0)),
                  pl.BlockSpec((1, 1, d), lambda b, i, j: (b, 0, 0)),
                  pl.BlockSpec((1, 1, d), lambda b, i, j: (b, 0, 0)),
                  pl.BlockSpec((d, tn), lambda b, i, j: (0, j))],
        out_specs=pl.BlockSpec((1, tm, tn), lambda b, i, j: (b, i, j)),
        out_shape=jax.ShapeDtypeStruct((bsz, l, n), BF16),
        scratch_shapes=[pltpu.VMEM((tm, d), BF16)],
        compiler_params=_params("parallel", "parallel", "arbitrary"),
        name="inproj",
    )(x, g.reshape(1, d), sc, sh, wcat)


def build_wcat(w_in):
    d = w_in.shape[0]
    pts = np.cumsum([NSA_WIDTH] + [KV_WIDTH] * 6 + [3 * NSA_HEADS, S5_WIDTH])
    q, kc, vc, ks, vs, kw, vw, gates, u, merge = jnp.split(w_in, pts, axis=1)

    def pair(k, v):
        return jnp.stack([k.reshape(d, NSA_KV_HEADS, HEAD_DIM), v.reshape(d, NSA_KV_HEADS, HEAD_DIM)],
                         axis=2).reshape(d, 2 * KV_WIDTH)

    gates_p = jnp.pad(gates, ((0, 0), (0, LANES - 3 * NSA_HEADS)))
    pad = jnp.zeros((d, PROJ_WIDTH - COL_GATES - LANES), w_in.dtype)
    wcat = jnp.concatenate([q * (HEAD_DIM ** -0.5 * LOG2E), u, merge, pair(ks, vs), pair(kw, vw),
                            kc, vc, gates_p, pad], axis=1)
    return wcat.astype(BF16)


def _compress_kernel(gk_ref, gv_ref, w1_ref, pos_ref, b1_ref, w2_ref, b2_ref, o_ref):
    half = CMP_STRIDE * HEAD_DIM
    nrow = gk_ref.shape[2]
    out = b2_ref[...]
    for j, g_ref in enumerate((gk_ref, gv_ref)):
        g = g_ref[0, 0]
        w1 = w1_ref[j]
        top = jnp.dot(g, w1[:half].astype(BF16), preferred_element_type=F32)
        bot = jnp.dot(g, w1[half:].astype(BF16), preferred_element_type=F32)
        cst = jnp.dot(pos_ref[j], w1, precision=HIGHEST, preferred_element_type=F32)[0:1] + b1_ref[j]
        hid = top + pltpu.roll(bot, nrow - 1, axis=0) + cst
        out = out + jnp.dot(_gelu(hid).astype(BF16), w2_ref[j].astype(BF16), preferred_element_type=F32)
    o_ref[0, 0] = out.astype(BF16)


def compress(gk, gv, w1, pos, b1, w2, b2):
    bsz, hkv, n16, wd = gk.shape
    posf = jnp.zeros((2, SUBLANES, CMP_LEN * HEAD_DIM), F32).at[:, 0].set(pos.reshape(2, -1))
    w2p = jnp.zeros((2, CMP_HIDDEN, 2 * HEAD_DIM), F32)
    w2p = w2p.at[0, :, :HEAD_DIM].set(w2[0]).at[1, :, HEAD_DIM:].set(w2[1])
    b2p = jnp.concatenate([b2[0], b2[1]]).reshape(1, 2 * HEAD_DIM)
    gspec = pl.BlockSpec((1, 1, n16, wd), lambda b, h: (b, h, 0, 0))
    full = lambda a: pl.BlockSpec(a.shape, lambda b, h: (0,) * a.ndim)
    b1r = b1.reshape(2, 1, CMP_HIDDEN)
    return pl.pallas_call(
        _compress_kernel,
        grid=(bsz, hkv),
        in_specs=[gspec, gspec, full(w1), full(posf), full(b1r), full(w2p), full(b2p)],
        out_specs=pl.BlockSpec((1, 1, n16, 2 * HEAD_DIM), lambda b, h: (b, h, 0, 0)),
        out_shape=jax.ShapeDtypeStruct((bsz, hkv, n16, 2 * HEAD_DIM), BF16),
        compiler_params=_params("parallel", "parallel"),
        name="compress",
    )(gk, gv, w1, posf, b1r, w2p, b2p)


def _t5_bucket_table(n):
    dist = np.arange(n)
    max_exact = REL_BUCKETS // 2
    ratio = np.log(np.maximum(dist, 1).astype(np.float32) / np.float32(max_exact)) / np.float32(
        math.log(REL_MAX_DIST / max_exact))
    large = np.minimum(max_exact + (ratio * np.float32(REL_BUCKETS - max_exact)).astype(np.int32), REL_BUCKETS - 1)
    return np.where(dist < max_exact, dist, large)


def _cmp_to_sel_t(n_cmp, n_sel):
    cs = np.arange(n_cmp)[:, None] * CMP_STRIDE
    ss = np.arange(n_sel)[None, :] * SEL_BLOCK
    ov = np.clip(np.minimum(cs + CMP_LEN, ss + SEL_BLOCK) - np.maximum(cs, ss), 0, None)
    return np.ascontiguousarray((ov / CMP_LEN).T.astype(np.float32))


def attention_tables(rel_bias):
    tq = ATT_TILE
    bucket = _t5_bucket_table(BIAS_SPAN)
    assert np.all(bucket[113:] == REL_BUCKETS - 1)
    bv = (rel_bias[bucket] - rel_bias[REL_BUCKETS - 1][None, :]).T * LOG2E
    heads = bv.shape[0]
    band0 = ATT_ORIGIN - BIAS_SPAN
    band = BIAS_SPAN + tq
    period = band + tq
    vext = jnp.concatenate([jnp.zeros((heads, period - BIAS_SPAN), F32), bv.astype(F32)], axis=1)
    toep = jnp.tile(vext, (1, band))[:, :band * (period - 1)].reshape(heads, band, period - 1)[:, :, :tq]
    tab = jnp.pad(toep, ((0, 0), (band0, ATT_TABLE_ROWS - band0 - band), (0, 0)))
    tab = tab.reshape(NSA_KV_HEADS, NSA_GROUP, ATT_TABLE_ROWS, tq).transpose(0, 2, 1, 3)
    tab = tab.reshape(NSA_KV_HEADS, ATT_TABLE_ROWS, NSA_GROUP * tq)
    lo = 16 * CMP_STRIDE - (8 * CMP_STRIDE - CMP_LEN + 1)
    fpad = jnp.pad(bv.astype(F32), ((0, 0), (lo, tq)))
    cmpb = jnp.stack([fpad[:, lo + (8 * CMP_STRIDE - CMP_LEN + 1) - CMP_STRIDE * r:][:, :tq] for r in range(16)],
                     axis=1)
    cmpb = cmpb.reshape(NSA_KV_HEADS, NSA_GROUP, 16, tq).transpose(0, 2, 1, 3).reshape(NSA_KV_HEADS, 16, NSA_GROUP * tq)
    return tab, cmpb


def _importance_taps(n_cmp, n_sel):
    m = _cmp_to_sel_t(n_cmp, n_sel)
    ratio = SEL_BLOCK // CMP_STRIDE
    cols = np.nonzero(m[1])[0]
    taps = [(int(c - ratio), float(m[1, c])) for c in cols]
    for j in range(n_sel):
        want = np.zeros(n_cmp, np.float32)
        for k, w in taps:
            if 0 <= ratio * j + k < n_cmp:
                want[ratio * j + k] = w
        assert np.array_equal(want, m[j])
    return ratio, taps


def _attn_kernel(q_ref, sel_ref, selt_ref, win_ref, wint_ref, cmp_ref, cmpt_ref, gate_ref, tab_ref, cmpb_ref,
                 ind_ref, kmask_ref, o_ref, m_scr, acc_scr, sel_scr, s_scr, p_scr, *, taps, ratio):
    tq = ATT_TILE
    sk = ATT_SUPER
    cols = NSA_GROUP * tq
    kvh = pl.program_id(1)
    qt = pl.program_id(2)
    n_cmp = cmp_ref.shape[2]
    n_sel = sel_scr.shape[0]
    per = sk // tq
    pad = SUBLANES

    qT = q_ref[0]
    zpad = jnp.zeros((HEAD_DIM, tq), BF16)
    qpad = jnp.concatenate(
        [jnp.concatenate([qT[g * HEAD_DIM:(g + 1) * HEAD_DIM], zpad], axis=0) for g in range(NSA_GROUP)],
        axis=1)
    col_i = lax.broadcasted_iota(jnp.int32, (1, cols), 1) & (tq - 1)
    t_col = qt * tq + col_i

    ckv = cmp_ref[0, 0]
    s_scr[0:pad, :] = jnp.zeros((pad, cols), F32)
    s_scr[pad + n_cmp:, :] = jnp.zeros((pad, cols), F32)
    s_scr[pad:pad + n_cmp, :] = jnp.dot(ckv, qpad, preferred_element_type=F32)
    slot0 = pl.multiple_of(qt * (tq // CMP_STRIDE), SUBLANES)
    s_scr[pl.ds(slot0, 16), :] += cmpb_ref[0]
    s = s_scr[pad:pad + n_cmp, :]
    n_idx = lax.broadcasted_iota(jnp.int32, (n_cmp, cols), 0)
    valid = (n_idx * CMP_STRIDE + (CMP_LEN - 1)) <= t_col
    s = jnp.where(valid, s, NEG_INF)
    mx = jnp.max(s, axis=0, keepdims=True)
    p = jnp.where(valid, jnp.exp2(s - mx), 0.0)
    den = jnp.sum(p, axis=0, keepdims=True)
    p = p * jnp.where(den > 0.0, 1.0 / den, 0.0)
    o_c = jnp.dot(cmpt_ref[0, 0], p.astype(BF16), preferred_element_type=F32)

    p_scr[0:pad, :] = jnp.zeros((pad, tq), F32)
    p_scr[pad:, :] = p[:, 0:tq] + p[:, tq:2 * tq] + p[:, 2 * tq:3 * tq] + p[:, 3 * tq:4 * tq]
    imp = jnp.zeros((n_sel, tq), F32)
    for k, w in taps:
        imp = imp + w * p_scr[pl.ds(pad + k, n_sel, stride=ratio), :]
    blk = lax.broadcasted_iota(jnp.int32, (n_sel, tq), 0)
    qi = lax.broadcasted_iota(jnp.int32, (n_sel, tq), 1)
    cur = jnp.right_shift(qt * tq + qi, 6)

    @pl.when((qt + 1) * tq <= SEL_TOP * SEL_BLOCK)
    def _():
        sel_scr[...] = jnp.where(blk <= cur, 1.0, 0.0)

    @pl.when((qt + 1) * tq > SEL_TOP * SEL_BLOCK)
    def _():
        forced = (blk == 0) | (blk == cur) | (blk == cur - 1)
        val = jnp.where(blk <= cur, imp + FORCE_BONUS * forced.astype(F32), NEG_INF)
        rank = jnp.zeros((n_sel, tq), F32)
        for k in range(n_sel):
            rk = val[k:k + 1, :]
            beats = (rk > val) | ((rk == val) & (blk > k))
            rank = rank + beats.astype(F32)
        sel_scr[...] = jnp.where(rank < float(SEL_TOP), 1.0, 0.0)

    m_scr[...] = jnp.full(m_scr.shape, NEG_INF, F32)
    acc_scr[...] = jnp.zeros(acc_scr.shape, F32)
    st_last = qt // per
    dq = qt - st_last * per
    blocks = sk // SEL_BLOCK
    q_top = qpad[0:HEAD_DIM]
    q_fill = jnp.zeros((HEAD_DIM - 2 * SUBLANES, cols), BF16)

    def sel_scores(st, table_row, causal_shift, live):
        kv = sel_ref[0, pl.ds(pl.multiple_of(st * sk, sk), sk), :]
        kvm = kv * kmask_ref[...] + ind_ref[...]
        picked = sel_scr[pl.ds(pl.multiple_of(st * blocks, blocks), blocks), :]
        if live is not None:
            picked = picked * live
        off = (picked - 1.0) * MASK_BIG
        off = jnp.concatenate([off, jnp.zeros_like(off)], axis=0)
        off = jnp.concatenate([off] * NSA_GROUP, axis=1).astype(BF16)
        sc = jnp.dot(kvm, jnp.concatenate([q_top, off, q_fill], axis=0), preferred_element_type=F32)
        if table_row is not None:
            sc = sc + tab_ref[0, pl.ds(pl.multiple_of(table_row, tq), sk), :]
        if causal_shift is not None:
            rel = col_i - lax.broadcasted_iota(jnp.int32, (sk, cols), 0)
            sc = jnp.where(rel + causal_shift >= 0, sc, NEG_INF)
        return sc

    def sel_tiles(*tiles):
        scs = [sel_scores(*t) for t in tiles]
        m_old = m_scr[...]
        m_new = m_old
        for sc in scs:
            m_new = jnp.maximum(m_new, jnp.max(sc, axis=0, keepdims=True))
        acc = jnp.exp2(m_old - m_new) * acc_scr[...]
        for (st, _, _, _), sc in zip(tiles, scs):
            pe = jnp.exp2(sc - m_new).astype(BF16)
            acc = acc + jnp.dot(selt_ref[0, 0, st], pe, preferred_element_type=F32)
        acc_scr[...] = acc
        m_scr[...] = m_new

    n_far = jnp.maximum(st_last - 1, 0)

    def pair_body(pr, carry):
        sel_tiles((2 * pr, None, None, None), (2 * pr + 1, None, None, None))
        return carry

    lax.fori_loop(0, n_far // 2, pair_body, 0)

    @pl.when(n_far % 2 == 1)
    def _():
        sel_tiles((n_far - 1, None, None, None))

    has_prev = jnp.where(st_last >= 1, 1.0, 0.0)
    sel_tiles((jnp.maximum(st_last - 1, 0), ATT_ORIGIN - sk - dq * tq, None, has_prev),
              (st_last, ATT_ORIGIN - dq * tq, dq * tq, None))
    o_s = acc_scr[HEAD_DIM:, :] * (1.0 / acc_scr[0:1, :])

    wk = WINDOW + tq
    n_w = wk // tq
    k0 = jnp.maximum(qt - WINDOW // tq, 0)
    dw = (qt - k0) * tq
    kv = win_ref[0, pl.ds(pl.multiple_of(k0 * tq, tq), wk), :]
    sc = jnp.dot(kv, qpad, preferred_element_type=F32)
    sc = sc + tab_ref[0, pl.ds(pl.multiple_of(ATT_ORIGIN - dw, tq), wk), :]
    dist = col_i - lax.broadcasted_iota(jnp.int32, (wk, cols), 0) + dw
    sc = jnp.where((dist >= 0) & (dist < WINDOW), sc, NEG_INF)
    mw = jnp.max(sc, axis=0, keepdims=True)
    pw = jnp.exp2(sc - mw).astype(BF16)
    ow = jnp.dot(wint_ref[0, 0, k0], pw[0:tq], preferred_element_type=F32)
    for m in range(1, n_w):
        ow = ow + jnp.dot(wint_ref[0, 0, k0 + m], pw[m * tq:(m + 1) * tq], preferred_element_type=F32)
    o_w = ow[HEAD_DIM:, :] * (1.0 / ow[0:1, :])

    branches = (o_c[HEAD_DIM:, :], o_s, o_w)
    outs = []
    for g in range(NSA_GROUP):
        acc = jnp.zeros((HEAD_DIM, tq), F32)
        for br in range(3):
            grow = gate_ref[0, pl.ds((kvh * NSA_GROUP + g) * 3 + br, 1), :]
            acc = acc + _sigmoid(grow) * branches[br][:, g * tq:(g + 1) * tq]
        outs.append(acc)
    for pr in range(NSA_GROUP // 2):
        pair = jnp.concatenate([outs[2 * pr], outs[2 * pr + 1]], axis=0)
        o_ref[0, :, pr * LANES:(pr + 1) * LANES] = pair.T.astype(BF16)


def attention(proj, cmp_kv, table, cmpb_t):
    bsz, l, _ = proj.shape
    tq = ATT_TILE
    sk = ATT_SUPER
    assert l % sk == 0 and l >= WINDOW + tq
    n_kt = l // tq
    n_st = l // sk
    n_cmp = cmp_kv.shape[2]
    n_sel = l // SEL_BLOCK
    ratio, taps = _importance_taps(n_cmp, n_sel)
    gw = NSA_GROUP * HEAD_DIM
    kvw = 2 * KV_WIDTH
    q_t = proj[:, :, COL_Q:COL_Q + NSA_WIDTH].transpose(0, 2, 1)

    def tiles_t(c0, nt):
        t = proj[:, :, c0:c0 + kvw].reshape(bsz, nt, l // nt, NSA_KV_HEADS, LANES).transpose(0, 3, 1, 4, 2)
        first = lax.broadcasted_iota(jnp.int32, t.shape, 3) == 0
        return jnp.where(first, jnp.ones((), BF16), t)

    sel_t = tiles_t(COL_SEL, n_st)
    win_t = tiles_t(COL_WIN, n_kt)
    cmp_t = cmp_kv.transpose(0, 1, 3, 2)
    gates_t = proj[:, :, COL_GATES:COL_GATES + LANES].astype(F32).transpose(0, 2, 1)
    key = np.arange(sk)[:, None]
    lane = np.arange(LANES)[None, :]
    ind = jnp.asarray((lane == HEAD_DIM + key // SEL_BLOCK).astype(np.float32), BF16)
    kmask = jnp.asarray(np.broadcast_to(lane < HEAD_DIM, (sk, LANES)).astype(np.float32), BF16)
    const = lambda a: pl.BlockSpec(a.shape, lambda b, h, i: (0,) * a.ndim)
    return pl.pallas_call(
        functools.partial(_attn_kernel, taps=taps, ratio=ratio),
        grid=(bsz, NSA_KV_HEADS, n_kt),
        in_specs=[pl.BlockSpec((1, gw, tq), lambda b, h, i: (b, h, i)),
                  pl.BlockSpec((1, l, LANES), lambda b, h, i: (b, 0, COL_SEL // LANES + h)),
                  pl.BlockSpec((1, 1, n_st, LANES, sk), lambda b, h, i: (b, h, 0, 0, 0)),
                  pl.BlockSpec((1, l, LANES), lambda b, h, i: (b, 0, COL_WIN // LANES + h)),
                  pl.BlockSpec((1, 1, n_kt, LANES, tq), lambda b, h, i: (b, h, 0, 0, 0)),
                  pl.BlockSpec((1, 1, n_cmp, LANES), lambda b, h, i: (b, h, 0, 0)),
                  pl.BlockSpec((1, 1, LANES, n_cmp), lambda b, h, i: (b, h, 0, 0)),
                  pl.BlockSpec((1, LANES, tq), lambda b, h, i: (b, 0, i)),
                  pl.BlockSpec((1, ATT_TABLE_ROWS, NSA_GROUP * tq), lambda b, h, i: (h, 0, 0)),
                  pl.BlockSpec((1, 16, NSA_GROUP * tq), lambda b, h, i: (h, 0, 0)),
                  const(ind), const(kmask)],
        out_specs=pl.BlockSpec((1, tq, gw), lambda b, h, i: (b, i, h)),
        out_shape=jax.ShapeDtypeStruct((bsz, l, NSA_WIDTH), BF16),
        scratch_shapes=[pltpu.VMEM((1, NSA_GROUP * tq), F32),
                        pltpu.VMEM((LANES, NSA_GROUP * tq), F32),
                        pltpu.VMEM((n_sel, tq), F32),
                        pltpu.VMEM((n_cmp + 2 * SUBLANES, NSA_GROUP * tq), F32),
                        pltpu.VMEM((n_cmp + SUBLANES, tq), F32)],
        compiler_params=_params("parallel", "parallel", "arbitrary"),
        name="nsa_attention",
    )(q_t, proj, sel_t, proj, win_t, cmp_kv, cmp_t, gates_t, table, cmpb_t, ind, kmask)


S5_BLOCK = 256
S5_CHUNK = S5_BLOCK // SUBLANES
S5_SLAB = 1024


def _s5_kernel(u_ref, perm_ref, permt_ref, bre_ref, bim_ref, are_ref, aim_ref, ace_ref, aci_ref, cre_ref, cim_ref,
               d_ref, gw_ref, gb_ref, o_ref, bu_re, bu_im, xs_re, xs_im, carry_re, carry_im, init_re, init_im):
    nr = S5_GROUPS // 16
    rw = 16 * S5_GROUP
    sw = 16 * S5_STATE

    @pl.when(pl.program_id(1) == 0)
    def _():
        carry_re[...] = jnp.zeros(carry_re.shape, F32)
        carry_im[...] = jnp.zeros(carry_im.shape, F32)

    perm = perm_ref[...]
    up = jnp.dot(perm, u_ref[0], preferred_element_type=F32).astype(BF16)
    for r in range(nr):
        ur = up[:, r * rw:(r + 1) * rw]
        bu_re[:, r * sw:(r + 1) * sw] = jnp.dot(ur, bre_ref[r], preferred_element_type=F32)
        bu_im[:, r * sw:(r + 1) * sw] = jnp.dot(ur, bim_ref[r], preferred_element_type=F32)

    nslab = S5_FLAT // S5_SLAB
    row8 = lax.broadcasted_iota(jnp.int32, (SUBLANES, S5_SLAB), 0)
    for sl in range(nslab):
        cs = slice(sl * S5_SLAB, (sl + 1) * S5_SLAB)
        a_re = jnp.broadcast_to(are_ref[:, cs], (SUBLANES, S5_SLAB))
        a_im = jnp.broadcast_to(aim_ref[:, cs], (SUBLANES, S5_SLAB))

        def step1(s, st):
            s_re, s_im = st
            off = pl.multiple_of(s * SUBLANES, SUBLANES)
            n_re = a_re * s_re - a_im * s_im + bu_re[pl.ds(off, SUBLANES), cs]
            n_im = a_re * s_im + a_im * s_re + bu_im[pl.ds(off, SUBLANES), cs]
            return n_re, n_im

        zero = jnp.zeros((SUBLANES, S5_SLAB), F32)
        e_re, e_im = lax.fori_loop(0, S5_CHUNK, step1, (zero, zero))

        ac_re = ace_ref[:, cs]
        ac_im = aci_ref[:, cs]
        x_re = carry_re[:, cs]
        x_im = carry_im[:, cs]
        st_re = jnp.zeros((SUBLANES, S5_SLAB), F32)
        st_im = jnp.zeros((SUBLANES, S5_SLAB), F32)
        for c in range(SUBLANES):
            st_re = jnp.where(row8 == c, x_re, st_re)
            st_im = jnp.where(row8 == c, x_im, st_im)
            n_re = ac_re * x_re - ac_im * x_im + e_re[c:c + 1]
            n_im = ac_re * x_im + ac_im * x_re + e_im[c:c + 1]
            x_re, x_im = n_re, n_im
        carry_re[:, cs] = x_re
        carry_im[:, cs] = x_im
        init_re[:, cs] = st_re
        init_im[:, cs] = st_im

        def step2(s, st):
            s_re, s_im = st
            off = pl.multiple_of(s * SUBLANES, SUBLANES)
            n_re = a_re * s_re - a_im * s_im + bu_re[pl.ds(off, SUBLANES), cs]
            n_im = a_re * s_im + a_im * s_re + bu_im[pl.ds(off, SUBLANES), cs]
            xs_re[pl.ds(off, SUBLANES), cs] = n_re
            xs_im[pl.ds(off, SUBLANES), cs] = n_im
            return n_re, n_im

        lax.fori_loop(0, S5_CHUNK, step2, (st_re, st_im))

    ys = []
    for r in range(nr):
        xr = xs_re[:, r * sw:(r + 1) * sw].astype(BF16)
        xi = xs_im[:, r * sw:(r + 1) * sw].astype(BF16)
        ys.append(jnp.dot(xr, cre_ref[r], preferred_element_type=F32)
                  - jnp.dot(xi, cim_ref[r], preferred_element_type=F32))
    y = jnp.concatenate(ys, axis=1) + d_ref[...] * up.astype(F32)
    y = _gelu(y)
    z = jnp.dot(y.astype(BF16), gw_ref[...], preferred_element_type=F32) + gb_ref[...]
    y = (y * _sigmoid(z)).astype(BF16)
    o_ref[0] = jnp.dot(permt_ref[...], y, preferred_element_type=F32).astype(BF16)


def s5_tables(lam_re, lam_im, log_step, b_re, b_im, c_re, c_im):
    f32 = F32
    lam_re, lam_im = lam_re.astype(f32), lam_im.astype(f32)
    step = jnp.exp(log_step.astype(f32))[:, None]
    mag = jnp.exp(lam_re * step)
    lb_re, lb_im = mag * jnp.cos(lam_im * step), mag * jnp.sin(lam_im * step)
    den = lam_re * lam_re + lam_im * lam_im
    n_re = lb_re - 1.0
    f_re = (n_re * lam_re + lb_im * lam_im) / den
    f_im = (lb_im * lam_re - n_re * lam_im) / den
    b_re, b_im = b_re.astype(f32), b_im.astype(f32)
    bb_re = f_re[..., None] * b_re - f_im[..., None] * b_im
    bb_im = f_re[..., None] * b_im + f_im[..., None] * b_re
    magc = jnp.exp(lam_re * step * S5_CHUNK)
    ac_re, ac_im = magc * jnp.cos(lam_im * step * S5_CHUNK), magc * jnp.sin(lam_im * step * S5_CHUNK)

    nr = S5_GROUPS // 16
    eye = jnp.eye(16, dtype=f32)

    def b_blocks(bb):
        bb = bb.reshape(nr, 16, S5_STATE, S5_GROUP)
        return jnp.einsum('rgph,gk->rghkp', bb, eye).reshape(nr, 16 * S5_GROUP, 16 * S5_STATE).astype(BF16)

    def c_blocks(cc):
        cc = cc.astype(f32).reshape(nr, 16, S5_GROUP, S5_STATE)
        return jnp.einsum('rghp,gk->rgpkh', cc, eye).reshape(nr, 16 * S5_STATE, 16 * S5_GROUP).astype(BF16)

    flat = lambda a: a.reshape(1, S5_FLAT)
    return (b_blocks(bb_re), b_blocks(bb_im), flat(lb_re), flat(lb_im), flat(ac_re), flat(ac_im),
            c_blocks(c_re), c_blocks(c_im))


def _s5_perm():
    p = np.zeros((S5_BLOCK, S5_BLOCK), np.float32)
    for c in range(SUBLANES):
        for s in range(S5_CHUNK):
            p[s * SUBLANES + c, c * S5_CHUNK + s] = 1.0
    return p


def s5_glu(proj, tables, d_skip, glu_w, glu_b):
    bsz, l, _ = proj.shape
    tb = S5_BLOCK
    bre, bim, are, aim, ace, aci, cre, cim = tables
    perm = jnp.asarray(_s5_perm(), BF16)
    permt = jnp.asarray(_s5_perm().T, BF16)
    full = lambda a: pl.BlockSpec(a.shape, lambda b, i: (0,) * a.ndim)
    d2 = d_skip.reshape(1, S5_WIDTH).astype(F32)
    gb2 = glu_b.reshape(1, S5_WIDTH).astype(F32)
    gw = glu_w.astype(BF16)
    return pl.pallas_call(
        _s5_kernel,
        grid=(bsz, l // tb),
        in_specs=[pl.BlockSpec((1, tb, S5_WIDTH), lambda b, i: (b, i, COL_U // S5_WIDTH)),
                  full(perm), full(permt), full(bre), full(bim), full(are), full(aim), full(ace), full(aci),
                  full(cre), full(cim), full(d2), full(gw), full(gb2)],
        out_specs=pl.BlockSpec((1, tb, S5_WIDTH), lambda b, i: (b, i, 0)),
        out_shape=jax.ShapeDtypeStruct((bsz, l, S5_WIDTH), BF16),
        scratch_shapes=[pltpu.VMEM((tb, S5_FLAT), F32), pltpu.VMEM((tb, S5_FLAT), F32),
                        pltpu.VMEM((tb, S5_FLAT), F32), pltpu.VMEM((tb, S5_FLAT), F32),
                        pltpu.VMEM((1, S5_FLAT), F32), pltpu.VMEM((1, S5_FLAT), F32),
                        pltpu.VMEM((SUBLANES, S5_FLAT), F32), pltpu.VMEM((SUBLANES, S5_FLAT), F32)],
        compiler_params=_params("parallel", "arbitrary"),
        name="s5_glu",
    )(proj, perm, permt, bre, bim, are, aim, ace, aci, cre, cim, d2, gw, gb2)


def _merge_kernel(x_ref, oa_ref, ys_ref, ga_ref, gb_ref, g1_ref, wa_ref, ws_ref, wo_ref,
                  ng_ref, sc_ref, sh_ref, x1_ref, h2_ref):
    a = jnp.dot(oa_ref[0], wa_ref[...], preferred_element_type=F32)
    s = jnp.dot(ys_ref[0], ws_ref[...], preferred_element_type=F32)
    mixed = _sigmoid(ga_ref[0].astype(F32)) * a + _sigmoid(gb_ref[0].astype(F32)) * s
    out = jnp.dot(mixed.astype(BF16), wo_ref[...], preferred_element_type=F32)
    x1 = x_ref[0] + g1_ref[0] * out
    x1_ref[0] = x1
    h2_ref[...] = _modulated_norm(x1, ng_ref[...], sc_ref[0], sh_ref[0]).T.astype(BF16)


def merge(x, o_attn, y_ssm, proj, g1, w_up_attn, w_up_ssm, w_out, norm_g, sc2, sh2, tm=256):
    bsz, l, d = x.shape
    tok = lambda w, cb=0: pl.BlockSpec((1, tm, w), lambda b, i: (b, i, cb))
    vec = pl.BlockSpec((1, 1, d), lambda b, i: (b, 0, 0))
    full = lambda a: pl.BlockSpec(a.shape, lambda b, i: (0,) * a.ndim)
    wa, ws, wo = w_up_attn.astype(BF16), w_up_ssm.astype(BF16), w_out.astype(BF16)
    ng = norm_g.reshape(1, d)
    return pl.pallas_call(
        _merge_kernel,
        grid=(bsz, l // tm),
        in_specs=[tok(d), tok(NSA_WIDTH), tok(S5_WIDTH), tok(d, COL_MERGE // d), tok(d, COL_MERGE // d + 1),
                  vec, full(wa), full(ws), full(wo), full(ng), vec, vec],
        out_specs=[tok(d), pl.BlockSpec((d, tm), lambda b, i: (0, b * (l // tm) + i))],
        out_shape=[jax.ShapeDtypeStruct((bsz, l, d), F32), jax.ShapeDtypeStruct((d, bsz * l), BF16)],
        compiler_params=_params("parallel", "parallel"),
        name="merge",
    )(x, o_attn, y_ssm, proj, proj, g1, wa, ws, wo, ng, sc2, sh2)


PEER_SCORE_TM = SUBLANES * LANES
PEER_RANKS = PEER_TOPK + 1
_PAIR_COUNTS = [PEER_RANKS // (a + 1) for a in range(PEER_RANKS)]
_N_PAIRS = sum(_PAIR_COUNTS)


def _rows_to_vregs(x):
    v = [x[:, b * LANES:(b + 1) * LANES] for b in range(SUBLANES)]
    sub = lax.broadcasted_iota(jnp.int32, (SUBLANES, LANES), 0)
    for dist in (4, 2, 1):
        keep = (sub & dist) == 0
        nv = list(v)
        for i in range(SUBLANES):
            if i & dist:
                continue
            lo, hi = v[i], v[i + dist]
            nv[i] = jnp.where(keep, lo, pltpu.roll(hi, dist, axis=0))
            nv[i + dist] = jnp.where(keep, pltpu.roll(lo, SUBLANES - dist, axis=0), hi)
        v = nv
    return v


def _tree_max(x):
    while x.shape[0] > 1:
        n = x.shape[0]
        half = n // 2
        top = jnp.maximum(x[:half], x[half:2 * half])
        x = top if n % 2 == 0 else jnp.concatenate([top, x[2 * half:]], axis=0)
    return x[0]


def _peer_score_kernel(ht_ref, wqt_ref, keys_ref, s_ref, st_ref, q_scr, key_scr, top_scr, cand_scr, best_scr):
    half = PEER_QDIM // 2
    nk = PEER_KEYS
    chunk = 2 * SUBLANES

    def drain(ref, out_ref, n_out):
        n = ref.shape[0]

        def body(r, prev):
            acc = jnp.full((chunk, SUBLANES, LANES), -jnp.inf, F32)
            for c0 in range(0, n, chunk):
                part = ref[c0:min(c0 + chunk, n)]
                part = jnp.where(part < prev[None], part, -jnp.inf)
                if part.shape[0] < chunk:
                    part = jnp.concatenate(
                        [part, jnp.full((chunk - part.shape[0], SUBLANES, LANES), -jnp.inf, F32)], axis=0)
                acc = jnp.maximum(acc, part)
            m = _tree_max(acc)
            out_ref[r] = m
            return m

        lax.fori_loop(0, n_out, body, jnp.full((SUBLANES, LANES), jnp.inf, F32))

    q_scr[...] = jnp.dot(wqt_ref[...], ht_ref[...], preferred_element_type=F32)

    def head(h, carry):
        for c in range(2):
            hc = 2 * h + c
            qt = q_scr[pl.ds(pl.multiple_of(hc * half, half), half), :]
            sc = jnp.dot(keys_ref[h, c], qt, precision=HIGHEST, preferred_element_type=F32)
            s_ref[hc] = sc
            for kg in range(nk // SUBLANES):
                rows = _rows_to_vregs(sc[kg * SUBLANES:(kg + 1) * SUBLANES, :])
                for k in range(SUBLANES):
                    key_scr[c, kg * SUBLANES + k] = rows[k]
            drain(key_scr.at[c], top_scr.at[c], PEER_RANKS)
        i = 0
        for a in range(PEER_RANKS):
            for bb in range(_PAIR_COUNTS[a]):
                cand_scr[i] = top_scr[0, a] + top_scr[1, bb]
                i += 1
        drain(cand_scr, best_scr, PEER_RANKS)
        theta = 0.5 * (best_scr[PEER_TOPK - 1] + best_scr[PEER_TOPK])
        z = jnp.zeros((SUBLANES, LANES), F32)
        for r in range(PEER_TOPK):
            z = z + jnp.exp(best_scr[r] - best_scr[0])
        for row, val in enumerate((theta, top_scr[0, 0], top_scr[1, 0], 1.0 / z)):
            for blk in range(SUBLANES):
                st_ref[h, row:row + 1, blk * LANES:(blk + 1) * LANES] = val[blk:blk + 1, :]
        st_ref[h, 4:, :] = jnp.zeros((SUBLANES - 4, st_ref.shape[2]), F32)
        return carry

    lax.fori_loop(0, PEER_HEADS, head, 0)


def peer_scores(h2t, w_q, sub_keys, tm=PEER_SCORE_TM):
    d, t = h2t.shape
    wqt = w_q.T.astype(BF16)
    nk = PEER_KEYS
    return pl.pallas_call(
        _peer_score_kernel,
        grid=(t // tm,),
        in_specs=[pl.BlockSpec((d, tm), lambda i: (0, i)),
                  pl.BlockSpec(wqt.shape, lambda i: (0, 0)),
                  pl.BlockSpec(sub_keys.shape, lambda i: (0, 0, 0, 0))],
        out_specs=[pl.BlockSpec((2 * PEER_HEADS, nk, tm), lambda i: (0, 0, i)),
                   pl.BlockSpec((PEER_HEADS, SUBLANES, tm), lambda i: (0, 0, i))],
        out_shape=[jax.ShapeDtypeStruct((2 * PEER_HEADS, nk, t), F32),
                   jax.ShapeDtypeStruct((PEER_HEADS, SUBLANES, t), F32)],
        scratch_shapes=[pltpu.VMEM((wqt.shape[0], tm), F32),
                        pltpu.VMEM((2, nk, SUBLANES, LANES), F32),
                        pltpu.VMEM((2, PEER_RANKS, SUBLANES, LANES), F32),
                        pltpu.VMEM((_N_PAIRS, SUBLANES, LANES), F32),
                        pltpu.VMEM((PEER_RANKS, SUBLANES, LANES), F32)],
        compiler_params=_params("parallel"),
        name="peer_scores",
    )(h2t, wqt, sub_keys)


PEER_TM = 512
PEER_TE = 1024
PEER_MXU_TILE = 256


def _peer_kernel(ht_ref, u_ref, vt_ref, s_ref, st_ref, o_ref, e_scr, thr_scr, e1_scr, at_a, at_b, gw_a, gw_b,
                 *, n_static):
    j = pl.program_id(1)
    nk = PEER_KEYS
    d, tm = ht_ref.shape
    te = u_ref.shape[0]
    rpt = te // nk
    mx = PEER_MXU_TILE

    @pl.when(j == 0)
    def _():
        o_ref[...] = jnp.zeros(o_ref.shape, F32)
        for h in range(PEER_HEADS):
            st = st_ref[h]
            e_scr[2 * h] = jnp.exp(s_ref[2 * h] - st[1:2]) * st[3:4]
            e_scr[2 * h + 1] = jnp.exp(s_ref[2 * h + 1] - st[2:3])

    def step(at_w, at_r, gw_w, gw_r, do1, do2, do3):
        def stage1(k):
            r0, c0 = (k // (tm // mx)) * mx, (k % (tm // mx)) * mx
            at_w[r0:r0 + mx, c0:c0 + mx] = jnp.dot(u_ref[r0:r0 + mx, :], ht_ref[:, c0:c0 + mx],
                                                   preferred_element_type=F32)

        def stage3(k):
            r0, c0 = (k // (tm // mx)) * mx, (k % (tm // mx)) * mx
            o_ref[r0:r0 + mx, c0:c0 + mx] += jnp.dot(vt_ref[0, r0:r0 + mx, :], gw_r[:, c0:c0 + mx],
                                                     preferred_element_type=F32)

        n1 = (te // mx) * (tm // mx) if do1 else 0
        n3 = (d // mx) * (tm // mx) if do3 else 0
        if not do2:
            for k in range(n1):
                stage1(k)
            for k in range(n3):
                stage3(k)
            return

        for h in range(PEER_HEADS):
            for rr in range(rpt):
                row = h * rpt + rr
                r = (j - 1) * rpt + rr
                thr_scr[row:row + 1, :] = st_ref[h, 0:1, :] - s_ref[2 * h, pl.ds(r, 1), :]
                e1_scr[row:row + 1, :] = e_scr[2 * h, pl.ds(r, 1), :]

        kh = nk // 4
        grp = 4
        n2 = (tm // LANES) * (nk // kh) * (rpt // grp)
        done1 = done3 = 0
        blk = 0
        for ch in range(tm // LANES):
            cs = slice(ch * LANES, (ch + 1) * LANES)
            for hf in range(nk // kh):
                ks = slice(hf * kh, (hf + 1) * kh)
                for g0 in range(0, rpt, grp):
                    ws = [jnp.zeros((kh, LANES), F32) for _ in range(grp)]
                    for h in range(PEER_HEADS):
                        s2c = s_ref[2 * h + 1, ks, cs]
                        e2c = e_scr[2 * h + 1, ks, cs]
                        for q in range(grp):
                            row = h * rpt + g0 + q
                            sel = jnp.where(s2c >= thr_scr[row:row + 1, cs], e2c, 0.0)
                            ws[q] = ws[q] + sel * e1_scr[row:row + 1, cs]
                    for q in range(grp):
                        r0 = (g0 + q) * nk + hf * kh
                        a = at_r[r0:r0 + kh, cs]
                        gw_w[r0:r0 + kh, cs] = _gelu(a.astype(BF16)) * ws[q].astype(BF16)
                    blk += 1
                    while done1 * n2 < blk * n1:
                        stage1(done1)
                        done1 += 1
                    while done3 * n2 < blk * n3:
                        stage3(done3)
                        done3 += 1

    bufs = ((at_a, at_b, gw_a, gw_b), (at_b, at_a, gw_b, gw_a))
    steady = (j >= 2) & (j < n_static)
    for jj, flags in ((0, (True, False, False)), (1, (True, True, False)),
                      (n_static, (False, True, True)), (n_static + 1, (False, False, True))):
        pl.when(j == jj)(functools.partial(step, *bufs[jj % 2], *flags))
    for par in range(2):
        pl.when(steady & (j % 2 == par))(functools.partial(step, *bufs[par], True, True, True))


def peer_experts(h2t, u_bf, vt_bf, scores, stats, tm=PEER_TM, te=PEER_TE):
    d, t = h2t.shape
    n_tiles = vt_bf.shape[0]
    return pl.pallas_call(
        functools.partial(_peer_kernel, n_static=n_tiles),
        grid=(t // tm, n_tiles + 2),
        in_specs=[pl.BlockSpec((d, tm), lambda i, j: (0, i)),
                  pl.BlockSpec((te, d), lambda i, j: (jnp.minimum(j, n_tiles - 1), 0)),
                  pl.BlockSpec((1, d, te), lambda i, j: (jnp.clip(j - 2, 0, n_tiles - 1), 0, 0)),
                  pl.BlockSpec((2 * PEER_HEADS, PEER_KEYS, tm), lambda i, j: (0, 0, i)),
                  pl.BlockSpec((PEER_HEADS, SUBLANES, tm), lambda i, j: (0, 0, i))],
        out_specs=pl.BlockSpec((d, tm), lambda i, j: (0, i)),
        out_shape=jax.ShapeDtypeStruct((d, t), F32),
        scratch_shapes=[pltpu.VMEM((2 * PEER_HEADS, PEER_KEYS, tm), F32),
                        pltpu.VMEM((PEER_HEADS * te // PEER_KEYS, tm), F32),
                        pltpu.VMEM((PEER_HEADS * te // PEER_KEYS, tm), F32),
                        pltpu.VMEM((te, tm), F32), pltpu.VMEM((te, tm), F32),
                        pltpu.VMEM((te, tm), BF16), pltpu.VMEM((te, tm), BF16)],
        compiler_params=_params("parallel", "arbitrary"),
        name="peer_experts",
    )(h2t, u_bf, vt_bf, scores, stats)


def _final_kernel(x1_ref, pt_ref, g2_ref, fg_ref, o_ref):
    x2 = x1_ref[...] + g2_ref[0] * pt_ref[...].T
    ms = jnp.mean(x2 * x2, axis=-1, keepdims=True)
    o_ref[...] = (x2 * lax.rsqrt(ms + EPS)) * fg_ref[...]


def final_norm(x1, peer_t, g2, final_g, l, tm=512):
    t, d = x1.shape
    tpb = l // tm
    return pl.pallas_call(
        _final_kernel,
        grid=(t // tm,),
        in_specs=[pl.BlockSpec((tm, d), lambda i: (i, 0)),
                  pl.BlockSpec((d, tm), lambda i: (0, i)),
                  pl.BlockSpec((1, 1, d), lambda i: (i // tpb, 0, 0)),
                  pl.BlockSpec((1, d), lambda i: (0, 0))],
        out_specs=pl.BlockSpec((tm, d), lambda i: (i, 0)),
        out_shape=jax.ShapeDtypeStruct((t, d), F32),
        compiler_params=_params("parallel"),
        name="final_norm",
    )(x1, peer_t, g2, final_g.reshape(1, d))


def kernel(x, c, ada_w, ada_b, norm_mix_g, norm_ffn_g, w_in, cmp_pos, cmp_w1, cmp_b1, cmp_w2, cmp_b2, rel_bias,
           s5_lam_re, s5_lam_im, s5_log_step, s5_b_re, s5_b_im, s5_c_re, s5_c_im, s5_d, glu_w, glu_b,
           w_up_attn, w_up_ssm, w_out, peer_w_q, peer_sub_keys, peer_u, peer_v, final_g):
    bsz, l, d = x.shape
    assert ada_w.shape[0] == 1, "single-layer block"
    mod = adaln(c, ada_w[0], ada_b[0])
    sh1, sc1, g1, sh2, sc2, g2 = [m[:, None, :] for m in jnp.split(mod, 6, axis=-1)]

    proj = inproj(x, norm_mix_g[0], sc1, sh1, build_wcat(w_in[0]))

    n16 = l // CMP_STRIDE

    def blocks16(cols):
        return cols.reshape(bsz, l, NSA_KV_HEADS, HEAD_DIM).transpose(0, 2, 1, 3).reshape(
            bsz, NSA_KV_HEADS, n16, CMP_STRIDE * HEAD_DIM)

    cmp_kv = compress(blocks16(proj[:, :, COL_CMP:COL_CMP + KV_WIDTH]),
                      blocks16(proj[:, :, COL_CMP + KV_WIDTH:COL_CMP + 2 * KV_WIDTH]),
                      cmp_w1[0], cmp_pos[0], cmp_b1[0], cmp_w2[0], cmp_b2[0])
    diag, cmpb = attention_tables(rel_bias)
    o_attn = attention(proj, cmp_kv, diag, cmpb)

    tables = s5_tables(s5_lam_re[0], s5_lam_im[0], s5_log_step[0], s5_b_re[0], s5_b_im[0],
                       s5_c_re[0], s5_c_im[0])
    y_ssm = s5_glu(proj, tables, s5_d[0], glu_w[0], glu_b[0])

    x1, h2t = merge(x, o_attn, y_ssm, proj, g1, w_up_attn[0], w_up_ssm[0], w_out[0],
                   norm_ffn_g[0], sc2, sh2)

    t = bsz * l
    scores, stats = peer_scores(h2t, peer_w_q[0], peer_sub_keys[0])
    v_tiles = peer_v[0].reshape(PEER_EXPERTS // PEER_TE, PEER_TE, d).transpose(0, 2, 1).astype(BF16)
    peer_t = peer_experts(h2t, peer_u[0].astype(BF16), v_tiles, scores, stats)
    out = final_norm(x1.reshape(t, d), peer_t, g2, final_g, l)
    return out.reshape(bsz, l, d)
```

```python
import functools
import math

import numpy as np
import jax
import jax.numpy as jnp
from jax import lax
from jax.experimental import pallas as pl
from jax.experimental.pallas import tpu as pltpu

F32 = jnp.float32
BF16 = jnp.bfloat16
HIGHEST = lax.Precision.HIGHEST

D_MODEL = 2048
NSA_HEADS = 16
NSA_KV_HEADS = 4
NSA_GROUP = NSA_HEADS // NSA_KV_HEADS
HEAD_DIM = 64
NSA_WIDTH = NSA_HEADS * HEAD_DIM
KV_WIDTH = NSA_KV_HEADS * HEAD_DIM
CMP_LEN = 32
CMP_STRIDE = 16
CMP_HIDDEN = 256
SEL_BLOCK = 64
SEL_TOP = 16
WINDOW = 512
S5_WIDTH = 1024
S5_GROUP = 16
S5_GROUPS = S5_WIDTH // S5_GROUP
S5_STATE = 64
S5_FLAT = S5_GROUPS * S5_STATE
REL_BUCKETS = 32
REL_MAX_DIST = 128
PEER_KEYS = 128
PEER_EXPERTS = PEER_KEYS * PEER_KEYS
PEER_HEADS = 8
PEER_TOPK = 16
PEER_QDIM = 256
EPS = 1e-6
NEG_INF = -1e30
FORCE_BONUS = 1e4
LOG2E = math.log2(math.e)
MASK_BIG = 1e30

VMEM_LIMIT_BYTES = 56 * 1024 * 1024
LANES = 128
SUBLANES = 8
MXU_DIM = 256

PROJ_WIDTH = 8192
COL_Q = 0
COL_U = 1024
COL_MERGE = 2048
COL_SEL = 6144
COL_WIN = 6656
COL_CMP = 7168
COL_GATES = 7680

ATT_TILE = 128
BIAS_SPAN = 256
CMP_BIAS_SLOTS = 16
SEL_SHIFT = SEL_BLOCK.bit_length() - 1
ATT_SUPER = 512
ATT_ORIGIN = 2 * ATT_SUPER - ATT_TILE
ATT_TABLE_ROWS = ATT_ORIGIN + WINDOW + ATT_TILE


def _params(*sem):
    return pltpu.CompilerParams(dimension_semantics=sem, vmem_limit_bytes=VMEM_LIMIT_BYTES)


def _gelu(x):
    return 0.5 * x * (1.0 + jnp.tanh(math.sqrt(2.0 / math.pi) * (x + 0.044715 * (x * x * x))))


def _sigmoid(x):
    return 1.0 / (1.0 + jnp.exp(-x))


def _adaln_kernel(c_ref, w_ref, b_ref, o_ref):
    c = c_ref[...]
    a = c * _sigmoid(c)
    o_ref[...] = jnp.dot(a, w_ref[...], precision=HIGHEST, preferred_element_type=F32) + b_ref[...]


def adaln(c, w, b):
    bsz, d = c.shape
    n = w.shape[1]
    tn = 1024
    cp = jnp.zeros((SUBLANES, d), F32).at[:bsz].set(c)
    out = pl.pallas_call(
        _adaln_kernel,
        grid=(n // tn,),
        in_specs=[pl.BlockSpec((SUBLANES, d), lambda j: (0, 0)),
                  pl.BlockSpec((d, tn), lambda j: (0, j)),
                  pl.BlockSpec((1, tn), lambda j: (0, j))],
        out_specs=pl.BlockSpec((SUBLANES, tn), lambda j: (0, j)),
        out_shape=jax.ShapeDtypeStruct((SUBLANES, n), F32),
        compiler_params=_params("parallel"),
        name="adaln",
    )(cp, w, b.reshape(1, n))
    return out[:bsz]


def _modulated_norm(x, g, sc, sh):
    ms = jnp.mean(x * x, axis=-1, keepdims=True)
    return (x * lax.rsqrt(ms + EPS)) * g * (1.0 + sc) + sh


def _inproj_kernel(x_ref, g_ref, sc_ref, sh_ref, w_ref, o_ref, h_scr):
    @pl.when(pl.program_id(2) == 0)
    def _():
        h_scr[...] = _modulated_norm(x_ref[0], g_ref[...], sc_ref[0], sh_ref[0]).astype(BF16)

    o_ref[0] = jnp.dot(h_scr[...], w_ref[...], preferred_element_type=F32).astype(BF16)


def inproj(x, g, sc, sh, wcat, tm=1024, tn=1024):
    bsz, l, d = x.shape
    n = wcat.shape[1]
    tm = min(tm, l)
    return pl.pallas_call(
        _inproj_kernel,
        grid=(bsz, l // tm, n // tn),
        in_specs=[pl.BlockSpec((1, tm, d), lambda b, i, j: (b, i, 0)),
                  pl.BlockSpec((1, d), lambda b, i, j: (0, 0)),
                  pl.BlockSpec((1, 1, d), lambda b, i, j: (b, 0, 0)),
                  pl.BlockSpec((1, 1, d), lambda b, i, j: (b, 0, 0)),
                  pl.BlockSpec((d, tn), lambda b, i, j: (0, j))],
        out_specs=pl.BlockSpec((1, tm, tn), lambda b, i, j: (b, i, j)),
        out_shape=jax.ShapeDtypeStruct((bsz, l, n), BF16),
        scratch_shapes=[pltpu.VMEM((tm, d), BF16)],
        compiler_params=_params("parallel", "parallel", "arbitrary"),
        name="inproj",
    )(x, g.reshape(1, d), sc, sh, wcat)


def build_wcat(w_in):
    d = w_in.shape[0]
    pts = np.cumsum([NSA_WIDTH] + [KV_WIDTH] * 6 + [3 * NSA_HEADS, S5_WIDTH])
    q, kc, vc, ks, vs, kw, vw, gates, u, merge = jnp.split(w_in, pts, axis=1)

    def pair(k, v):
        return jnp.stack([k.reshape(d, NSA_KV_HEADS, HEAD_DIM), v.reshape(d, NSA_KV_HEADS, HEAD_DIM)],
                         axis=2).reshape(d, 2 * KV_WIDTH)

    gates_p = jnp.pad(gates, ((0, 0), (0, LANES - 3 * NSA_HEADS)))
    pad = jnp.zeros((d, PROJ_WIDTH - COL_GATES - LANES), w_in.dtype)
    wcat = jnp.concatenate([q * (HEAD_DIM ** -0.5 * LOG2E), u, merge, pair(ks, vs), pair(kw, vw),
                            kc, vc, gates_p, pad], axis=1)
    return wcat.astype(BF16)


def _compress_kernel(gk_ref, gv_ref, w1_ref, pos_ref, b1_ref, w2_ref, b2_ref, o_ref):
    half = CMP_STRIDE * HEAD_DIM
    nrow = gk_ref.shape[2]
    out = b2_ref[...]
    for j, g_ref in enumerate((gk_ref, gv_ref)):
        g = g_ref[0, 0]
        w1 = w1_ref[j]
        top = jnp.dot(g, w1[:half].astype(BF16), preferred_element_type=F32)
        bot = jnp.dot(g, w1[half:].astype(BF16), preferred_element_type=F32)
        cst = jnp.dot(pos_ref[j], w1, precision=HIGHEST, preferred_element_type=F32)[0:1] + b1_ref[j]
        hid = top + pltpu.roll(bot, nrow - 1, axis=0) + cst
        out = out + jnp.dot(_gelu(hid).astype(BF16), w2_ref[j].astype(BF16), preferred_element_type=F32)
    o_ref[0, 0] = out.astype(BF16)


def compress(gk, gv, w1, pos, b1, w2, b2):
    bsz, hkv, n16, wd = gk.shape
    posf = jnp.zeros((2, SUBLANES, CMP_LEN * HEAD_DIM), F32).at[:, 0].set(pos.reshape(2, -1))
    w2p = jnp.zeros((2, CMP_HIDDEN, 2 * HEAD_DIM), F32)
    w2p = w2p.at[0, :, :HEAD_DIM].set(w2[0]).at[1, :, HEAD_DIM:].set(w2[1])
    b2p = jnp.concatenate([b2[0], b2[1]]).reshape(1, 2 * HEAD_DIM)
    gspec = pl.BlockSpec((1, 1, n16, wd), lambda b, h: (b, h, 0, 0))
    full = lambda a: pl.BlockSpec(a.shape, lambda b, h: (0,) * a.ndim)
    b1r = b1.reshape(2, 1, CMP_HIDDEN)
    return pl.pallas_call(
        _compress_kernel,
        grid=(bsz, hkv),
        in_specs=[gspec, gspec, full(w1), full(posf), full(b1r), full(w2p), full(b2p)],
        out_specs=pl.BlockSpec((1, 1, n16, 2 * HEAD_DIM), lambda b, h: (b, h, 0, 0)),
        out_shape=jax.ShapeDtypeStruct((bsz, hkv, n16, 2 * HEAD_DIM), BF16),
        compiler_params=_params("parallel", "parallel"),
        name="compress",
    )(gk, gv, w1, posf, b1r, w2p, b2p)


def _t5_bucket_table(n):
    dist = np.arange(n)
    max_exact = REL_BUCKETS // 2
    ratio = np.log(np.maximum(dist, 1).astype(np.float32) / np.float32(max_exact)) / np.float32(
        math.log(REL_MAX_DIST / max_exact))
    large = np.minimum(max_exact + (ratio * np.float32(REL_BUCKETS - max_exact)).astype(np.int32), REL_BUCKETS - 1)
    return np.where(dist < max_exact, dist, large)


def _cmp_to_sel_t(n_cmp, n_sel):
    cs = np.arange(n_cmp)[:, None] * CMP_STRIDE
    ss = np.arange(n_sel)[None, :] * SEL_BLOCK
    ov = np.clip(np.minimum(cs + CMP_LEN, ss + SEL_BLOCK) - np.maximum(cs, ss), 0, None)
    return np.ascontiguousarray((ov / CMP_LEN).T.astype(np.float32))


def attention_tables(rel_bias):
    tq = ATT_TILE
    bucket = _t5_bucket_table(BIAS_SPAN)
    assert np.all(bucket[113:] == REL_BUCKETS - 1)
    bv = (rel_bias[bucket] - rel_bias[REL_BUCKETS - 1][None, :]).T * LOG2E
    heads = bv.shape[0]
    band0 = ATT_ORIGIN - BIAS_SPAN
    band = BIAS_SPAN + tq
    period = band + tq
    vext = jnp.concatenate([jnp.zeros((heads, period - BIAS_SPAN), F32), bv.astype(F32)], axis=1)
    toep = jnp.tile(vext, (1, band))[:, :band * (period - 1)].reshape(heads, band, period - 1)[:, :, :tq]
    tab = jnp.pad(toep, ((0, 0), (band0, ATT_TABLE_ROWS - band0 - band), (0, 0)))
    tab = tab.reshape(NSA_KV_HEADS, NSA_GROUP, ATT_TABLE_ROWS, tq).transpose(0, 2, 1, 3)
    tab = tab.reshape(NSA_KV_HEADS, ATT_TABLE_ROWS, NSA_GROUP * tq)
    first = CMP_BIAS_SLOTS // 2 * CMP_STRIDE - CMP_LEN + 1
    lo = CMP_BIAS_SLOTS * CMP_STRIDE - first
    fpad = jnp.pad(bv.astype(F32), ((0, 0), (lo, tq)))
    cmpb = jnp.stack([fpad[:, lo + first - CMP_STRIDE * r:][:, :tq] for r in range(CMP_BIAS_SLOTS)],
                     axis=1)
    cmpb = cmpb.reshape(NSA_KV_HEADS, NSA_GROUP, CMP_BIAS_SLOTS, tq).transpose(0, 2, 1, 3).reshape(
        NSA_KV_HEADS, CMP_BIAS_SLOTS, NSA_GROUP * tq)
    return tab, cmpb


def _importance_taps(n_cmp, n_sel):
    m = _cmp_to_sel_t(n_cmp, n_sel)
    ratio = SEL_BLOCK // CMP_STRIDE
    cols = np.nonzero(m[1])[0]
    taps = [(int(c - ratio), float(m[1, c])) for c in cols]
    for j in range(n_sel):
        want = np.zeros(n_cmp, np.float32)
        for k, w in taps:
            if 0 <= ratio * j + k < n_cmp:
                want[ratio * j + k] = w
        assert np.array_equal(want, m[j])
    return ratio, taps


def _attn_kernel(q_ref, sel_ref, selt_ref, win_ref, wint_ref, cmp_ref, cmpt_ref, gate_ref, tab_ref, cmpb_ref,
                 ind_ref, kmask_ref, o_ref, m_scr, acc_scr, sel_scr, s_scr, p_scr, *, taps, ratio):
    tq = ATT_TILE
    sk = ATT_SUPER
    cols = NSA_GROUP * tq
    kvh = pl.program_id(1)
    qt = pl.program_id(2)
    n_cmp = cmp_ref.shape[2]
    n_sel = sel_scr.shape[0]
    per = sk // tq
    pad = SUBLANES

    qT = q_ref[0]
    zpad = jnp.zeros((HEAD_DIM, tq), BF16)
    qpad = jnp.concatenate(
        [jnp.concatenate([qT[g * HEAD_DIM:(g + 1) * HEAD_DIM], zpad], axis=0) for g in range(NSA_GROUP)],
        axis=1)
    col_i = lax.broadcasted_iota(jnp.int32, (1, cols), 1) & (tq - 1)
    t_col = qt * tq + col_i

    ckv = cmp_ref[0, 0]
    s_scr[0:pad, :] = jnp.zeros((pad, cols), F32)
    s_scr[pad + n_cmp:, :] = jnp.zeros((pad, cols), F32)
    s_scr[pad:pad + n_cmp, :] = jnp.dot(ckv, qpad, preferred_element_type=F32)
    assert CMP_BIAS_SLOTS // 2 == pad
    slot0 = pl.multiple_of(qt * (tq // CMP_STRIDE), SUBLANES)
    s_scr[pl.ds(slot0, CMP_BIAS_SLOTS), :] += cmpb_ref[0]
    s = s_scr[pad:pad + n_cmp, :]
    n_idx = lax.broadcasted_iota(jnp.int32, (n_cmp, cols), 0)
    valid = (n_idx * CMP_STRIDE + (CMP_LEN - 1)) <= t_col
    s = jnp.where(valid, s, NEG_INF)
    mx = jnp.max(s, axis=0, keepdims=True)
    p = jnp.where(valid, jnp.exp2(s - mx), 0.0)
    den = jnp.sum(p, axis=0, keepdims=True)
    p = p * jnp.where(den > 0.0, 1.0 / den, 0.0)
    o_c = jnp.dot(cmpt_ref[0, 0], p.astype(BF16), preferred_element_type=F32)

    p_scr[0:pad, :] = jnp.zeros((pad, tq), F32)
    p_scr[pad:, :] = p[:, 0:tq] + p[:, tq:2 * tq] + p[:, 2 * tq:3 * tq] + p[:, 3 * tq:4 * tq]
    imp = jnp.zeros((n_sel, tq), F32)
    for k, w in taps:
        imp = imp + w * p_scr[pl.ds(pad + k, n_sel, stride=ratio), :]
    blk = lax.broadcasted_iota(jnp.int32, (n_sel, tq), 0)
    qi = lax.broadcasted_iota(jnp.int32, (n_sel, tq), 1)
    cur = jnp.right_shift(qt * tq + qi, SEL_SHIFT)

    @pl.when((qt + 1) * tq <= SEL_TOP * SEL_BLOCK)
    def _():
        sel_scr[...] = jnp.where(blk <= cur, 1.0, 0.0)

    @pl.when((qt + 1) * tq > SEL_TOP * SEL_BLOCK)
    def _():
        forced = (blk == 0) | (blk == cur) | (blk == cur - 1)
        val = jnp.where(blk <= cur, imp + FORCE_BONUS * forced.astype(F32), NEG_INF)
        rank = jnp.zeros((n_sel, tq), F32)
        for k in range(n_sel):
            rk = val[k:k + 1, :]
            beats = (rk > val) | ((rk == val) & (blk > k))
            rank = rank + beats.astype(F32)
        sel_scr[...] = jnp.where(rank < float(SEL_TOP), 1.0, 0.0)

    m_scr[...] = jnp.full(m_scr.shape, NEG_INF, F32)
    acc_scr[...] = jnp.zeros(acc_scr.shape, F32)
    st_last = qt // per
    dq = qt - st_last * per
    blocks = sk // SEL_BLOCK
    q_top = qpad[0:HEAD_DIM]
    mask_rows = max(blocks, 2 * SUBLANES)
    q_fill = jnp.zeros((HEAD_DIM - mask_rows, cols), BF16)

    def sel_scores(st, table_row, causal_shift, live):
        kv = sel_ref[0, pl.ds(pl.multiple_of(st * sk, sk), sk), :]
        kvm = kv * kmask_ref[...] + ind_ref[...]
        picked = sel_scr[pl.ds(pl.multiple_of(st * blocks, blocks), blocks), :]
        if live is not None:
            picked = picked * live
        off = (picked - 1.0) * MASK_BIG
        if blocks < mask_rows:
            off = jnp.concatenate([off, jnp.zeros((mask_rows - blocks, tq), F32)], axis=0)
        off = jnp.concatenate([off] * NSA_GROUP, axis=1).astype(BF16)
        sc = jnp.dot(kvm, jnp.concatenate([q_top, off, q_fill], axis=0), preferred_element_type=F32)
        if table_row is not None:
            sc = sc + tab_ref[0, pl.ds(pl.multiple_of(table_row, tq), sk), :]
        if causal_shift is not None:
            rel = col_i - lax.broadcasted_iota(jnp.int32, (sk, cols), 0)
            sc = jnp.where(rel + causal_shift >= 0, sc, NEG_INF)
        return sc

    def sel_tiles(*tiles):
        scs = [sel_scores(*t) for t in tiles]
        m_old = m_scr[...]
        m_new = m_old
        for sc in scs:
            m_new = jnp.maximum(m_new, jnp.max(sc, axis=0, keepdims=True))
        acc = jnp.exp2(m_old - m_new) * acc_scr[...]
        for (st, _, _, _), sc in zip(tiles, scs):
            pe = jnp.exp2(sc - m_new).astype(BF16)
            acc = acc + jnp.dot(selt_ref[0, 0, st], pe, preferred_element_type=F32)
        acc_scr[...] = acc
        m_scr[...] = m_new

    n_far = jnp.maximum(st_last - 1, 0)

    def pair_body(pr, carry):
        sel_tiles((2 * pr, None, None, None), (2 * pr + 1, None, None, None))
        return carry

    lax.fori_loop(0, n_far // 2, pair_body, 0)

    @pl.when(n_far % 2 == 1)
    def _():
        sel_tiles((n_far - 1, None, None, None))

    has_prev = jnp.where(st_last >= 1, 1.0, 0.0)
    sel_tiles((jnp.maximum(st_last - 1, 0), ATT_ORIGIN - sk - dq * tq, None, has_prev),
              (st_last, ATT_ORIGIN - dq * tq, dq * tq, None))
    o_s = acc_scr[HEAD_DIM:, :] * (1.0 / acc_scr[0:1, :])

    wk = WINDOW + tq
    n_w = wk // tq
    k0 = jnp.maximum(qt - WINDOW // tq, 0)
    dw = (qt - k0) * tq
    kv = win_ref[0, pl.ds(pl.multiple_of(k0 * tq, tq), wk), :]
    sc = jnp.dot(kv, qpad, preferred_element_type=F32)
    sc = sc + tab_ref[0, pl.ds(pl.multiple_of(ATT_ORIGIN - dw, tq), wk), :]
    dist = col_i - lax.broadcasted_iota(jnp.int32, (wk, cols), 0) + dw
    sc = jnp.where((dist >= 0) & (dist < WINDOW), sc, NEG_INF)
    mw = jnp.max(sc, axis=0, keepdims=True)
    pw = jnp.exp2(sc - mw).astype(BF16)
    ow = jnp.dot(wint_ref[0, 0, k0], pw[0:tq], preferred_element_type=F32)
    for m in range(1, n_w):
        ow = ow + jnp.dot(wint_ref[0, 0, k0 + m], pw[m * tq:(m + 1) * tq], preferred_element_type=F32)
    o_w = ow[HEAD_DIM:, :] * (1.0 / ow[0:1, :])

    branches = (o_c[HEAD_DIM:, :], o_s, o_w)
    outs = []
    for g in range(NSA_GROUP):
        acc = jnp.zeros((HEAD_DIM, tq), F32)
        for br in range(3):
            grow = gate_ref[0, pl.ds((kvh * NSA_GROUP + g) * 3 + br, 1), :]
            acc = acc + _sigmoid(grow) * branches[br][:, g * tq:(g + 1) * tq]
        outs.append(acc)
    for pr in range(NSA_GROUP // 2):
        pair = jnp.concatenate([outs[2 * pr], outs[2 * pr + 1]], axis=0)
        o_ref[0, :, pr * LANES:(pr + 1) * LANES] = pair.T.astype(BF16)


def attention(proj, cmp_kv, table, cmpb_t):
    bsz, l, _ = proj.shape
    tq = ATT_TILE
    sk = ATT_SUPER
    assert l % sk == 0 and l >= WINDOW + tq
    n_kt = l // tq
    n_st = l // sk
    n_cmp = cmp_kv.shape[2]
    n_sel = l // SEL_BLOCK
    ratio, taps = _importance_taps(n_cmp, n_sel)
    gw = NSA_GROUP * HEAD_DIM
    kvw = 2 * KV_WIDTH
    q_t = proj[:, :, COL_Q:COL_Q + NSA_WIDTH].transpose(0, 2, 1)

    def tiles_t(c0, nt):
        t = proj[:, :, c0:c0 + kvw].reshape(bsz, nt, l // nt, NSA_KV_HEADS, LANES).transpose(0, 3, 1, 4, 2)
        first = lax.broadcasted_iota(jnp.int32, t.shape, 3) == 0
        return jnp.where(first, jnp.ones((), BF16), t)

    sel_t = tiles_t(COL_SEL, n_st)
    win_t = tiles_t(COL_WIN, n_kt)
    cmp_t = cmp_kv.transpose(0, 1, 3, 2)
    gates_t = proj[:, :, COL_GATES:COL_GATES + LANES].astype(F32).transpose(0, 2, 1)
    key = np.arange(sk)[:, None]
    lane = np.arange(LANES)[None, :]
    ind = jnp.asarray((lane == HEAD_DIM + key // SEL_BLOCK).astype(np.float32), BF16)
    kmask = jnp.asarray(np.broadcast_to(lane < HEAD_DIM, (sk, LANES)).astype(np.float32), BF16)
    const = lambda a: pl.BlockSpec(a.shape, lambda b, h, i: (0,) * a.ndim)
    return pl.pallas_call(
        functools.partial(_attn_kernel, taps=taps, ratio=ratio),
        grid=(bsz, NSA_KV_HEADS, n_kt),
        in_specs=[pl.BlockSpec((1, gw, tq), lambda b, h, i: (b, h, i)),
                  pl.BlockSpec((1, l, LANES), lambda b, h, i: (b, 0, COL_SEL // LANES + h)),
                  pl.BlockSpec((1, 1, n_st, LANES, sk), lambda b, h, i: (b, h, 0, 0, 0)),
                  pl.BlockSpec((1, l, LANES), lambda b, h, i: (b, 0, COL_WIN // LANES + h)),
                  pl.BlockSpec((1, 1, n_kt, LANES, tq), lambda b, h, i: (b, h, 0, 0, 0)),
                  pl.BlockSpec((1, 1, n_cmp, LANES), lambda b, h, i: (b, h, 0, 0)),
                  pl.BlockSpec((1, 1, LANES, n_cmp), lambda b, h, i: (b, h, 0, 0)),
                  pl.BlockSpec((1, LANES, tq), lambda b, h, i: (b, 0, i)),
                  pl.BlockSpec((1, ATT_TABLE_ROWS, NSA_GROUP * tq), lambda b, h, i: (h, 0, 0)),
                  pl.BlockSpec((1, CMP_BIAS_SLOTS, NSA_GROUP * tq), lambda b, h, i: (h, 0, 0)),
                  const(ind), const(kmask)],
        out_specs=pl.BlockSpec((1, tq, gw), lambda b, h, i: (b, i, h)),
        out_shape=jax.ShapeDtypeStruct((bsz, l, NSA_WIDTH), BF16),
        scratch_shapes=[pltpu.VMEM((1, NSA_GROUP * tq), F32),
                        pltpu.VMEM((LANES, NSA_GROUP * tq), F32),
                        pltpu.VMEM((n_sel, tq), F32),
                        pltpu.VMEM((n_cmp + 2 * SUBLANES, NSA_GROUP * tq), F32),
                        pltpu.VMEM((n_cmp + SUBLANES, tq), F32)],
        compiler_params=_params("parallel", "parallel", "arbitrary"),
        name="nsa_attention",
    )(q_t, proj, sel_t, proj, win_t, cmp_kv, cmp_t, gates_t, table, cmpb_t, ind, kmask)


S5_BLOCK = 256
S5_CHUNK = S5_BLOCK // SUBLANES
S5_SLAB = 1024
S5_RANGE = MXU_DIM // S5_GROUP


def _s5_kernel(u_ref, perm_ref, permt_ref, bre_ref, bim_ref, are_ref, aim_ref, ace_ref, aci_ref, cre_ref, cim_ref,
               d_ref, gw_ref, gb_ref, o_ref, bu_re, bu_im, xs_re, xs_im, carry_re, carry_im, init_re, init_im):
    nr = S5_GROUPS // S5_RANGE
    rw = S5_RANGE * S5_GROUP
    sw = S5_RANGE * S5_STATE

    @pl.when(pl.program_id(1) == 0)
    def _():
        carry_re[...] = jnp.zeros(carry_re.shape, F32)
        carry_im[...] = jnp.zeros(carry_im.shape, F32)

    perm = perm_ref[...]
    up = jnp.dot(perm, u_ref[0], preferred_element_type=F32).astype(BF16)
    for r in range(nr):
        ur = up[:, r * rw:(r + 1) * rw]
        bu_re[:, r * sw:(r + 1) * sw] = jnp.dot(ur, bre_ref[r], preferred_element_type=F32)
        bu_im[:, r * sw:(r + 1) * sw] = jnp.dot(ur, bim_ref[r], preferred_element_type=F32)

    nslab = S5_FLAT // S5_SLAB
    row8 = lax.broadcasted_iota(jnp.int32, (SUBLANES, S5_SLAB), 0)
    for sl in range(nslab):
        cs = slice(sl * S5_SLAB, (sl + 1) * S5_SLAB)
        a_re = jnp.broadcast_to(are_ref[:, cs], (SUBLANES, S5_SLAB))
        a_im = jnp.broadcast_to(aim_ref[:, cs], (SUBLANES, S5_SLAB))

        def step1(s, st):
            s_re, s_im = st
            off = pl.multiple_of(s * SUBLANES, SUBLANES)
            n_re = a_re * s_re - a_im * s_im + bu_re[pl.ds(off, SUBLANES), cs]
            n_im = a_re * s_im + a_im * s_re + bu_im[pl.ds(off, SUBLANES), cs]
            return n_re, n_im

        zero = jnp.zeros((SUBLANES, S5_SLAB), F32)
        e_re, e_im = lax.fori_loop(0, S5_CHUNK, step1, (zero, zero))

        ac_re = ace_ref[:, cs]
        ac_im = aci_ref[:, cs]
        x_re = carry_re[:, cs]
        x_im = carry_im[:, cs]
        st_re = jnp.zeros((SUBLANES, S5_SLAB), F32)
        st_im = jnp.zeros((SUBLANES, S5_SLAB), F32)
        for c in range(SUBLANES):
            st_re = jnp.where(row8 == c, x_re, st_re)
            st_im = jnp.where(row8 == c, x_im, st_im)
            n_re = ac_re * x_re - ac_im * x_im + e_re[c:c + 1]
            n_im = ac_re * x_im + ac_im * x_re + e_im[c:c + 1]
            x_re, x_im = n_re, n_im
        carry_re[:, cs] = x_re
        carry_im[:, cs] = x_im
        init_re[:, cs] = st_re
        init_im[:, cs] = st_im

        def step2(s, st):
            s_re, s_im = st
            off = pl.multiple_of(s * SUBLANES, SUBLANES)
            n_re = a_re * s_re - a_im * s_im + bu_re[pl.ds(off, SUBLANES), cs]
            n_im = a_re * s_im + a_im * s_re + bu_im[pl.ds(off, SUBLANES), cs]
            xs_re[pl.ds(off, SUBLANES), cs] = n_re
            xs_im[pl.ds(off, SUBLANES), cs] = n_im
            return n_re, n_im

        lax.fori_loop(0, S5_CHUNK, step2, (st_re, st_im))

    ys = []
    for r in range(nr):
        xr = xs_re[:, r * sw:(r + 1) * sw].astype(BF16)
        xi = xs_im[:, r * sw:(r + 1) * sw].astype(BF16)
        ys.append(jnp.dot(xr, cre_ref[r], preferred_element_type=F32)
                  - jnp.dot(xi, cim_ref[r], preferred_element_type=F32))
    y = jnp.concatenate(ys, axis=1) + d_ref[...] * up.astype(F32)
    y = _gelu(y)
    z = jnp.dot(y.astype(BF16), gw_ref[...], preferred_element_type=F32) + gb_ref[...]
    y = (y * _sigmoid(z)).astype(BF16)
    o_ref[0] = jnp.dot(permt_ref[...], y, preferred_element_type=F32).astype(BF16)


def s5_tables(lam_re, lam_im, log_step, b_re, b_im, c_re, c_im):
    f32 = F32
    lam_re, lam_im = lam_re.astype(f32), lam_im.astype(f32)
    step = jnp.exp(log_step.astype(f32))[:, None]
    mag = jnp.exp(lam_re * step)
    lb_re, lb_im = mag * jnp.cos(lam_im * step), mag * jnp.sin(lam_im * step)
    den = lam_re * lam_re + lam_im * lam_im
    n_re = lb_re - 1.0
    f_re = (n_re * lam_re + lb_im * lam_im) / den
    f_im = (lb_im * lam_re - n_re * lam_im) / den
    b_re, b_im = b_re.astype(f32), b_im.astype(f32)
    bb_re = f_re[..., None] * b_re - f_im[..., None] * b_im
    bb_im = f_re[..., None] * b_im + f_im[..., None] * b_re
    magc = jnp.exp(lam_re * step * S5_CHUNK)
    ac_re, ac_im = magc * jnp.cos(lam_im * step * S5_CHUNK), magc * jnp.sin(lam_im * step * S5_CHUNK)

    nr = S5_GROUPS // S5_RANGE
    eye = jnp.eye(S5_RANGE, dtype=f32)

    def b_blocks(bb):
        bb = bb.reshape(nr, S5_RANGE, S5_STATE, S5_GROUP)
        return jnp.einsum('rgph,gk->rghkp', bb, eye).reshape(nr, S5_RANGE * S5_GROUP, S5_RANGE * S5_STATE).astype(BF16)

    def c_blocks(cc):
        cc = cc.astype(f32).reshape(nr, S5_RANGE, S5_GROUP, S5_STATE)
        return jnp.einsum('rghp,gk->rgpkh', cc, eye).reshape(nr, S5_RANGE * S5_STATE, S5_RANGE * S5_GROUP).astype(BF16)

    flat = lambda a: a.reshape(1, S5_FLAT)
    return (b_blocks(bb_re), b_blocks(bb_im), flat(lb_re), flat(lb_im), flat(ac_re), flat(ac_im),
            c_blocks(c_re), c_blocks(c_im))


def _s5_perm():
    p = np.zeros((S5_BLOCK, S5_BLOCK), np.float32)
    for c in range(SUBLANES):
        for s in range(S5_CHUNK):
            p[s * SUBLANES + c, c * S5_CHUNK + s] = 1.0
    return p


def s5_glu(proj, tables, d_skip, glu_w, glu_b):
    bsz, l, _ = proj.shape
    tb = S5_BLOCK
    bre, bim, are, aim, ace, aci, cre, cim = tables
    perm = jnp.asarray(_s5_perm(), BF16)
    permt = jnp.asarray(_s5_perm().T, BF16)
    full = lambda a: pl.BlockSpec(a.shape, lambda b, i: (0,) * a.ndim)
    d2 = d_skip.reshape(1, S5_WIDTH).astype(F32)
    gb2 = glu_b.reshape(1, S5_WIDTH).astype(F32)
    gw = glu_w.astype(BF16)
    return pl.pallas_call(
        _s5_kernel,
        grid=(bsz, l // tb),
        in_specs=[pl.BlockSpec((1, tb, S5_WIDTH), lambda b, i: (b, i, COL_U // S5_WIDTH)),
                  full(perm), full(permt), full(bre), full(bim), full(are), full(aim), full(ace), full(aci),
                  full(cre), full(cim), full(d2), full(gw), full(gb2)],
        out_specs=pl.BlockSpec((1, tb, S5_WIDTH), lambda b, i: (b, i, 0)),
        out_shape=jax.ShapeDtypeStruct((bsz, l, S5_WIDTH), BF16),
        scratch_shapes=[pltpu.VMEM((tb, S5_FLAT), F32), pltpu.VMEM((tb, S5_FLAT), F32),
                        pltpu.VMEM((tb, S5_FLAT), F32), pltpu.VMEM((tb, S5_FLAT), F32),
                        pltpu.VMEM((1, S5_FLAT), F32), pltpu.VMEM((1, S5_FLAT), F32),
                        pltpu.VMEM((SUBLANES, S5_FLAT), F32), pltpu.VMEM((SUBLANES, S5_FLAT), F32)],
        compiler_params=_params("parallel", "arbitrary"),
        name="s5_glu",
    )(proj, perm, permt, bre, bim, are, aim, ace, aci, cre, cim, d2, gw, gb2)


def _merge_kernel(x_ref, oa_ref, ys_ref, ga_ref, gb_ref, g1_ref, wa_ref, ws_ref, wo_ref,
                  ng_ref, sc_ref, sh_ref, x1_ref, h2_ref):
    a = jnp.dot(oa_ref[0], wa_ref[...], preferred_element_type=F32)
    s = jnp.dot(ys_ref[0], ws_ref[...], preferred_element_type=F32)
    mixed = _sigmoid(ga_ref[0].astype(F32)) * a + _sigmoid(gb_ref[0].astype(F32)) * s
    out = jnp.dot(mixed.astype(BF16), wo_ref[...], preferred_element_type=F32)
    x1 = x_ref[0] + g1_ref[0] * out
    x1_ref[0] = x1
    h2_ref[...] = _modulated_norm(x1, ng_ref[...], sc_ref[0], sh_ref[0]).T.astype(BF16)


def merge(x, o_attn, y_ssm, proj, g1, w_up_attn, w_up_ssm, w_out, norm_g, sc2, sh2, tm=256):
    bsz, l, d = x.shape
    tok = lambda w, cb=0: pl.BlockSpec((1, tm, w), lambda b, i: (b, i, cb))
    vec = pl.BlockSpec((1, 1, d), lambda b, i: (b, 0, 0))
    full = lambda a: pl.BlockSpec(a.shape, lambda b, i: (0,) * a.ndim)
    wa, ws, wo = w_up_attn.astype(BF16), w_up_ssm.astype(BF16), w_out.astype(BF16)
    ng = norm_g.reshape(1, d)
    return pl.pallas_call(
        _merge_kernel,
        grid=(bsz, l // tm),
        in_specs=[tok(d), tok(NSA_WIDTH), tok(S5_WIDTH), tok(d, COL_MERGE // d), tok(d, COL_MERGE // d + 1),
                  vec, full(wa), full(ws), full(wo), full(ng), vec, vec],
        out_specs=[tok(d), pl.BlockSpec((d, tm), lambda b, i: (0, b * (l // tm) + i))],
        out_shape=[jax.ShapeDtypeStruct((bsz, l, d), F32), jax.ShapeDtypeStruct((d, bsz * l), BF16)],
        compiler_params=_params("parallel", "parallel"),
        name="merge",
    )(x, o_attn, y_ssm, proj, proj, g1, wa, ws, wo, ng, sc2, sh2)


PEER_SCORE_TM = SUBLANES * LANES
PEER_RANKS = PEER_TOPK + 1
_PAIR_COUNTS = [PEER_RANKS // (a + 1) for a in range(PEER_RANKS)]
_N_PAIRS = sum(_PAIR_COUNTS)


def _rows_to_vregs(x):
    v = [x[:, b * LANES:(b + 1) * LANES] for b in range(SUBLANES)]
    sub = lax.broadcasted_iota(jnp.int32, (SUBLANES, LANES), 0)
    for dist in (4, 2, 1):
        keep = (sub & dist) == 0
        nv = list(v)
        for i in range(SUBLANES):
            if i & dist:
                continue
            lo, hi = v[i], v[i + dist]
            nv[i] = jnp.where(keep, lo, pltpu.roll(hi, dist, axis=0))
            nv[i + dist] = jnp.where(keep, pltpu.roll(lo, SUBLANES - dist, axis=0), hi)
        v = nv
    return v


def _tree_max(x):
    while x.shape[0] > 1:
        n = x.shape[0]
        half = n // 2
        top = jnp.maximum(x[:half], x[half:2 * half])
        x = top if n % 2 == 0 else jnp.concatenate([top, x[2 * half:]], axis=0)
    return x[0]


def _peer_score_kernel(ht_ref, wqt_ref, keys_ref, s_ref, st_ref, q_scr, key_scr, top_scr, cand_scr, best_scr):
    half = PEER_QDIM // 2
    nk = PEER_KEYS
    chunk = 2 * SUBLANES

    def drain(ref, out_ref, n_out):
        n = ref.shape[0]

        def body(r, prev):
            acc = jnp.full((chunk, SUBLANES, LANES), -jnp.inf, F32)
            for c0 in range(0, n, chunk):
                part = ref[c0:min(c0 + chunk, n)]
                part = jnp.where(part < prev[None], part, -jnp.inf)
                if part.shape[0] < chunk:
                    part = jnp.concatenate(
                        [part, jnp.full((chunk - part.shape[0], SUBLANES, LANES), -jnp.inf, F32)], axis=0)
                acc = jnp.maximum(acc, part)
            m = _tree_max(acc)
            out_ref[r] = m
            return m

        lax.fori_loop(0, n_out, body, jnp.full((SUBLANES, LANES), jnp.inf, F32))

    q_scr[...] = jnp.dot(wqt_ref[...], ht_ref[...], preferred_element_type=F32)

    def head(h, carry):
        for c in range(2):
            hc = 2 * h + c
            qt = q_scr[pl.ds(pl.multiple_of(hc * half, half), half), :]
            sc = jnp.dot(keys_ref[h, c], qt, precision=HIGHEST, preferred_element_type=F32)
            s_ref[hc] = sc
            for kg in range(nk // SUBLANES):
                rows = _rows_to_vregs(sc[kg * SUBLANES:(kg + 1) * SUBLANES, :])
                for k in range(SUBLANES):
                    key_scr[c, kg * SUBLANES + k] = rows[k]
            drain(key_scr.at[c], top_scr.at[c], PEER_RANKS)
        i = 0
        for a in range(PEER_RANKS):
            for bb in range(_PAIR_COUNTS[a]):
                cand_scr[i] = top_scr[0, a] + top_scr[1, bb]
                i += 1
        drain(cand_scr, best_scr, PEER_RANKS)
        theta = 0.5 * (best_scr[PEER_TOPK - 1] + best_scr[PEER_TOPK])
        z = jnp.zeros((SUBLANES, LANES), F32)
        for r in range(PEER_TOPK):
            z = z + jnp.exp(best_scr[r] - best_scr[0])
        for row, val in enumerate((theta, top_scr[0, 0], top_scr[1, 0], 1.0 / z)):
            for blk in range(SUBLANES):
                st_ref[h, row:row + 1, blk * LANES:(blk + 1) * LANES] = val[blk:blk + 1, :]
        st_ref[h, 4:, :] = jnp.zeros((SUBLANES - 4, st_ref.shape[2]), F32)
        return carry

    lax.fori_loop(0, PEER_HEADS, head, 0)


def peer_scores(h2t, w_q, sub_keys, tm=PEER_SCORE_TM):
    d, t = h2t.shape
    wqt = w_q.T.astype(BF16)
    nk = PEER_KEYS
    return pl.pallas_call(
        _peer_score_kernel,
        grid=(t // tm,),
        in_specs=[pl.BlockSpec((d, tm), lambda i: (0, i)),
                  pl.BlockSpec(wqt.shape, lambda i: (0, 0)),
                  pl.BlockSpec(sub_keys.shape, lambda i: (0, 0, 0, 0))],
        out_specs=[pl.BlockSpec((2 * PEER_HEADS, nk, tm), lambda i: (0, 0, i)),
                   pl.BlockSpec((PEER_HEADS, SUBLANES, tm), lambda i: (0, 0, i))],
        out_shape=[jax.ShapeDtypeStruct((2 * PEER_HEADS, nk, t), F32),
                   jax.ShapeDtypeStruct((PEER_HEADS, SUBLANES, t), F32)],
        scratch_shapes=[pltpu.VMEM((wqt.shape[0], tm), F32),
                        pltpu.VMEM((2, nk, SUBLANES, LANES), F32),
                        pltpu.VMEM((2, PEER_RANKS, SUBLANES, LANES), F32),
                        pltpu.VMEM((_N_PAIRS, SUBLANES, LANES), F32),
                        pltpu.VMEM((PEER_RANKS, SUBLANES, LANES), F32)],
        compiler_params=_params("parallel"),
        name="peer_scores",
    )(h2t, wqt, sub_keys)


PEER_TM = 512
PEER_TE = 1024
PEER_MXU_TILE = 256


def _peer_kernel(ht_ref, u_ref, vt_ref, s_ref, st_ref, o_ref, e_scr, thr_scr, e1_scr, at_a, at_b, gw_a, gw_b,
                 *, n_static):
    j = pl.program_id(1)
    nk = PEER_KEYS
    d, tm = ht_ref.shape
    te = u_ref.shape[0]
    rpt = te // nk
    mx = PEER_MXU_TILE

    @pl.when(j == 0)
    def _():
        o_ref[...] = jnp.zeros(o_ref.shape, F32)
        for h in range(PEER_HEADS):
            st = st_ref[h]
            e_scr[2 * h] = jnp.exp(s_ref[2 * h] - st[1:2]) * st[3:4]
            e_scr[2 * h + 1] = jnp.exp(s_ref[2 * h + 1] - st[2:3])

    def step(at_w, at_r, gw_w, gw_r, do1, do2, do3):
        def stage1(k):
            r0, c0 = (k // (tm // mx)) * mx, (k % (tm // mx)) * mx
            at_w[r0:r0 + mx, c0:c0 + mx] = jnp.dot(u_ref[r0:r0 + mx, :], ht_ref[:, c0:c0 + mx],
                                                   preferred_element_type=F32)

        def stage3(k):
            r0, c0 = (k // (tm // mx)) * mx, (k % (tm // mx)) * mx
            o_ref[r0:r0 + mx, c0:c0 + mx] += jnp.dot(vt_ref[0, r0:r0 + mx, :], gw_r[:, c0:c0 + mx],
                                                     preferred_element_type=F32)

        n1 = (te // mx) * (tm // mx) if do1 else 0
        n3 = (d // mx) * (tm // mx) if do3 else 0
        if not do2:
            for k in range(n1):
                stage1(k)
            for k in range(n3):
                stage3(k)
            return

        for h in range(PEER_HEADS):
            for rr in range(rpt):
                row = h * rpt + rr
                r = (j - 1) * rpt + rr
                thr_scr[row:row + 1, :] = st_ref[h, 0:1, :] - s_ref[2 * h, pl.ds(r, 1), :]
                e1_scr[row:row + 1, :] = e_scr[2 * h, pl.ds(r, 1), :]

        kh = nk // 4
        grp = 4
        n2 = (tm // LANES) * (nk // kh) * (rpt // grp)
        done1 = done3 = 0
        blk = 0
        for ch in range(tm // LANES):
            cs = slice(ch * LANES, (ch + 1) * LANES)
            for hf in range(nk // kh):
                ks = slice(hf * kh, (hf + 1) * kh)
                for g0 in range(0, rpt, grp):
                    ws = [jnp.zeros((kh, LANES), F32) for _ in range(grp)]
                    for h in range(PEER_HEADS):
                        s2c = s_ref[2 * h + 1, ks, cs]
                        e2c = e_scr[2 * h + 1, ks, cs]
                        for q in range(grp):
                            row = h * rpt + g0 + q
                            sel = jnp.where(s2c >= thr_scr[row:row + 1, cs], e2c, 0.0)
                            ws[q] = ws[q] + sel * e1_scr[row:row + 1, cs]
                    for q in range(grp):
                        r0 = (g0 + q) * nk + hf * kh
                        a = at_r[r0:r0 + kh, cs]
                        gw_w[r0:r0 + kh, cs] = _gelu(a.astype(BF16)) * ws[q].astype(BF16)
                    blk += 1
                    while done1 * n2 < blk * n1:
                        stage1(done1)
                        done1 += 1
                    while done3 * n2 < blk * n3:
                        stage3(done3)
                        done3 += 1

    bufs = ((at_a, at_b, gw_a, gw_b), (at_b, at_a, gw_b, gw_a))
    steady = (j >= 2) & (j < n_static)
    for jj, flags in ((0, (True, False, False)), (1, (True, True, False)),
                      (n_static, (False, True, True)), (n_static + 1, (False, False, True))):
        pl.when(j == jj)(functools.partial(step, *bufs[jj % 2], *flags))
    for par in range(2):
        pl.when(steady & (j % 2 == par))(functools.partial(step, *bufs[par], True, True, True))


def peer_experts(h2t, u_bf, vt_bf, scores, stats, tm=PEER_TM, te=PEER_TE):
    d, t = h2t.shape
    n_tiles = vt_bf.shape[0]
    return pl.pallas_call(
        functools.partial(_peer_kernel, n_static=n_tiles),
        grid=(t // tm, n_tiles + 2),
        in_specs=[pl.BlockSpec((d, tm), lambda i, j: (0, i)),
                  pl.BlockSpec((te, d), lambda i, j: (jnp.minimum(j, n_tiles - 1), 0)),
                  pl.BlockSpec((1, d, te), lambda i, j: (jnp.clip(j - 2, 0, n_tiles - 1), 0, 0)),
                  pl.BlockSpec((2 * PEER_HEADS, PEER_KEYS, tm), lambda i, j: (0, 0, i)),
                  pl.BlockSpec((PEER_HEADS, SUBLANES, tm), lambda i, j: (0, 0, i))],
        out_specs=pl.BlockSpec((d, tm), lambda i, j: (0, i)),
        out_shape=jax.ShapeDtypeStruct((d, t), F32),
        scratch_shapes=[pltpu.VMEM((2 * PEER_HEADS, PEER_KEYS, tm), F32),
                        pltpu.VMEM((PEER_HEADS * te // PEER_KEYS, tm), F32),
                        pltpu.VMEM((PEER_HEADS * te // PEER_KEYS, tm), F32),
                        pltpu.VMEM((te, tm), F32), pltpu.VMEM((te, tm), F32),
                        pltpu.VMEM((te, tm), BF16), pltpu.VMEM((te, tm), BF16)],
        compiler_params=_params("parallel", "arbitrary"),
        name="peer_experts",
    )(h2t, u_bf, vt_bf, scores, stats)


def _final_kernel(x1_ref, pt_ref, g2_ref, fg_ref, o_ref):
    x2 = x1_ref[...] + g2_ref[0] * pt_ref[...].T
    ms = jnp.mean(x2 * x2, axis=-1, keepdims=True)
    o_ref[...] = (x2 * lax.rsqrt(ms + EPS)) * fg_ref[...]


def final_norm(x1, peer_t, g2, final_g, l, tm=512):
    t, d = x1.shape
    tpb = l // tm
    return pl.pallas_call(
        _final_kernel,
        grid=(t // tm,),
        in_specs=[pl.BlockSpec((tm, d), lambda i: (i, 0)),
                  pl.BlockSpec((d, tm), lambda i: (0, i)),
                  pl.BlockSpec((1, 1, d), lambda i: (i // tpb, 0, 0)),
                  pl.BlockSpec((1, d), lambda i: (0, 0))],
        out_specs=pl.BlockSpec((tm, d), lambda i: (i, 0)),
        out_shape=jax.ShapeDtypeStruct((t, d), F32),
        compiler_params=_params("parallel"),
        name="final_norm",
    )(x1, peer_t, g2, final_g.reshape(1, d))


def kernel(x, c, ada_w, ada_b, norm_mix_g, norm_ffn_g, w_in, cmp_pos, cmp_w1, cmp_b1, cmp_w2, cmp_b2, rel_bias,
           s5_lam_re, s5_lam_im, s5_log_step, s5_b_re, s5_b_im, s5_c_re, s5_c_im, s5_d, glu_w, glu_b,
           w_up_attn, w_up_ssm, w_out, peer_w_q, peer_sub_keys, peer_u, peer_v, final_g):
    bsz, l, d = x.shape
    assert ada_w.shape[0] == 1, "single-layer block"
    mod = adaln(c, ada_w[0], ada_b[0])
    sh1, sc1, g1, sh2, sc2, g2 = [m[:, None, :] for m in jnp.split(mod, 6, axis=-1)]

    proj = inproj(x, norm_mix_g[0], sc1, sh1, build_wcat(w_in[0]))

    n16 = l // CMP_STRIDE

    def blocks16(cols):
        return cols.reshape(bsz, l, NSA_KV_HEADS, HEAD_DIM).transpose(0, 2, 1, 3).reshape(
            bsz, NSA_KV_HEADS, n16, CMP_STRIDE * HEAD_DIM)

    cmp_kv = compress(blocks16(proj[:, :, COL_CMP:COL_CMP + KV_WIDTH]),
                      blocks16(proj[:, :, COL_CMP + KV_WIDTH:COL_CMP + 2 * KV_WIDTH]),
                      cmp_w1[0], cmp_pos[0], cmp_b1[0], cmp_w2[0], cmp_b2[0])
    diag, cmpb = attention_tables(rel_bias)
    o_attn = attention(proj, cmp_kv, diag, cmpb)

    tables = s5_tables(s5_lam_re[0], s5_lam_im[0], s5_log_step[0], s5_b_re[0], s5_b_im[0],
                       s5_c_re[0], s5_c_im[0])
    y_ssm = s5_glu(proj, tables, s5_d[0], glu_w[0], glu_b[0])

    x1, h2t = merge(x, o_attn, y_ssm, proj, g1, w_up_attn[0], w_up_ssm[0], w_out[0],
                   norm_ffn_g[0], sc2, sh2)

    t = bsz * l
    scores, stats = peer_scores(h2t, peer_w_q[0], peer_sub_keys[0])
    v_tiles = peer_v[0].reshape(PEER_EXPERTS // PEER_TE, PEER_TE, d).transpose(0, 2, 1).astype(BF16)
    peer_t = peer_experts(h2t, peer_u[0].astype(BF16), v_tiles, scores, stats)
    out = final_norm(x1.reshape(t, d), peer_t, g2, final_g, l)
    return out.reshape(bsz, l, d)
```

```python
import functools
import math

import numpy as np
import jax
import jax.numpy as jnp
from jax import lax
from jax.experimental import pallas as pl
from jax.experimental.pallas import tpu as pltpu

F32 = jnp.float32
BF16 = jnp.bfloat16
HIGHEST = lax.Precision.HIGHEST

D_MODEL = 2048
NSA_HEADS = 16
NSA_KV_HEADS = 4
NSA_GROUP = NSA_HEADS // NSA_KV_HEADS
HEAD_DIM = 64
NSA_WIDTH = NSA_HEADS * HEAD_DIM
KV_WIDTH = NSA_KV_HEADS * HEAD_DIM
CMP_LEN = 32
CMP_STRIDE = 16
CMP_HIDDEN = 256
SEL_BLOCK = 64
SEL_TOP = 16
WINDOW = 512
S5_WIDTH = 1024
S5_GROUP = 16
S5_GROUPS = S5_WIDTH // S5_GROUP
S5_STATE = 64
S5_FLAT = S5_GROUPS * S5_STATE
REL_BUCKETS = 32
REL_MAX_DIST = 128
PEER_KEYS = 128
PEER_EXPERTS = PEER_KEYS * PEER_KEYS
PEER_HEADS = 8
PEER_TOPK = 16
PEER_QDIM = 256
EPS = 1e-6
NEG_INF = -1e30
FORCE_BONUS = 1e4
LOG2E = math.log2(math.e)
MASK_BIG = 1e30

VMEM_LIMIT_BYTES = 56 * 1024 * 1024
LANES = 128
SUBLANES = 8
MXU_DIM = 256

PROJ_WIDTH = 8192
COL_Q = 0
COL_U = 1024
COL_MERGE = 2048
COL_SEL = 6144
COL_WIN = 6656
COL_CMP = 7168
COL_GATES = 7680

ATT_TILE = 128
BIAS_SPAN = 256
CMP_BIAS_SLOTS = 16
SEL_SHIFT = SEL_BLOCK.bit_length() - 1
ATT_SUPER = 512
ATT_ORIGIN = 2 * ATT_SUPER - ATT_TILE
ATT_TABLE_ROWS = ATT_ORIGIN + WINDOW + ATT_TILE


def _params(*sem):
    return pltpu.CompilerParams(dimension_semantics=sem, vmem_limit_bytes=VMEM_LIMIT_BYTES)


def _gelu(x):
    return 0.5 * x * (1.0 + jnp.tanh(math.sqrt(2.0 / math.pi) * (x + 0.044715 * (x * x * x))))


def _sigmoid(x):
    return 1.0 / (1.0 + jnp.exp(-x))


def _adaln_kernel(c_ref, w_ref, b_ref, o_ref):
    c = c_ref[...]
    a = c * _sigmoid(c)
    o_ref[...] = jnp.dot(a, w_ref[...], precision=HIGHEST, preferred_element_type=F32) + b_ref[...]


def adaln(c, w, b):
    bsz, d = c.shape
    n = w.shape[1]
    tn = 1024
    cp = jnp.zeros((SUBLANES, d), F32).at[:bsz].set(c)
    out = pl.pallas_call(
        _adaln_kernel,
        grid=(n // tn,),
        in_specs=[pl.BlockSpec((SUBLANES, d), lambda j: (0, 0)),
                  pl.BlockSpec((d, tn), lambda j: (0, j)),
                  pl.BlockSpec((1, tn), lambda j: (0, j))],
        out_specs=pl.BlockSpec((SUBLANES, tn), lambda j: (0, j)),
        out_shape=jax.ShapeDtypeStruct((SUBLANES, n), F32),
        compiler_params=_params("parallel"),
        name="adaln",
    )(cp, w, b.reshape(1, n))
    return out[:bsz]


def _modulated_norm(x, g, sc, sh):
    ms = jnp.mean(x * x, axis=-1, keepdims=True)
    return (x * lax.rsqrt(ms + EPS)) * g * (1.0 + sc) + sh


def _inproj_kernel(x_ref, g_ref, sc_ref, sh_ref, w_ref, o_ref, h_scr):
    @pl.when(pl.program_id(2) == 0)
    def _():
        h_scr[...] = _modulated_norm(x_ref[0], g_ref[...], sc_ref[0], sh_ref[0]).astype(BF16)

    o_ref[0] = jnp.dot(h_scr[...], w_ref[...], preferred_element_type=F32).astype(BF16)


def inproj(x, g, sc, sh, wcat, tm=1024, tn=1024):
    bsz, l, d = x.shape
    n = wcat.shape[1]
    tm = min(tm, l)
    return pl.pallas_call(
        _inproj_kernel,
        grid=(bsz, l // tm, n // tn),
        in_specs=[pl.BlockSpec((1, tm, d), lambda b, i, j: (b, i, 0)),
                  pl.BlockSpec((1, d), lambda b, i, j: (0, 0)),
                  pl.BlockSpec((1, 1, d), lambda b, i, j: (b, 0, 0)),
                  pl.BlockSpec((1, 1, d), lambda b, i, j: (b, 0, 0)),
                  pl.BlockSpec((d, tn), lambda b, i, j: (0, j))],
        out_specs=pl.BlockSpec((1, tm, tn), lambda b, i, j: (b, i, j)),
        out_shape=jax.ShapeDtypeStruct((bsz, l, n), BF16),
        scratch_shapes=[pltpu.VMEM((tm, d), BF16)],
        compiler_params=_params("parallel", "parallel", "arbitrary"),
        name="inproj",
    )(x, g.reshape(1, d), sc, sh, wcat)


def build_wcat(w_in):
    d = w_in.shape[0]
    pts = np.cumsum([NSA_WIDTH] + [KV_WIDTH] * 6 + [3 * NSA_HEADS, S5_WIDTH])
    q, kc, vc, ks, vs, kw, vw, gates, u, merge = jnp.split(w_in, pts, axis=1)

    def pair(k, v):
        return jnp.stack([k.reshape(d, NSA_KV_HEADS, HEAD_DIM), v.reshape(d, NSA_KV_HEADS, HEAD_DIM)],
                         axis=2).reshape(d, 2 * KV_WIDTH)

    gates_p = jnp.pad(gates, ((0, 0), (0, LANES - 3 * NSA_HEADS)))
    pad = jnp.zeros((d, PROJ_WIDTH - COL_GATES - LANES), w_in.dtype)
    wcat = jnp.concatenate([q * (HEAD_DIM ** -0.5 * LOG2E), u, merge, pair(ks, vs), pair(kw, vw),
                            kc, vc, gates_p, pad], axis=1)
    return wcat.astype(BF16)


def _compress_kernel(gk_ref, gv_ref, w1_ref, pos_ref, b1_ref, w2_ref, b2_ref, o_ref):
    half = CMP_STRIDE * HEAD_DIM
    nrow = gk_ref.shape[2]
    out = b2_ref[...]
    for j, g_ref in enumerate((gk_ref, gv_ref)):
        g = g_ref[0, 0]
        w1 = w1_ref[j]
        top = jnp.dot(g, w1[:half].astype(BF16), preferred_element_type=F32)
        bot = jnp.dot(g, w1[half:].astype(BF16), preferred_element_type=F32)
        cst = jnp.dot(pos_ref[j], w1, precision=HIGHEST, preferred_element_type=F32)[0:1] + b1_ref[j]
        hid = top + pltpu.roll(bot, nrow - 1, axis=0) + cst
        out = out + jnp.dot(_gelu(hid).astype(BF16), w2_ref[j].astype(BF16), preferred_element_type=F32)
    o_ref[0, 0] = out.astype(BF16)


def compress(gk, gv, w1, pos, b1, w2, b2):
    bsz, hkv, n16, wd = gk.shape
    posf = jnp.zeros((2, SUBLANES, CMP_LEN * HEAD_DIM), F32).at[:, 0].set(pos.reshape(2, -1))
    w2p = jnp.zeros((2, CMP_HIDDEN, 2 * HEAD_DIM), F32)
    w2p = w2p.at[0, :, :HEAD_DIM].set(w2[0]).at[1, :, HEAD_DIM:].set(w2[1])
    b2p = jnp.concatenate([b2[0], b2[1]]).reshape(1, 2 * HEAD_DIM)
    gspec = pl.BlockSpec((1, 1, n16, wd), lambda b, h: (b, h, 0, 0))
    full = lambda a: pl.BlockSpec(a.shape, lambda b, h: (0,) * a.ndim)
    b1r = b1.reshape(2, 1, CMP_HIDDEN)
    return pl.pallas_call(
        _compress_kernel,
        grid=(bsz, hkv),
        in_specs=[gspec, gspec, full(w1), full(posf), full(b1r), full(w2p), full(b2p)],
        out_specs=pl.BlockSpec((1, 1, n16, 2 * HEAD_DIM), lambda b, h: (b, h, 0, 0)),
        out_shape=jax.ShapeDtypeStruct((bsz, hkv, n16, 2 * HEAD_DIM), BF16),
        compiler_params=_params("parallel", "parallel"),
        name="compress",
    )(gk, gv, w1, posf, b1r, w2p, b2p)


def _t5_bucket_table(n):
    dist = np.arange(n)
    max_exact = REL_BUCKETS // 2
    ratio = np.log(np.maximum(dist, 1).astype(np.float32) / np.float32(max_exact)) / np.float32(
        math.log(REL_MAX_DIST / max_exact))
    large = np.minimum(max_exact + (ratio * np.float32(REL_BUCKETS - max_exact)).astype(np.int32), REL_BUCKETS - 1)
    return np.where(dist < max_exact, dist, large)


def _cmp_to_sel_t(n_cmp, n_sel):
    cs = np.arange(n_cmp)[:, None] * CMP_STRIDE
    ss = np.arange(n_sel)[None, :] * SEL_BLOCK
    ov = np.clip(np.minimum(cs + CMP_LEN, ss + SEL_BLOCK) - np.maximum(cs, ss), 0, None)
    return np.ascontiguousarray((ov / CMP_LEN).T.astype(np.float32))


def attention_tables(rel_bias):
    tq = ATT_TILE
    bucket = _t5_bucket_table(BIAS_SPAN)
    assert np.all(bucket[113:] == REL_BUCKETS - 1)
    bv = (rel_bias[bucket] - rel_bias[REL_BUCKETS - 1][None, :]).T * LOG2E
    heads = bv.shape[0]
    band0 = ATT_ORIGIN - BIAS_SPAN
    band = BIAS_SPAN + tq
    period = band + tq
    vext = jnp.concatenate([jnp.zeros((heads, period - BIAS_SPAN), F32), bv.astype(F32)], axis=1)
    toep = jnp.tile(vext, (1, band))[:, :band * (period - 1)].reshape(heads, band, period - 1)[:, :, :tq]
    tab = jnp.pad(toep, ((0, 0), (band0, ATT_TABLE_ROWS - band0 - band), (0, 0)))
    tab = tab.reshape(NSA_KV_HEADS, NSA_GROUP, ATT_TABLE_ROWS, tq).transpose(0, 2, 1, 3)
    tab = tab.reshape(NSA_KV_HEADS, ATT_TABLE_ROWS, NSA_GROUP * tq)
    first = CMP_BIAS_SLOTS // 2 * CMP_STRIDE - CMP_LEN + 1
    lo = CMP_BIAS_SLOTS * CMP_STRIDE - first
    fpad = jnp.pad(bv.astype(F32), ((0, 0), (lo, tq)))
    cmpb = jnp.stack([fpad[:, lo + first - CMP_STRIDE * r:][:, :tq] for r in range(CMP_BIAS_SLOTS)],
                     axis=1)
    cmpb = cmpb.reshape(NSA_KV_HEADS, NSA_GROUP, CMP_BIAS_SLOTS, tq).transpose(0, 2, 1, 3).reshape(
        NSA_KV_HEADS, CMP_BIAS_SLOTS, NSA_GROUP * tq)
    return tab, cmpb


def _importance_taps(n_cmp, n_sel):
    m = _cmp_to_sel_t(n_cmp, n_sel)
    ratio = SEL_BLOCK // CMP_STRIDE
    cols = np.nonzero(m[1])[0]
    taps = [(int(c - ratio), float(m[1, c])) for c in cols]
    for j in range(n_sel):
        want = np.zeros(n_cmp, np.float32)
        for k, w in taps:
            if 0 <= ratio * j + k < n_cmp:
                want[ratio * j + k] = w
        assert np.array_equal(want, m[j])
    return ratio, taps


def _attn_kernel(q_ref, sel_ref, selt_ref, win_ref, wint_ref, cmp_ref, cmpt_ref, gate_ref, tab_ref, cmpb_ref,
                 ind_ref, kmask_ref, o_ref, m_scr, acc_scr, sel_scr, s_scr, p_scr, *, taps, ratio):
    tq = ATT_TILE
    sk = ATT_SUPER
    cols = NSA_GROUP * tq
    kvh = pl.program_id(1)
    qt = pl.program_id(2)
    n_cmp = cmp_ref.shape[2]
    n_sel = sel_scr.shape[0]
    per = sk // tq
    pad = SUBLANES

    qT = q_ref[0]
    zpad = jnp.zeros((HEAD_DIM, tq), BF16)
    qpad = jnp.concatenate(
        [jnp.concatenate([qT[g * HEAD_DIM:(g + 1) * HEAD_DIM], zpad], axis=0) for g in range(NSA_GROUP)],
        axis=1)
    col_i = lax.broadcasted_iota(jnp.int32, (1, cols), 1) & (tq - 1)
    t_col = qt * tq + col_i

    ckv = cmp_ref[0, 0]
    s_scr[0:pad, :] = jnp.zeros((pad, cols), F32)
    s_scr[pad + n_cmp:, :] = jnp.zeros((pad, cols), F32)
    s_scr[pad:pad + n_cmp, :] = jnp.dot(ckv, qpad, preferred_element_type=F32)
    assert CMP_BIAS_SLOTS // 2 == pad
    slot0 = pl.multiple_of(qt * (tq // CMP_STRIDE), SUBLANES)
    s_scr[pl.ds(slot0, CMP_BIAS_SLOTS), :] += cmpb_ref[0]
    s = s_scr[pad:pad + n_cmp, :]
    n_idx = lax.broadcasted_iota(jnp.int32, (n_cmp, cols), 0)
    valid = (n_idx * CMP_STRIDE + (CMP_LEN - 1)) <= t_col
    s = jnp.where(valid, s, NEG_INF)
    mx = jnp.max(s, axis=0, keepdims=True)
    p = jnp.where(valid, jnp.exp2(s - mx), 0.0)
    den = jnp.sum(p, axis=0, keepdims=True)
    p = p * jnp.where(den > 0.0, 1.0 / den, 0.0)
    o_c = jnp.dot(cmpt_ref[0, 0], p.astype(BF16), preferred_element_type=F32)

    p_scr[0:pad, :] = jnp.zeros((pad, tq), F32)
    p_scr[pad:, :] = p[:, 0:tq] + p[:, tq:2 * tq] + p[:, 2 * tq:3 * tq] + p[:, 3 * tq:4 * tq]
    imp = jnp.zeros((n_sel, tq), F32)
    for k, w in taps:
        imp = imp + w * p_scr[pl.ds(pad + k, n_sel, stride=ratio), :]
    blk = lax.broadcasted_iota(jnp.int32, (n_sel, tq), 0)
    qi = lax.broadcasted_iota(jnp.int32, (n_sel, tq), 1)
    cur = jnp.right_shift(qt * tq + qi, SEL_SHIFT)

    @pl.when((qt + 1) * tq <= SEL_TOP * SEL_BLOCK)
    def _():
        sel_scr[...] = jnp.where(blk <= cur, 1.0, 0.0)

    @pl.when((qt + 1) * tq > SEL_TOP * SEL_BLOCK)
    def _():
        forced = (blk == 0) | (blk == cur) | (blk == cur - 1)
        val = jnp.where(blk <= cur, imp + FORCE_BONUS * forced.astype(F32), NEG_INF)
        rank = jnp.zeros((n_sel, tq), F32)
        for k in range(n_sel):
            rk = val[k:k + 1, :]
            beats = (rk > val) | ((rk == val) & (blk > k))
            rank = rank + beats.astype(F32)
        sel_scr[...] = jnp.where(rank < float(SEL_TOP), 1.0, 0.0)

    m_scr[...] = jnp.full(m_scr.shape, NEG_INF, F32)
    acc_scr[...] = jnp.zeros(acc_scr.shape, F32)
    st_last = qt // per
    dq = qt - st_last * per
    blocks = sk // SEL_BLOCK
    q_top = qpad[0:HEAD_DIM]
    mask_rows = max(blocks, 2 * SUBLANES)
    q_fill = jnp.zeros((HEAD_DIM - mask_rows, cols), BF16)

    def sel_scores(st, table_row, causal_shift, live):
        kv = sel_ref[0, pl.ds(pl.multiple_of(st * sk, sk), sk), :]
        kvm = kv * kmask_ref[...] + ind_ref[...]
        picked = sel_scr[pl.ds(pl.multiple_of(st * blocks, blocks), blocks), :]
        if live is not None:
            picked = picked * live
        off = (picked - 1.0) * MASK_BIG
        if blocks < mask_rows:
            off = jnp.concatenate([off, jnp.zeros((mask_rows - blocks, tq), F32)], axis=0)
        off = jnp.concatenate([off] * NSA_GROUP, axis=1).astype(BF16)
        sc = jnp.dot(kvm, jnp.concatenate([q_top, off, q_fill], axis=0), preferred_element_type=F32)
        if table_row is not None:
            sc = sc + tab_ref[0, pl.ds(pl.multiple_of(table_row, tq), sk), :]
        if causal_shift is not None:
            rel = col_i - lax.broadcasted_iota(jnp.int32, (sk, cols), 0)
            sc = jnp.where(rel + causal_shift >= 0, sc, NEG_INF)
        return sc

    def sel_tiles(*tiles):
        scs = [sel_scores(*t) for t in tiles]
        m_old = m_scr[...]
        m_new = m_old
        for sc in scs:
            m_new = jnp.maximum(m_new, jnp.max(sc, axis=0, keepdims=True))
        acc = jnp.exp2(m_old - m_new) * acc_scr[...]
        for (st, _, _, _), sc in zip(tiles, scs):
            pe = jnp.exp2(sc - m_new).astype(BF16)
            acc = acc + jnp.dot(selt_ref[0, 0, st], pe, preferred_element_type=F32)
        acc_scr[...] = acc
        m_scr[...] = m_new

    n_far = jnp.maximum(st_last - 1, 0)

    def pair_body(pr, carry):
        sel_tiles((2 * pr, None, None, None), (2 * pr + 1, None, None, None))
        return carry

    lax.fori_loop(0, n_far // 2, pair_body, 0)

    @pl.when(n_far % 2 == 1)
    def _():
        sel_tiles((n_far - 1, None, None, None))

    has_prev = jnp.where(st_last >= 1, 1.0, 0.0)
    sel_tiles((jnp.maximum(st_last - 1, 0), ATT_ORIGIN - sk - dq * tq, None, has_prev),
              (st_last, ATT_ORIGIN - dq * tq, dq * tq, None))
    o_s = acc_scr[HEAD_DIM:, :] * (1.0 / acc_scr[0:1, :])

    wk = WINDOW + tq
    n_w = wk // tq
    k0 = jnp.maximum(qt - WINDOW // tq, 0)
    dw = (qt - k0) * tq
    kv = win_ref[0, pl.ds(pl.multiple_of(k0 * tq, tq), wk), :]
    sc = jnp.dot(kv, qpad, preferred_element_type=F32)
    sc = sc + tab_ref[0, pl.ds(pl.multiple_of(ATT_ORIGIN - dw, tq), wk), :]
    dist = col_i - lax.broadcasted_iota(jnp.int32, (wk, cols), 0) + dw
    sc = jnp.where((dist >= 0) & (dist < WINDOW), sc, NEG_INF)
    mw = jnp.max(sc, axis=0, keepdims=True)
    pw = jnp.exp2(sc - mw).astype(BF16)
    ow = jnp.dot(wint_ref[0, 0, k0], pw[0:tq], preferred_element_type=F32)
    for m in range(1, n_w):
        ow = ow + jnp.dot(wint_ref[0, 0, k0 + m], pw[m * tq:(m + 1) * tq], preferred_element_type=F32)
    o_w = ow[HEAD_DIM:, :] * (1.0 / ow[0:1, :])

    branches = (o_c[HEAD_DIM:, :], o_s, o_w)
    outs = []
    for g in range(NSA_GROUP):
        acc = jnp.zeros((HEAD_DIM, tq), F32)
        for br in range(3):
            grow = gate_ref[0, pl.ds((kvh * NSA_GROUP + g) * 3 + br, 1), :]
            acc = acc + _sigmoid(grow) * branches[br][:, g * tq:(g + 1) * tq]
        outs.append(acc)
    for pr in range(NSA_GROUP // 2):
        pair = jnp.concatenate([outs[2 * pr], outs[2 * pr + 1]], axis=0)
        o_ref[0, :, pr * LANES:(pr + 1) * LANES] = pair.T.astype(BF16)


def attention(proj, cmp_kv, table, cmpb_t):
    bsz, l, _ = proj.shape
    tq = ATT_TILE
    sk = ATT_SUPER
    assert l % sk == 0 and l >= WINDOW + tq
    n_kt = l // tq
    n_st = l // sk
    n_cmp = cmp_kv.shape[2]
    n_sel = l // SEL_BLOCK
    ratio, taps = _importance_taps(n_cmp, n_sel)
    gw = NSA_GROUP * HEAD_DIM
    kvw = 2 * KV_WIDTH
    q_t = proj[:, :, COL_Q:COL_Q + NSA_WIDTH].transpose(0, 2, 1)

    def tiles_t(c0, nt):
        t = proj[:, :, c0:c0 + kvw].reshape(bsz, nt, l // nt, NSA_KV_HEADS, LANES).transpose(0, 3, 1, 4, 2)
        first = lax.broadcasted_iota(jnp.int32, t.shape, 3) == 0
        return jnp.where(first, jnp.ones((), BF16), t)

    sel_t = tiles_t(COL_SEL, n_st)
    win_t = tiles_t(COL_WIN, n_kt)
    cmp_t = cmp_kv.transpose(0, 1, 3, 2)
    gates_t = proj[:, :, COL_GATES:COL_GATES + LANES].astype(F32).transpose(0, 2, 1)
    key = np.arange(sk)[:, None]
    lane = np.arange(LANES)[None, :]
    ind = jnp.asarray((lane == HEAD_DIM + key // SEL_BLOCK).astype(np.float32), BF16)
    kmask = jnp.asarray(np.broadcast_to(lane < HEAD_DIM, (sk, LANES)).astype(np.float32), BF16)
    const = lambda a: pl.BlockSpec(a.shape, lambda b, h, i: (0,) * a.ndim)
    return pl.pallas_call(
        functools.partial(_attn_kernel, taps=taps, ratio=ratio),
        grid=(bsz, NSA_KV_HEADS, n_kt),
        in_specs=[pl.BlockSpec((1, gw, tq), lambda b, h, i: (b, h, i)),
                  pl.BlockSpec((1, l, LANES), lambda b, h, i: (b, 0, COL_SEL // LANES + h)),
                  pl.BlockSpec((1, 1, n_st, LANES, sk), lambda b, h, i: (b, h, 0, 0, 0)),
                  pl.BlockSpec((1, l, LANES), lambda b, h, i: (b, 0, COL_WIN // LANES + h)),
                  pl.BlockSpec((1, 1, n_kt, LANES, tq), lambda b, h, i: (b, h, 0, 0, 0)),
                  pl.BlockSpec((1, 1, n_cmp, LANES), lambda b, h, i: (b, h, 0, 0)),
                  pl.BlockSpec((1, 1, LANES, n_cmp), lambda b, h, i: (b, h, 0, 0)),
                  pl.BlockSpec((1, LANES, tq), lambda b, h, i: (b, 0, i)),
                  pl.BlockSpec((1, ATT_TABLE_ROWS, NSA_GROUP * tq), lambda b, h, i: (h, 0, 0)),
                  pl.BlockSpec((1, CMP_BIAS_SLOTS, NSA_GROUP * tq), lambda b, h, i: (h, 0, 0)),
                  const(ind), const(kmask)],
        out_specs=pl.BlockSpec((1, tq, gw), lambda b, h, i: (b, i, h)),
        out_shape=jax.ShapeDtypeStruct((bsz, l, NSA_WIDTH), BF16),
        scratch_shapes=[pltpu.VMEM((1, NSA_GROUP * tq), F32),
                        pltpu.VMEM((LANES, NSA_GROUP * tq), F32),
                        pltpu.VMEM((n_sel, tq), F32),
                        pltpu.VMEM((n_cmp + 2 * SUBLANES, NSA_GROUP * tq), F32),
                        pltpu.VMEM((n_cmp + SUBLANES, tq), F32)],
        compiler_params=_params("parallel", "parallel", "arbitrary"),
        name="nsa_attention",
    )(q_t, proj, sel_t, proj, win_t, cmp_kv, cmp_t, gates_t, table, cmpb_t, ind, kmask)


S5_BLOCK = 256
S5_CHUNK = S5_BLOCK // SUBLANES
S5_SLAB = 1024
S5_RANGE = MXU_DIM // S5_GROUP


def _s5_kernel(u_ref, perm_ref, permt_ref, bre_ref, bim_ref, are_ref, aim_ref, ace_ref, aci_ref, cre_ref, cim_ref,
               d_ref, gw_ref, gb_ref, o_ref, bu_re, bu_im, xs_re, xs_im, carry_re, carry_im, init_re, init_im):
    nr = S5_GROUPS // S5_RANGE
    rw = S5_RANGE * S5_GROUP
    sw = S5_RANGE * S5_STATE

    @pl.when(pl.program_id(1) == 0)
    def _():
        carry_re[...] = jnp.zeros(carry_re.shape, F32)
        carry_im[...] = jnp.zeros(carry_im.shape, F32)

    perm = perm_ref[...]
    up = jnp.dot(perm, u_ref[0], preferred_element_type=F32).astype(BF16)
    for r in range(nr):
        ur = up[:, r * rw:(r + 1) * rw]
        bu_re[:, r * sw:(r + 1) * sw] = jnp.dot(ur, bre_ref[r], preferred_element_type=F32)
        bu_im[:, r * sw:(r + 1) * sw] = jnp.dot(ur, bim_ref[r], preferred_element_type=F32)

    nslab = S5_FLAT // S5_SLAB
    row8 = lax.broadcasted_iota(jnp.int32, (SUBLANES, S5_SLAB), 0)
    for sl in range(nslab):
        cs = slice(sl * S5_SLAB, (sl + 1) * S5_SLAB)
        a_re = jnp.broadcast_to(are_ref[:, cs], (SUBLANES, S5_SLAB))
        a_im = jnp.broadcast_to(aim_ref[:, cs], (SUBLANES, S5_SLAB))

        def step1(s, st):
            s_re, s_im = st
            off = pl.multiple_of(s * SUBLANES, SUBLANES)
            n_re = a_re * s_re - a_im * s_im + bu_re[pl.ds(off, SUBLANES), cs]
            n_im = a_re * s_im + a_im * s_re + bu_im[pl.ds(off, SUBLANES), cs]
            return n_re, n_im

        zero = jnp.zeros((SUBLANES, S5_SLAB), F32)
        e_re, e_im = lax.fori_loop(0, S5_CHUNK, step1, (zero, zero))

        ac_re = ace_ref[:, cs]
        ac_im = aci_ref[:, cs]
        x_re = carry_re[:, cs]
        x_im = carry_im[:, cs]
        st_re = jnp.zeros((SUBLANES, S5_SLAB), F32)
        st_im = jnp.zeros((SUBLANES, S5_SLAB), F32)
        for c in range(SUBLANES):
            st_re = jnp.where(row8 == c, x_re, st_re)
            st_im = jnp.where(row8 == c, x_im, st_im)
            n_re = ac_re * x_re - ac_im * x_im + e_re[c:c + 1]
            n_im = ac_re * x_im + ac_im * x_re + e_im[c:c + 1]
            x_re, x_im = n_re, n_im
        carry_re[:, cs] = x_re
        carry_im[:, cs] = x_im
        init_re[:, cs] = st_re
        init_im[:, cs] = st_im

        def step2(s, st):
            s_re, s_im = st
            off = pl.multiple_of(s * SUBLANES, SUBLANES)
            n_re = a_re * s_re - a_im * s_im + bu_re[pl.ds(off, SUBLANES), cs]
            n_im = a_re * s_im + a_im * s_re + bu_im[pl.ds(off, SUBLANES), cs]
            xs_re[pl.ds(off, SUBLANES), cs] = n_re
            xs_im[pl.ds(off, SUBLANES), cs] = n_im
            return n_re, n_im

        lax.fori_loop(0, S5_CHUNK, step2, (st_re, st_im))

    ys = []
    for r in range(nr):
        xr = xs_re[:, r * sw:(r + 1) * sw].astype(BF16)
        xi = xs_im[:, r * sw:(r + 1) * sw].astype(BF16)
        ys.append(jnp.dot(xr, cre_ref[r], preferred_element_type=F32)
                  - jnp.dot(xi, cim_ref[r], preferred_element_type=F32))
    y = jnp.concatenate(ys, axis=1) + d_ref[...] * up.astype(F32)
    y = _gelu(y)
    z = jnp.dot(y.astype(BF16), gw_ref[...], preferred_element_type=F32) + gb_ref[...]
    y = (y * _sigmoid(z)).astype(BF16)
    o_ref[0] = jnp.dot(permt_ref[...], y, preferred_element_type=F32).astype(BF16)


def s5_tables(lam_re, lam_im, log_step, b_re, b_im, c_re, c_im):
    f32 = F32
    lam_re, lam_im = lam_re.astype(f32), lam_im.astype(f32)
    step = jnp.exp(log_step.astype(f32))[:, None]
    mag = jnp.exp(lam_re * step)
    lb_re, lb_im = mag * jnp.cos(lam_im * step), mag * jnp.sin(lam_im * step)
    den = lam_re * lam_re + lam_im * lam_im
    n_re = lb_re - 1.0
    f_re = (n_re * lam_re + lb_im * lam_im) / den
    f_im = (lb_im * lam_re - n_re * lam_im) / den
    b_re, b_im = b_re.astype(f32), b_im.astype(f32)
    bb_re = f_re[..., None] * b_re - f_im[..., None] * b_im
    bb_im = f_re[..., None] * b_im + f_im[..., None] * b_re
    magc = jnp.exp(lam_re * step * S5_CHUNK)
    ac_re, ac_im = magc * jnp.cos(lam_im * step * S5_CHUNK), magc * jnp.sin(lam_im * step * S5_CHUNK)

    nr = S5_GROUPS // S5_RANGE
    eye = jnp.eye(S5_RANGE, dtype=f32)

    def b_blocks(bb):
        bb = bb.reshape(nr, S5_RANGE, S5_STATE, S5_GROUP)
        return jnp.einsum('rgph,gk->rghkp', bb, eye).reshape(nr, S5_RANGE * S5_GROUP, S5_RANGE * S5_STATE).astype(BF16)

    def c_blocks(cc):
        cc = cc.astype(f32).reshape(nr, S5_RANGE, S5_GROUP, S5_STATE)
        return jnp.einsum('rghp,gk->rgpkh', cc, eye).reshape(nr, S5_RANGE * S5_STATE, S5_RANGE * S5_GROUP).astype(BF16)

    flat = lambda a: a.reshape(1, S5_FLAT)
    return (b_blocks(bb_re), b_blocks(bb_im), flat(lb_re), flat(lb_im), flat(ac_re), flat(ac_im),
            c_blocks(c_re), c_blocks(c_im))


def _s5_perm():
    p = np.zeros((S5_BLOCK, S5_BLOCK), np.float32)
    for c in range(SUBLANES):
        for s in range(S5_CHUNK):
            p[s * SUBLANES + c, c * S5_CHUNK + s] = 1.0
    return p


def s5_glu(proj, tables, d_skip, glu_w, glu_b):
    bsz, l, _ = proj.shape
    tb = S5_BLOCK
    bre, bim, are, aim, ace, aci, cre, cim = tables
    perm = jnp.asarray(_s5_perm(), BF16)
    permt = jnp.asarray(_s5_perm().T, BF16)
    full = lambda a: pl.BlockSpec(a.shape, lambda b, i: (0,) * a.ndim)
    d2 = d_skip.reshape(1, S5_WIDTH).astype(F32)
    gb2 = glu_b.reshape(1, S5_WIDTH).astype(F32)
    gw = glu_w.astype(BF16)
    return pl.pallas_call(
        _s5_kernel,
        grid=(bsz, l // tb),
        in_specs=[pl.BlockSpec((1, tb, S5_WIDTH), lambda b, i: (b, i, COL_U // S5_WIDTH)),
                  full(perm), full(permt), full(bre), full(bim), full(are), full(aim), full(ace), full(aci),
                  full(cre), full(cim), full(d2), full(gw), full(gb2)],
        out_specs=pl.BlockSpec((1, tb, S5_WIDTH), lambda b, i: (b, i, 0)),
        out_shape=jax.ShapeDtypeStruct((bsz, l, S5_WIDTH), BF16),
        scratch_shapes=[pltpu.VMEM((tb, S5_FLAT), F32), pltpu.VMEM((tb, S5_FLAT), F32),
                        pltpu.VMEM((tb, S5_FLAT), F32), pltpu.VMEM((tb, S5_FLAT), F32),
                        pltpu.VMEM((1, S5_FLAT), F32), pltpu.VMEM((1, S5_FLAT), F32),
                        pltpu.VMEM((SUBLANES, S5_FLAT), F32), pltpu.VMEM((SUBLANES, S5_FLAT), F32)],
        compiler_params=_params("parallel", "arbitrary"),
        name="s5_glu",
    )(proj, perm, permt, bre, bim, are, aim, ace, aci, cre, cim, d2, gw, gb2)


def _merge_kernel(x_ref, oa_ref, ys_ref, ga_ref, gb_ref, g1_ref, wa_ref, ws_ref, wo_ref,
                  ng_ref, sc_ref, sh_ref, x1_ref, h2_ref):
    a = jnp.dot(oa_ref[0], wa_ref[...], preferred_element_type=F32)
    s = jnp.dot(ys_ref[0], ws_ref[...], preferred_element_type=F32)
    mixed = _sigmoid(ga_ref[0].astype(F32)) * a + _sigmoid(gb_ref[0].astype(F32)) * s
    out = jnp.dot(mixed.astype(BF16), wo_ref[...], preferred_element_type=F32)
    x1 = x_ref[0] + g1_ref[0] * out
    x1_ref[0] = x1
    h2_ref[...] = _modulated_norm(x1, ng_ref[...], sc_ref[0], sh_ref[0]).T.astype(BF16)


def merge(x, o_attn, y_ssm, proj, g1, w_up_attn, w_up_ssm, w_out, norm_g, sc2, sh2, tm=256):
    bsz, l, d = x.shape
    tok = lambda w, cb=0: pl.BlockSpec((1, tm, w), lambda b, i: (b, i, cb))
    vec = pl.BlockSpec((1, 1, d), lambda b, i: (b, 0, 0))
    full = lambda a: pl.BlockSpec(a.shape, lambda b, i: (0,) * a.ndim)
    wa, ws, wo = w_up_attn.astype(BF16), w_up_ssm.astype(BF16), w_out.astype(BF16)
    ng = norm_g.reshape(1, d)
    return pl.pallas_call(
        _merge_kernel,
        grid=(bsz, l // tm),
        in_specs=[tok(d), tok(NSA_WIDTH), tok(S5_WIDTH), tok(d, COL_MERGE // d), tok(d, COL_MERGE // d + 1),
                  vec, full(wa), full(ws), full(wo), full(ng), vec, vec],
        out_specs=[tok(d), pl.BlockSpec((d, tm), lambda b, i: (0, b * (l // tm) + i))],
        out_shape=[jax.ShapeDtypeStruct((bsz, l, d), F32), jax.ShapeDtypeStruct((d, bsz * l), BF16)],
        compiler_params=_params("parallel", "parallel"),
        name="merge",
    )(x, o_attn, y_ssm, proj, proj, g1, wa, ws, wo, ng, sc2, sh2)


PEER_SCORE_TM = SUBLANES * LANES
PEER_RANKS = PEER_TOPK + 1
_PAIR_COUNTS = [PEER_RANKS // (a + 1) for a in range(PEER_RANKS)]
_N_PAIRS = sum(_PAIR_COUNTS)


def _rows_to_vregs(x):
    v = [x[:, b * LANES:(b + 1) * LANES] for b in range(SUBLANES)]
    sub = lax.broadcasted_iota(jnp.int32, (SUBLANES, LANES), 0)
    for dist in (4, 2, 1):
        keep = (sub & dist) == 0
        nv = list(v)
        for i in range(SUBLANES):
            if i & dist:
                continue
            lo, hi = v[i], v[i + dist]
            nv[i] = jnp.where(keep, lo, pltpu.roll(hi, dist, axis=0))
            nv[i + dist] = jnp.where(keep, pltpu.roll(lo, SUBLANES - dist, axis=0), hi)
        v = nv
    return v


def _tree_max(x):
    while x.shape[0] > 1:
        n = x.shape[0]
        half = n // 2
        top = jnp.maximum(x[:half], x[half:2 * half])
        x = top if n % 2 == 0 else jnp.concatenate([top, x[2 * half:]], axis=0)
    return x[0]


def _peer_score_kernel(ht_ref, wqt_ref, khi_ref, klo_ref, s_ref, st_ref, q_scr, key_scr, top_scr, cand_scr, best_scr):
    half = PEER_QDIM // 2
    nk = PEER_KEYS
    chunk = 2 * SUBLANES

    def drain(ref, out_ref, n_out):
        n = ref.shape[0]

        def body(r, prev):
            acc = jnp.full((chunk, SUBLANES, LANES), -jnp.inf, F32)
            for c0 in range(0, n, chunk):
                part = ref[c0:min(c0 + chunk, n)]
                part = jnp.where(part < prev[None], part, -jnp.inf)
                if part.shape[0] < chunk:
                    part = jnp.concatenate(
                        [part, jnp.full((chunk - part.shape[0], SUBLANES, LANES), -jnp.inf, F32)], axis=0)
                acc = jnp.maximum(acc, part)
            m = _tree_max(acc)
            out_ref[r] = m
            return m

        lax.fori_loop(0, n_out, body, jnp.full((SUBLANES, LANES), jnp.inf, F32))

    q_scr[...] = jnp.dot(wqt_ref[...], ht_ref[...], preferred_element_type=F32)

    def head(h, carry):
        for c in range(2):
            hc = 2 * h + c
            qt = q_scr[pl.ds(pl.multiple_of(hc * half, half), half), :]
            q_hi = qt.astype(BF16)
            q_lo = (qt - q_hi.astype(F32)).astype(BF16)
            sc = (jnp.dot(khi_ref[h, c], q_hi, preferred_element_type=F32)
                  + jnp.dot(klo_ref[h, c], q_hi, preferred_element_type=F32)
                  + jnp.dot(khi_ref[h, c], q_lo, preferred_element_type=F32))
            s_ref[hc] = sc
            for kg in range(nk // SUBLANES):
                rows = _rows_to_vregs(sc[kg * SUBLANES:(kg + 1) * SUBLANES, :])
                for k in range(SUBLANES):
                    key_scr[c, kg * SUBLANES + k] = rows[k]
            drain(key_scr.at[c], top_scr.at[c], PEER_RANKS)
        i = 0
        for a in range(PEER_RANKS):
            for bb in range(_PAIR_COUNTS[a]):
                cand_scr[i] = top_scr[0, a] + top_scr[1, bb]
                i += 1
        drain(cand_scr, best_scr, PEER_RANKS)
        theta = 0.5 * (best_scr[PEER_TOPK - 1] + best_scr[PEER_TOPK])
        z = jnp.zeros((SUBLANES, LANES), F32)
        for r in range(PEER_TOPK):
            z = z + jnp.exp(best_scr[r] - best_scr[0])
        for row, val in enumerate((theta, top_scr[0, 0], top_scr[1, 0], 1.0 / z)):
            for blk in range(SUBLANES):
                st_ref[h, row:row + 1, blk * LANES:(blk + 1) * LANES] = val[blk:blk + 1, :]
        st_ref[h, 4:, :] = jnp.zeros((SUBLANES - 4, st_ref.shape[2]), F32)
        return carry

    lax.fori_loop(0, PEER_HEADS, head, 0)


def peer_scores(h2t, w_q, sub_keys, tm=PEER_SCORE_TM):
    d, t = h2t.shape
    wqt = w_q.T.astype(BF16)
    k_hi = sub_keys.astype(BF16)
    k_lo = (sub_keys - k_hi.astype(F32)).astype(BF16)
    nk = PEER_KEYS
    return pl.pallas_call(
        _peer_score_kernel,
        grid=(t // tm,),
        in_specs=[pl.BlockSpec((d, tm), lambda i: (0, i)),
                  pl.BlockSpec(wqt.shape, lambda i: (0, 0)),
                  pl.BlockSpec(sub_keys.shape, lambda i: (0, 0, 0, 0)),
                  pl.BlockSpec(sub_keys.shape, lambda i: (0, 0, 0, 0))],
        out_specs=[pl.BlockSpec((2 * PEER_HEADS, nk, tm), lambda i: (0, 0, i)),
                   pl.BlockSpec((PEER_HEADS, SUBLANES, tm), lambda i: (0, 0, i))],
        out_shape=[jax.ShapeDtypeStruct((2 * PEER_HEADS, nk, t), F32),
                   jax.ShapeDtypeStruct((PEER_HEADS, SUBLANES, t), F32)],
        scratch_shapes=[pltpu.VMEM((wqt.shape[0], tm), F32),
                        pltpu.VMEM((2, nk, SUBLANES, LANES), F32),
                        pltpu.VMEM((2, PEER_RANKS, SUBLANES, LANES), F32),
                        pltpu.VMEM((_N_PAIRS, SUBLANES, LANES), F32),
                        pltpu.VMEM((PEER_RANKS, SUBLANES, LANES), F32)],
        compiler_params=_params("parallel"),
        name="peer_scores",
    )(h2t, wqt, k_hi, k_lo)


PEER_TM = 512
PEER_TE = 1024
PEER_MXU_TILE = 256


def _peer_kernel(ht_ref, u_ref, vt_ref, s_ref, st_ref, o_ref, e_scr, thr_scr, e1_scr, at_a, at_b, gw_a, gw_b,
                 *, n_static):
    j = pl.program_id(1)
    nk = PEER_KEYS
    d, tm = ht_ref.shape
    te = u_ref.shape[0]
    rpt = te // nk
    mx = PEER_MXU_TILE

    @pl.when(j == 0)
    def _():
        o_ref[...] = jnp.zeros(o_ref.shape, F32)
        for h in range(PEER_HEADS):
            st = st_ref[h]
            e_scr[2 * h] = jnp.exp(s_ref[2 * h] - st[1:2]) * st[3:4]
            e_scr[2 * h + 1] = jnp.exp(s_ref[2 * h + 1] - st[2:3])

    def step(at_w, at_r, gw_w, gw_r, do1, do2, do3):
        def stage1(k):
            r0, c0 = (k // (tm // mx)) * mx, (k % (tm // mx)) * mx
            at_w[r0:r0 + mx, c0:c0 + mx] = jnp.dot(u_ref[r0:r0 + mx, :], ht_ref[:, c0:c0 + mx],
                                                   preferred_element_type=F32)

        def stage3(k):
            r0, c0 = (k // (tm // mx)) * mx, (k % (tm // mx)) * mx
            o_ref[r0:r0 + mx, c0:c0 + mx] += jnp.dot(vt_ref[0, r0:r0 + mx, :], gw_r[:, c0:c0 + mx],
                                                     preferred_element_type=F32)

        n1 = (te // mx) * (tm // mx) if do1 else 0
        n3 = (d // mx) * (tm // mx) if do3 else 0
        if not do2:
            for k in range(n1):
                stage1(k)
            for k in range(n3):
                stage3(k)
            return

        for h in range(PEER_HEADS):
            for rr in range(rpt):
                row = h * rpt + rr
                r = (j - 1) * rpt + rr
                thr_scr[row:row + 1, :] = st_ref[h, 0:1, :] - s_ref[2 * h, pl.ds(r, 1), :]
                e1_scr[row:row + 1, :] = e_scr[2 * h, pl.ds(r, 1), :]

        kh = nk // 8
        grp = 8
        n2 = (tm // LANES) * (nk // kh) * (rpt // grp)
        done1 = done3 = 0
        blk = 0
        for ch in range(tm // LANES):
            cs = slice(ch * LANES, (ch + 1) * LANES)
            for hf in range(nk // kh):
                ks = slice(hf * kh, (hf + 1) * kh)
                for g0 in range(0, rpt, grp):
                    ws = [jnp.zeros((kh, LANES), F32) for _ in range(grp)]
                    for h in range(PEER_HEADS):
                        s2c = s_ref[2 * h + 1, ks, cs]
                        e2c = e_scr[2 * h + 1, ks, cs]
                        for q in range(grp):
                            row = h * rpt + g0 + q
                            sel = jnp.where(s2c >= thr_scr[row:row + 1, cs], e2c, 0.0)
                            ws[q] = ws[q] + sel * e1_scr[row:row + 1, cs]
                    for q in range(grp):
                        r0 = (g0 + q) * nk + hf * kh
                        a = at_r[r0:r0 + kh, cs]
                        gw_w[r0:r0 + kh, cs] = _gelu(a.astype(BF16)) * ws[q].astype(BF16)
                    blk += 1
                    while done1 * n2 < blk * n1:
                        stage1(done1)
                        done1 += 1
                    while done3 * n2 < blk * n3:
                        stage3(done3)
                        done3 += 1

    bufs = ((at_a, at_b, gw_a, gw_b), (at_b, at_a, gw_b, gw_a))
    steady = (j >= 2) & (j < n_static)
    for jj, flags in ((0, (True, False, False)), (1, (True, True, False)),
                      (n_static, (False, True, True)), (n_static + 1, (False, False, True))):
        pl.when(j == jj)(functools.partial(step, *bufs[jj % 2], *flags))
    for par in range(2):
        pl.when(steady & (j % 2 == par))(functools.partial(step, *bufs[par], True, True, True))


def peer_experts(h2t, u_bf, vt_bf, scores, stats, tm=PEER_TM, te=PEER_TE):
    d, t = h2t.shape
    n_tiles = vt_bf.shape[0]
    return pl.pallas_call(
        functools.partial(_peer_kernel, n_static=n_tiles),
        grid=(t // tm, n_tiles + 2),
        in_specs=[pl.BlockSpec((d, tm), lambda i, j: (0, i)),
                  pl.BlockSpec((te, d), lambda i, j: (jnp.minimum(j, n_tiles - 1), 0)),
                  pl.BlockSpec((1, d, te), lambda i, j: (jnp.clip(j - 2, 0, n_tiles - 1), 0, 0)),
                  pl.BlockSpec((2 * PEER_HEADS, PEER_KEYS, tm), lambda i, j: (0, 0, i)),
                  pl.BlockSpec((PEER_HEADS, SUBLANES, tm), lambda i, j: (0, 0, i))],
        out_specs=pl.BlockSpec((d, tm), lambda i, j: (0, i)),
        out_shape=jax.ShapeDtypeStruct((d, t), F32),
        scratch_shapes=[pltpu.VMEM((2 * PEER_HEADS, PEER_KEYS, tm), F32),
                        pltpu.VMEM((PEER_HEADS * te // PEER_KEYS, tm), F32),
                        pltpu.VMEM((PEER_HEADS * te // PEER_KEYS, tm), F32),
                        pltpu.VMEM((te, tm), F32), pltpu.VMEM((te, tm), F32),
                        pltpu.VMEM((te, tm), BF16), pltpu.VMEM((te, tm), BF16)],
        compiler_params=_params("parallel", "arbitrary"),
        name="peer_experts",
    )(h2t, u_bf, vt_bf, scores, stats)


def _final_kernel(x1_ref, pt_ref, g2_ref, fg_ref, o_ref):
    x2 = x1_ref[...] + g2_ref[0] * pt_ref[...].T
    ms = jnp.mean(x2 * x2, axis=-1, keepdims=True)
    o_ref[...] = (x2 * lax.rsqrt(ms + EPS)) * fg_ref[...]


def final_norm(x1, peer_t, g2, final_g, l, tm=512):
    t, d = x1.shape
    tpb = l // tm
    return pl.pallas_call(
        _final_kernel,
        grid=(t // tm,),
        in_specs=[pl.BlockSpec((tm, d), lambda i: (i, 0)),
                  pl.BlockSpec((d, tm), lambda i: (0, i)),
                  pl.BlockSpec((1, 1, d), lambda i: (i // tpb, 0, 0)),
                  pl.BlockSpec((1, d), lambda i: (0, 0))],
        out_specs=pl.BlockSpec((tm, d), lambda i: (i, 0)),
        out_shape=jax.ShapeDtypeStruct((t, d), F32),
        compiler_params=_params("parallel"),
        name="final_norm",
    )(x1, peer_t, g2, final_g.reshape(1, d))


def kernel(x, c, ada_w, ada_b, norm_mix_g, norm_ffn_g, w_in, cmp_pos, cmp_w1, cmp_b1, cmp_w2, cmp_b2, rel_bias,
           s5_lam_re, s5_lam_im, s5_log_step, s5_b_re, s5_b_im, s5_c_re, s5_c_im, s5_d, glu_w, glu_b,
           w_up_attn, w_up_ssm, w_out, peer_w_q, peer_sub_keys, peer_u, peer_v, final_g):
    bsz, l, d = x.shape
    assert ada_w.shape[0] == 1, "single-layer block"
    mod = adaln(c, ada_w[0], ada_b[0])
    sh1, sc1, g1, sh2, sc2, g2 = [m[:, None, :] for m in jnp.split(mod, 6, axis=-1)]

    proj = inproj(x, norm_mix_g[0], sc1, sh1, build_wcat(w_in[0]))

    n16 = l // CMP_STRIDE

    def blocks16(cols):
        return cols.reshape(bsz, l, NSA_KV_HEADS, HEAD_DIM).transpose(0, 2, 1, 3).reshape(
            bsz, NSA_KV_HEADS, n16, CMP_STRIDE * HEAD_DIM)

    cmp_kv = compress(blocks16(proj[:, :, COL_CMP:COL_CMP + KV_WIDTH]),
                      blocks16(proj[:, :, COL_CMP + KV_WIDTH:COL_CMP + 2 * KV_WIDTH]),
                      cmp_w1[0], cmp_pos[0], cmp_b1[0], cmp_w2[0], cmp_b2[0])
    diag, cmpb = attention_tables(rel_bias)
    o_attn = attention(proj, cmp_kv, diag, cmpb)

    tables = s5_tables(s5_lam_re[0], s5_lam_im[0], s5_log_step[0], s5_b_re[0], s5_b_im[0],
                       s5_c_re[0], s5_c_im[0])
    y_ssm = s5_glu(proj, tables, s5_d[0], glu_w[0], glu_b[0])

    x1, h2t = merge(x, o_attn, y_ssm, proj, g1, w_up_attn[0], w_up_ssm[0], w_out[0],
                   norm_ffn_g[0], sc2, sh2)

    t = bsz * l
    scores, stats = peer_scores(h2t, peer_w_q[0], peer_sub_keys[0])
    v_tiles = peer_v[0].reshape(PEER_EXPERTS // PEER_TE, PEER_TE, d).transpose(0, 2, 1).astype(BF16)
    peer_t = peer_experts(h2t, peer_u[0].astype(BF16), v_tiles, scores, stats)
    out = final_norm(x1.reshape(t, d), peer_t, g2, final_g, l)
    return out.reshape(bsz, l, d)
```

```python
import functools
import math

import numpy as np
import jax
import jax.numpy as jnp
from jax import lax
from jax.experimental import pallas as pl
from jax.experimental.pallas import tpu as pltpu

F32 = jnp.float32
BF16 = jnp.bfloat16
HIGHEST = lax.Precision.HIGHEST

D_MODEL = 2048
NSA_HEADS = 16
NSA_KV_HEADS = 4
NSA_GROUP = NSA_HEADS // NSA_KV_HEADS
HEAD_DIM = 64
NSA_WIDTH = NSA_HEADS * HEAD_DIM
KV_WIDTH = NSA_KV_HEADS * HEAD_DIM
CMP_LEN = 32
CMP_STRIDE = 16
CMP_HIDDEN = 256
SEL_BLOCK = 64
SEL_TOP = 16
WINDOW = 512
S5_WIDTH = 1024
S5_GROUP = 16
S5_GROUPS = S5_WIDTH // S5_GROUP
S5_STATE = 64
S5_FLAT = S5_GROUPS * S5_STATE
REL_BUCKETS = 32
REL_MAX_DIST = 128
PEER_KEYS = 128
PEER_EXPERTS = PEER_KEYS * PEER_KEYS
PEER_HEADS = 8
PEER_TOPK = 16
PEER_QDIM = 256
EPS = 1e-6
NEG_INF = -1e30
FORCE_BONUS = 1e4
LOG2E = math.log2(math.e)
MASK_BIG = 1e30

VMEM_LIMIT_BYTES = 56 * 1024 * 1024
LANES = 128
SUBLANES = 8
MXU_DIM = 256

PROJ_WIDTH = 8192
COL_Q = 0
COL_U = 1024
COL_MERGE = 2048
COL_SEL = 6144
COL_WIN = 6656
COL_CMP = 7168
COL_GATES = 7680

ATT_TILE = 128
BIAS_SPAN = 256
CMP_BIAS_SLOTS = 16
SEL_SHIFT = SEL_BLOCK.bit_length() - 1
ATT_SUPER = 512
ATT_ORIGIN = 2 * ATT_SUPER - ATT_TILE
ATT_TABLE_ROWS = ATT_ORIGIN + WINDOW + ATT_TILE


def _params(*sem):
    return pltpu.CompilerParams(dimension_semantics=sem, vmem_limit_bytes=VMEM_LIMIT_BYTES)


def _gelu(x):
    return 0.5 * x * (1.0 + jnp.tanh(math.sqrt(2.0 / math.pi) * (x + 0.044715 * (x * x * x))))


def _sigmoid(x):
    return 1.0 / (1.0 + jnp.exp(-x))


def _adaln_kernel(c_ref, w_ref, b_ref, o_ref):
    c = c_ref[...]
    a = c * _sigmoid(c)
    o_ref[...] = jnp.dot(a, w_ref[...], precision=HIGHEST, preferred_element_type=F32) + b_ref[...]


def adaln(c, w, b):
    bsz, d = c.shape
    n = w.shape[1]
    tn = 1024
    cp = jnp.zeros((SUBLANES, d), F32).at[:bsz].set(c)
    out = pl.pallas_call(
        _adaln_kernel,
        grid=(n // tn,),
        in_specs=[pl.BlockSpec((SUBLANES, d), lambda j: (0, 0)),
                  pl.BlockSpec((d, tn), lambda j: (0, j)),
                  pl.BlockSpec((1, tn), lambda j: (0, j))],
        out_specs=pl.BlockSpec((SUBLANES, tn), lambda j: (0, j)),
        out_shape=jax.ShapeDtypeStruct((SUBLANES, n), F32),
        compiler_params=_params("parallel"),
        name="adaln",
    )(cp, w, b.reshape(1, n))
    return out[:bsz]


def _modulated_norm(x, g, sc, sh):
    ms = jnp.mean(x * x, axis=-1, keepdims=True)
    return (x * lax.rsqrt(ms + EPS)) * g * (1.0 + sc) + sh


def _inproj_kernel(x_ref, g_ref, sc_ref, sh_ref, w_ref, o_ref, h_scr):
    @pl.when(pl.program_id(2) == 0)
    def _():
        h_scr[...] = _modulated_norm(x_ref[0], g_ref[...], sc_ref[0], sh_ref[0]).astype(BF16)

    o_ref[0] = jnp.dot(h_scr[...], w_ref[...], preferred_element_type=F32).astype(BF16)


def inproj(x, g, sc, sh, wcat, tm=1024, tn=2048):
    bsz, l, d = x.shape
    n = wcat.shape[1]
    tm = min(tm, l)
    return pl.pallas_call(
        _inproj_kernel,
        grid=(bsz, l // tm, n // tn),
        in_specs=[pl.BlockSpec((1, tm, d), lambda b, i, j: (b, i, 0)),
                  pl.BlockSpec((1, d), lambda b, i, j: (0, 0)),
                  pl.BlockSpec((1, 1, d), lambda b, i, j: (b, 0, 0)),
                  pl.BlockSpec((1, 1, d), lambda b, i, j: (b, 0, 0)),
                  pl.BlockSpec((d, tn), lambda b, i, j: (0, j))],
        out_specs=pl.BlockSpec((1, tm, tn), lambda b, i, j: (b, i, j)),
        out_shape=jax.ShapeDtypeStruct((bsz, l, n), BF16),
        scratch_shapes=[pltpu.VMEM((tm, d), BF16)],
        compiler_params=_params("parallel", "parallel", "arbitrary"),
        name="inproj",
    )(x, g.reshape(1, d), sc, sh, wcat)


def build_wcat(w_in):
    d = w_in.shape[0]
    pts = np.cumsum([NSA_WIDTH] + [KV_WIDTH] * 6 + [3 * NSA_HEADS, S5_WIDTH])
    q, kc, vc, ks, vs, kw, vw, gates, u, merge = jnp.split(w_in, pts, axis=1)

    def pair(k, v):
        return jnp.stack([k.reshape(d, NSA_KV_HEADS, HEAD_DIM), v.reshape(d, NSA_KV_HEADS, HEAD_DIM)],
                         axis=2).reshape(d, 2 * KV_WIDTH)

    gates_p = jnp.pad(gates, ((0, 0), (0, LANES - 3 * NSA_HEADS)))
    pad = jnp.zeros((d, PROJ_WIDTH - COL_GATES - LANES), w_in.dtype)
    wcat = jnp.concatenate([q * (HEAD_DIM ** -0.5 * LOG2E), u, merge, pair(ks, vs), pair(kw, vw),
                            kc, vc, gates_p, pad], axis=1)
    return wcat.astype(BF16)


def _compress_kernel(gk_ref, gv_ref, w1_ref, pos_ref, b1_ref, w2_ref, b2_ref, o_ref):
    half = CMP_STRIDE * HEAD_DIM
    nrow = gk_ref.shape[2]
    out = b2_ref[...]
    for j, g_ref in enumerate((gk_ref, gv_ref)):
        g = g_ref[0, 0]
        w1 = w1_ref[j]
        top = jnp.dot(g, w1[:half].astype(BF16), preferred_element_type=F32)
        bot = jnp.dot(g, w1[half:].astype(BF16), preferred_element_type=F32)
        cst = jnp.dot(pos_ref[j], w1, precision=HIGHEST, preferred_element_type=F32)[0:1] + b1_ref[j]
        hid = top + pltpu.roll(bot, nrow - 1, axis=0) + cst
        out = out + jnp.dot(_gelu(hid).astype(BF16), w2_ref[j].astype(BF16), preferred_element_type=F32)
    o_ref[0, 0] = out.astype(BF16)


def compress(gk, gv, w1, pos, b1, w2, b2):
    bsz, hkv, n16, wd = gk.shape
    posf = jnp.zeros((2, SUBLANES, CMP_LEN * HEAD_DIM), F32).at[:, 0].set(pos.reshape(2, -1))
    w2p = jnp.zeros((2, CMP_HIDDEN, 2 * HEAD_DIM), F32)
    w2p = w2p.at[0, :, :HEAD_DIM].set(w2[0]).at[1, :, HEAD_DIM:].set(w2[1])
    b2p = jnp.concatenate([b2[0], b2[1]]).reshape(1, 2 * HEAD_DIM)
    gspec = pl.BlockSpec((1, 1, n16, wd), lambda b, h: (b, h, 0, 0))
    full = lambda a: pl.BlockSpec(a.shape, lambda b, h: (0,) * a.ndim)
    b1r = b1.reshape(2, 1, CMP_HIDDEN)
    return pl.pallas_call(
        _compress_kernel,
        grid=(bsz, hkv),
        in_specs=[gspec, gspec, full(w1), full(posf), full(b1r), full(w2p), full(b2p)],
        out_specs=pl.BlockSpec((1, 1, n16, 2 * HEAD_DIM), lambda b, h: (b, h, 0, 0)),
        out_shape=jax.ShapeDtypeStruct((bsz, hkv, n16, 2 * HEAD_DIM), BF16),
        compiler_params=_params("parallel", "parallel"),
        name="compress",
    )(gk, gv, w1, posf, b1r, w2p, b2p)


def _t5_bucket_table(n):
    dist = np.arange(n)
    max_exact = REL_BUCKETS // 2
    ratio = np.log(np.maximum(dist, 1).astype(np.float32) / np.float32(max_exact)) / np.float32(
        math.log(REL_MAX_DIST / max_exact))
    large = np.minimum(max_exact + (ratio * np.float32(REL_BUCKETS - max_exact)).astype(np.int32), REL_BUCKETS - 1)
    return np.where(dist < max_exact, dist, large)


def _cmp_to_sel_t(n_cmp, n_sel):
    cs = np.arange(n_cmp)[:, None] * CMP_STRIDE
    ss = np.arange(n_sel)[None, :] * SEL_BLOCK
    ov = np.clip(np.minimum(cs + CMP_LEN, ss + SEL_BLOCK) - np.maximum(cs, ss), 0, None)
    return np.ascontiguousarray((ov / CMP_LEN).T.astype(np.float32))


def attention_tables(rel_bias):
    tq = ATT_TILE
    bucket = _t5_bucket_table(BIAS_SPAN)
    assert np.all(bucket[113:] == REL_BUCKETS - 1)
    bv = (rel_bias[bucket] - rel_bias[REL_BUCKETS - 1][None, :]).T * LOG2E
    heads = bv.shape[0]
    band0 = ATT_ORIGIN - BIAS_SPAN
    band = BIAS_SPAN + tq
    period = band + tq
    vext = jnp.concatenate([jnp.zeros((heads, period - BIAS_SPAN), F32), bv.astype(F32)], axis=1)
    toep = jnp.tile(vext, (1, band))[:, :band * (period - 1)].reshape(heads, band, period - 1)[:, :, :tq]
    tab = jnp.pad(toep, ((0, 0), (band0, ATT_TABLE_ROWS - band0 - band), (0, 0)))
    tab = tab.reshape(NSA_KV_HEADS, NSA_GROUP, ATT_TABLE_ROWS, tq).transpose(0, 2, 1, 3)
    tab = tab.reshape(NSA_KV_HEADS, ATT_TABLE_ROWS, NSA_GROUP * tq)
    first = CMP_BIAS_SLOTS // 2 * CMP_STRIDE - CMP_LEN + 1
    lo = CMP_BIAS_SLOTS * CMP_STRIDE - first
    fpad = jnp.pad(bv.astype(F32), ((0, 0), (lo, tq)))
    cmpb = jnp.stack([fpad[:, lo + first - CMP_STRIDE * r:][:, :tq] for r in range(CMP_BIAS_SLOTS)],
                     axis=1)
    cmpb = cmpb.reshape(NSA_KV_HEADS, NSA_GROUP, CMP_BIAS_SLOTS, tq).transpose(0, 2, 1, 3).reshape(
        NSA_KV_HEADS, CMP_BIAS_SLOTS, NSA_GROUP * tq)
    return tab, cmpb


def _importance_taps(n_cmp, n_sel):
    m = _cmp_to_sel_t(n_cmp, n_sel)
    ratio = SEL_BLOCK // CMP_STRIDE
    cols = np.nonzero(m[1])[0]
    taps = [(int(c - ratio), float(m[1, c])) for c in cols]
    for j in range(n_sel):
        want = np.zeros(n_cmp, np.float32)
        for k, w in taps:
            if 0 <= ratio * j + k < n_cmp:
                want[ratio * j + k] = w
        assert np.array_equal(want, m[j])
    return ratio, taps


def _attn_kernel(q_ref, sel_ref, selt_ref, win_ref, wint_ref, cmp_ref, cmpt_ref, gate_ref, tab_ref, cmpb_ref,
                 ind_ref, kmask_ref, o_ref, m_scr, acc_scr, sel_scr, s_scr, p_scr, *, taps, ratio):
    tq = ATT_TILE
    sk = ATT_SUPER
    cols = NSA_GROUP * tq
    kvh = pl.program_id(1)
    qt = pl.program_id(2)
    n_cmp = cmp_ref.shape[2]
    n_sel = sel_scr.shape[0]
    per = sk // tq
    pad = SUBLANES

    qT = q_ref[0]
    zpad = jnp.zeros((HEAD_DIM, tq), BF16)
    qpad = jnp.concatenate(
        [jnp.concatenate([qT[g * HEAD_DIM:(g + 1) * HEAD_DIM], zpad], axis=0) for g in range(NSA_GROUP)],
        axis=1)
    col_i = lax.broadcasted_iota(jnp.int32, (1, cols), 1) & (tq - 1)
    t_col = qt * tq + col_i

    ckv = cmp_ref[0, 0]
    s_scr[0:pad, :] = jnp.zeros((pad, cols), F32)
    s_scr[pad + n_cmp:, :] = jnp.zeros((pad, cols), F32)
    s_scr[pad:pad + n_cmp, :] = jnp.dot(ckv, qpad, preferred_element_type=F32)
    assert CMP_BIAS_SLOTS // 2 == pad
    slot0 = pl.multiple_of(qt * (tq // CMP_STRIDE), SUBLANES)
    s_scr[pl.ds(slot0, CMP_BIAS_SLOTS), :] += cmpb_ref[0]
    s = s_scr[pad:pad + n_cmp, :]
    n_idx = lax.broadcasted_iota(jnp.int32, (n_cmp, cols), 0)
    valid = (n_idx * CMP_STRIDE + (CMP_LEN - 1)) <= t_col
    s = jnp.where(valid, s, NEG_INF)
    mx = jnp.max(s, axis=0, keepdims=True)
    p = jnp.where(valid, jnp.exp2(s - mx), 0.0)
    den = jnp.sum(p, axis=0, keepdims=True)
    p = p * jnp.where(den > 0.0, 1.0 / den, 0.0)
    o_c = jnp.dot(cmpt_ref[0, 0], p.astype(BF16), preferred_element_type=F32)

    p_scr[0:pad, :] = jnp.zeros((pad, tq), F32)
    p_scr[pad:, :] = p[:, 0:tq] + p[:, tq:2 * tq] + p[:, 2 * tq:3 * tq] + p[:, 3 * tq:4 * tq]
    imp = jnp.zeros((n_sel, tq), F32)
    for k, w in taps:
        imp = imp + w * p_scr[pl.ds(pad + k, n_sel, stride=ratio), :]
    blk = lax.broadcasted_iota(jnp.int32, (n_sel, tq), 0)
    qi = lax.broadcasted_iota(jnp.int32, (n_sel, tq), 1)
    cur = jnp.right_shift(qt * tq + qi, SEL_SHIFT)

    @pl.when((qt + 1) * tq <= SEL_TOP * SEL_BLOCK)
    def _():
        sel_scr[...] = jnp.where(blk <= cur, 1.0, 0.0)

    @pl.when((qt + 1) * tq > SEL_TOP * SEL_BLOCK)
    def _():
        forced = (blk == 0) | (blk == cur) | (blk == cur - 1)
        val = jnp.where(blk <= cur, imp + FORCE_BONUS * forced.astype(F32), NEG_INF)
        rank = jnp.zeros((n_sel, tq), F32)
        for k in range(n_sel):
            rk = val[k:k + 1, :]
            beats = (rk > val) | ((rk == val) & (blk > k))
            rank = rank + beats.astype(F32)
        sel_scr[...] = jnp.where(rank < float(SEL_TOP), 1.0, 0.0)

    m_scr[...] = jnp.full(m_scr.shape, NEG_INF, F32)
    acc_scr[...] = jnp.zeros(acc_scr.shape, F32)
    st_last = qt // per
    dq = qt - st_last * per
    blocks = sk // SEL_BLOCK
    q_top = qpad[0:HEAD_DIM]
    mask_rows = max(blocks, 2 * SUBLANES)
    q_fill = jnp.zeros((HEAD_DIM - mask_rows, cols), BF16)

    def sel_scores(st, table_row, causal_shift, live):
        kv = sel_ref[0, pl.ds(pl.multiple_of(st * sk, sk), sk), :]
        kvm = kv * kmask_ref[...] + ind_ref[...]
        picked = sel_scr[pl.ds(pl.multiple_of(st * blocks, blocks), blocks), :]
        if live is not None:
            picked = picked * live
        off = (picked - 1.0) * MASK_BIG
        if blocks < mask_rows:
            off = jnp.concatenate([off, jnp.zeros((mask_rows - blocks, tq), F32)], axis=0)
        off = jnp.concatenate([off] * NSA_GROUP, axis=1).astype(BF16)
        sc = jnp.dot(kvm, jnp.concatenate([q_top, off, q_fill], axis=0), preferred_element_type=F32)
        if table_row is not None:
            sc = sc + tab_ref[0, pl.ds(pl.multiple_of(table_row, tq), sk), :]
        if causal_shift is not None:
            rel = col_i - lax.broadcasted_iota(jnp.int32, (sk, cols), 0)
            sc = jnp.where(rel + causal_shift >= 0, sc, NEG_INF)
        return sc

    def sel_tiles(*tiles):
        scs = [sel_scores(*t) for t in tiles]
        m_old = m_scr[...]
        m_new = m_old
        for sc in scs:
            m_new = jnp.maximum(m_new, jnp.max(sc, axis=0, keepdims=True))
        acc = jnp.exp2(m_old - m_new) * acc_scr[...]
        for (st, _, _, _), sc in zip(tiles, scs):
            pe = jnp.exp2(sc - m_new).astype(BF16)
            acc = acc + jnp.dot(selt_ref[0, 0, st], pe, preferred_element_type=F32)
        acc_scr[...] = acc
        m_scr[...] = m_new

    n_far = jnp.maximum(st_last - 1, 0)

    def pair_body(pr, carry):
        sel_tiles((2 * pr, None, None, None), (2 * pr + 1, None, None, None))
        return carry

    lax.fori_loop(0, n_far // 2, pair_body, 0)

    @pl.when(n_far % 2 == 1)
    def _():
        sel_tiles((n_far - 1, None, None, None))

    has_prev = jnp.where(st_last >= 1, 1.0, 0.0)
    sel_tiles((jnp.maximum(st_last - 1, 0), ATT_ORIGIN - sk - dq * tq, None, has_prev),
              (st_last, ATT_ORIGIN - dq * tq, dq * tq, None))
    o_s = acc_scr[HEAD_DIM:, :] * (1.0 / acc_scr[0:1, :])

    wk = WINDOW + tq
    n_w = wk // tq
    k0 = jnp.maximum(qt - WINDOW // tq, 0)
    dw = (qt - k0) * tq
    kv = win_ref[0, pl.ds(pl.multiple_of(k0 * tq, tq), wk), :]
    sc = jnp.dot(kv, qpad, preferred_element_type=F32)
    sc = sc + tab_ref[0, pl.ds(pl.multiple_of(ATT_ORIGIN - dw, tq), wk), :]
    dist = col_i - lax.broadcasted_iota(jnp.int32, (wk, cols), 0) + dw
    sc = jnp.where((dist >= 0) & (dist < WINDOW), sc, NEG_INF)
    mw = jnp.max(sc, axis=0, keepdims=True)
    pw = jnp.exp2(sc - mw).astype(BF16)
    ow = jnp.dot(wint_ref[0, 0, k0], pw[0:tq], preferred_element_type=F32)
    for m in range(1, n_w):
        ow = ow + jnp.dot(wint_ref[0, 0, k0 + m], pw[m * tq:(m + 1) * tq], preferred_element_type=F32)
    o_w = ow[HEAD_DIM:, :] * (1.0 / ow[0:1, :])

    branches = (o_c[HEAD_DIM:, :], o_s, o_w)
    outs = []
    for g in range(NSA_GROUP):
        acc = jnp.zeros((HEAD_DIM, tq), F32)
        for br in range(3):
            grow = gate_ref[0, pl.ds((kvh * NSA_GROUP + g) * 3 + br, 1), :]
            acc = acc + _sigmoid(grow) * branches[br][:, g * tq:(g + 1) * tq]
        outs.append(acc)
    for pr in range(NSA_GROUP // 2):
        pair = jnp.concatenate([outs[2 * pr], outs[2 * pr + 1]], axis=0)
        o_ref[0, :, pr * LANES:(pr + 1) * LANES] = pair.T.astype(BF16)


def attention(proj, cmp_kv, table, cmpb_t):
    bsz, l, _ = proj.shape
    tq = ATT_TILE
    sk = ATT_SUPER
    assert l % sk == 0 and l >= WINDOW + tq
    n_kt = l // tq
    n_st = l // sk
    n_cmp = cmp_kv.shape[2]
    n_sel = l // SEL_BLOCK
    ratio, taps = _importance_taps(n_cmp, n_sel)
    gw = NSA_GROUP * HEAD_DIM
    kvw = 2 * KV_WIDTH
    q_t = proj[:, :, COL_Q:COL_Q + NSA_WIDTH].transpose(0, 2, 1)

    def tiles_t(c0, nt):
        t = proj[:, :, c0:c0 + kvw].reshape(bsz, nt, l // nt, NSA_KV_HEADS, LANES).transpose(0, 3, 1, 4, 2)
        first = lax.broadcasted_iota(jnp.int32, t.shape, 3) == 0
        return jnp.where(first, jnp.ones((), BF16), t)

    sel_t = tiles_t(COL_SEL, n_st)
    win_t = tiles_t(COL_WIN, n_kt)
    cmp_t = cmp_kv.transpose(0, 1, 3, 2)
    gates_t = proj[:, :, COL_GATES:COL_GATES + LANES].astype(F32).transpose(0, 2, 1)
    key = np.arange(sk)[:, None]
    lane = np.arange(LANES)[None, :]
    ind = jnp.asarray((lane == HEAD_DIM + key // SEL_BLOCK).astype(np.float32), BF16)
    kmask = jnp.asarray(np.broadcast_to(lane < HEAD_DIM, (sk, LANES)).astype(np.float32), BF16)
    const = lambda a: pl.BlockSpec(a.shape, lambda b, h, i: (0,) * a.ndim)
    return pl.pallas_call(
        functools.partial(_attn_kernel, taps=taps, ratio=ratio),
        grid=(bsz, NSA_KV_HEADS, n_kt),
        in_specs=[pl.BlockSpec((1, gw, tq), lambda b, h, i: (b, h, i)),
                  pl.BlockSpec((1, l, LANES), lambda b, h, i: (b, 0, COL_SEL // LANES + h)),
                  pl.BlockSpec((1, 1, n_st, LANES, sk), lambda b, h, i: (b, h, 0, 0, 0)),
                  pl.BlockSpec((1, l, LANES), lambda b, h, i: (b, 0, COL_WIN // LANES + h)),
                  pl.BlockSpec((1, 1, n_kt, LANES, tq), lambda b, h, i: (b, h, 0, 0, 0)),
                  pl.BlockSpec((1, 1, n_cmp, LANES), lambda b, h, i: (b, h, 0, 0)),
                  pl.BlockSpec((1, 1, LANES, n_cmp), lambda b, h, i: (b, h, 0, 0)),
                  pl.BlockSpec((1, LANES, tq), lambda b, h, i: (b, 0, i)),
                  pl.BlockSpec((1, ATT_TABLE_ROWS, NSA_GROUP * tq), lambda b, h, i: (h, 0, 0)),
                  pl.BlockSpec((1, CMP_BIAS_SLOTS, NSA_GROUP * tq), lambda b, h, i: (h, 0, 0)),
                  const(ind), const(kmask)],
        out_specs=pl.BlockSpec((1, tq, gw), lambda b, h, i: (b, i, h)),
        out_shape=jax.ShapeDtypeStruct((bsz, l, NSA_WIDTH), BF16),
        scratch_shapes=[pltpu.VMEM((1, NSA_GROUP * tq), F32),
                        pltpu.VMEM((LANES, NSA_GROUP * tq), F32),
                        pltpu.VMEM((n_sel, tq), F32),
                        pltpu.VMEM((n_cmp + 2 * SUBLANES, NSA_GROUP * tq), F32),
                        pltpu.VMEM((n_cmp + SUBLANES, tq), F32)],
        compiler_params=_params("parallel", "parallel", "arbitrary"),
        name="nsa_attention",
    )(q_t, proj, sel_t, proj, win_t, cmp_kv, cmp_t, gates_t, table, cmpb_t, ind, kmask)


S5_BLOCK = 256
S5_CHUNK = S5_BLOCK // SUBLANES
S5_SLAB = 1024
S5_RANGE = MXU_DIM // S5_GROUP


def _s5_kernel(u_ref, perm_ref, permt_ref, bre_ref, bim_ref, are_ref, aim_ref, ace_ref, aci_ref, cre_ref, cim_ref,
               d_ref, gw_ref, gb_ref, o_ref, bu_re, bu_im, xs_re, xs_im, carry_re, carry_im, init_re, init_im):
    nr = S5_GROUPS // S5_RANGE
    rw = S5_RANGE * S5_GROUP
    sw = S5_RANGE * S5_STATE

    @pl.when(pl.program_id(1) == 0)
    def _():
        carry_re[...] = jnp.zeros(carry_re.shape, F32)
        carry_im[...] = jnp.zeros(carry_im.shape, F32)

    perm = perm_ref[...]
    up = jnp.dot(perm, u_ref[0], preferred_element_type=F32).astype(BF16)
    for r in range(nr):
        ur = up[:, r * rw:(r + 1) * rw]
        bu_re[:, r * sw:(r + 1) * sw] = jnp.dot(ur, bre_ref[r], preferred_element_type=F32)
        bu_im[:, r * sw:(r + 1) * sw] = jnp.dot(ur, bim_ref[r], preferred_element_type=F32)

    nslab = S5_FLAT // S5_SLAB
    row8 = lax.broadcasted_iota(jnp.int32, (SUBLANES, S5_SLAB), 0)
    for sl in range(nslab):
        cs = slice(sl * S5_SLAB, (sl + 1) * S5_SLAB)
        a_re = jnp.broadcast_to(are_ref[:, cs], (SUBLANES, S5_SLAB))
        a_im = jnp.broadcast_to(aim_ref[:, cs], (SUBLANES, S5_SLAB))

        def step1(s, st):
            s_re, s_im = st
            off = pl.multiple_of(s * SUBLANES, SUBLANES)
            n_re = a_re * s_re - a_im * s_im + bu_re[pl.ds(off, SUBLANES), cs]
            n_im = a_re * s_im + a_im * s_re + bu_im[pl.ds(off, SUBLANES), cs]
            return n_re, n_im

        zero = jnp.zeros((SUBLANES, S5_SLAB), F32)
        e_re, e_im = lax.fori_loop(0, S5_CHUNK, step1, (zero, zero))

        ac_re = ace_ref[:, cs]
        ac_im = aci_ref[:, cs]
        x_re = carry_re[:, cs]
        x_im = carry_im[:, cs]
        st_re = jnp.zeros((SUBLANES, S5_SLAB), F32)
        st_im = jnp.zeros((SUBLANES, S5_SLAB), F32)
        for c in range(SUBLANES):
            st_re = jnp.where(row8 == c, x_re, st_re)
            st_im = jnp.where(row8 == c, x_im, st_im)
            n_re = ac_re * x_re - ac_im * x_im + e_re[c:c + 1]
            n_im = ac_re * x_im + ac_im * x_re + e_im[c:c + 1]
            x_re, x_im = n_re, n_im
        carry_re[:, cs] = x_re
        carry_im[:, cs] = x_im
        init_re[:, cs] = st_re
        init_im[:, cs] = st_im

        def step2(s, st):
            s_re, s_im = st
            off = pl.multiple_of(s * SUBLANES, SUBLANES)
            n_re = a_re * s_re - a_im * s_im + bu_re[pl.ds(off, SUBLANES), cs]
            n_im = a_re * s_im + a_im * s_re + bu_im[pl.ds(off, SUBLANES), cs]
            xs_re[pl.ds(off, SUBLANES), cs] = n_re
            xs_im[pl.ds(off, SUBLANES), cs] = n_im
            return n_re, n_im

        lax.fori_loop(0, S5_CHUNK, step2, (st_re, st_im))

    ys = []
    for r in range(nr):
        xr = xs_re[:, r * sw:(r + 1) * sw].astype(BF16)
        xi = xs_im[:, r * sw:(r + 1) * sw].astype(BF16)
        ys.append(jnp.dot(xr, cre_ref[r], preferred_element_type=F32)
                  - jnp.dot(xi, cim_ref[r], preferred_element_type=F32))
    y = jnp.concatenate(ys, axis=1) + d_ref[...] * up.astype(F32)
    y = _gelu(y)
    z = jnp.dot(y.astype(BF16), gw_ref[...], preferred_element_type=F32) + gb_ref[...]
    y = (y * _sigmoid(z)).astype(BF16)
    o_ref[0] = jnp.dot(permt_ref[...], y, preferred_element_type=F32).astype(BF16)


def s5_tables(lam_re, lam_im, log_step, b_re, b_im, c_re, c_im):
    f32 = F32
    lam_re, lam_im = lam_re.astype(f32), lam_im.astype(f32)
    step = jnp.exp(log_step.astype(f32))[:, None]
    mag = jnp.exp(lam_re * step)
    lb_re, lb_im = mag * jnp.cos(lam_im * step), mag * jnp.sin(lam_im * step)
    den = lam_re * lam_re + lam_im * lam_im
    n_re = lb_re - 1.0
    f_re = (n_re * lam_re + lb_im * lam_im) / den
    f_im = (lb_im * lam_re - n_re * lam_im) / den
    b_re, b_im = b_re.astype(f32), b_im.astype(f32)
    bb_re = f_re[..., None] * b_re - f_im[..., None] * b_im
    bb_im = f_re[..., None] * b_im + f_im[..., None] * b_re
    magc = jnp.exp(lam_re * step * S5_CHUNK)
    ac_re, ac_im = magc * jnp.cos(lam_im * step * S5_CHUNK), magc * jnp.sin(lam_im * step * S5_CHUNK)

    nr = S5_GROUPS // S5_RANGE
    eye = jnp.eye(S5_RANGE, dtype=f32)

    def b_blocks(bb):
        bb = bb.reshape(nr, S5_RANGE, S5_STATE, S5_GROUP)
        return jnp.einsum('rgph,gk->rghkp', bb, eye).reshape(nr, S5_RANGE * S5_GROUP, S5_RANGE * S5_STATE).astype(BF16)

    def c_blocks(cc):
        cc = cc.astype(f32).reshape(nr, S5_RANGE, S5_GROUP, S5_STATE)
        return jnp.einsum('rghp,gk->rgpkh', cc, eye).reshape(nr, S5_RANGE * S5_STATE, S5_RANGE * S5_GROUP).astype(BF16)

    flat = lambda a: a.reshape(1, S5_FLAT)
    return (b_blocks(bb_re), b_blocks(bb_im), flat(lb_re), flat(lb_im), flat(ac_re), flat(ac_im),
            c_blocks(c_re), c_blocks(c_im))


def _s5_perm():
    p = np.zeros((S5_BLOCK, S5_BLOCK), np.float32)
    for c in range(SUBLANES):
        for s in range(S5_CHUNK):
            p[s * SUBLANES + c, c * S5_CHUNK + s] = 1.0
    return p


def s5_glu(proj, tables, d_skip, glu_w, glu_b):
    bsz, l, _ = proj.shape
    tb = S5_BLOCK
    bre, bim, are, aim, ace, aci, cre, cim = tables
    perm = jnp.asarray(_s5_perm(), BF16)
    permt = jnp.asarray(_s5_perm().T, BF16)
    full = lambda a: pl.BlockSpec(a.shape, lambda b, i: (0,) * a.ndim)
    d2 = d_skip.reshape(1, S5_WIDTH).astype(F32)
    gb2 = glu_b.reshape(1, S5_WIDTH).astype(F32)
    gw = glu_w.astype(BF16)
    return pl.pallas_call(
        _s5_kernel,
        grid=(bsz, l // tb),
        in_specs=[pl.BlockSpec((1, tb, S5_WIDTH), lambda b, i: (b, i, COL_U // S5_WIDTH)),
                  full(perm), full(permt), full(bre), full(bim), full(are), full(aim), full(ace), full(aci),
                  full(cre), full(cim), full(d2), full(gw), full(gb2)],
        out_specs=pl.BlockSpec((1, tb, S5_WIDTH), lambda b, i: (b, i, 0)),
        out_shape=jax.ShapeDtypeStruct((bsz, l, S5_WIDTH), BF16),
        scratch_shapes=[pltpu.VMEM((tb, S5_FLAT), F32), pltpu.VMEM((tb, S5_FLAT), F32),
                        pltpu.VMEM((tb, S5_FLAT), F32), pltpu.VMEM((tb, S5_FLAT), F32),
                        pltpu.VMEM((1, S5_FLAT), F32), pltpu.VMEM((1, S5_FLAT), F32),
                        pltpu.VMEM((SUBLANES, S5_FLAT), F32), pltpu.VMEM((SUBLANES, S5_FLAT), F32)],
        compiler_params=_params("parallel", "arbitrary"),
        name="s5_glu",
    )(proj, perm, permt, bre, bim, are, aim, ace, aci, cre, cim, d2, gw, gb2)


def _merge_kernel(x_ref, oa_ref, ys_ref, ga_ref, gb_ref, g1_ref, wa_ref, ws_ref, wo_ref,
                  ng_ref, sc_ref, sh_ref, x1_ref, h2_ref):
    a = jnp.dot(oa_ref[0], wa_ref[...], preferred_element_type=F32)
    s = jnp.dot(ys_ref[0], ws_ref[...], preferred_element_type=F32)
    mixed = _sigmoid(ga_ref[0].astype(F32)) * a + _sigmoid(gb_ref[0].astype(F32)) * s
    out = jnp.dot(mixed.astype(BF16), wo_ref[...], preferred_element_type=F32)
    x1 = x_ref[0] + g1_ref[0] * out
    x1_ref[0] = x1
    h2_ref[...] = _modulated_norm(x1, ng_ref[...], sc_ref[0], sh_ref[0]).T.astype(BF16)


def merge(x, o_attn, y_ssm, proj, g1, w_up_attn, w_up_ssm, w_out, norm_g, sc2, sh2, tm=256):
    bsz, l, d = x.shape
    tok = lambda w, cb=0: pl.BlockSpec((1, tm, w), lambda b, i: (b, i, cb))
    vec = pl.BlockSpec((1, 1, d), lambda b, i: (b, 0, 0))
    full = lambda a: pl.BlockSpec(a.shape, lambda b, i: (0,) * a.ndim)
    wa, ws, wo = w_up_attn.astype(BF16), w_up_ssm.astype(BF16), w_out.astype(BF16)
    ng = norm_g.reshape(1, d)
    return pl.pallas_call(
        _merge_kernel,
        grid=(bsz, l // tm),
        in_specs=[tok(d), tok(NSA_WIDTH), tok(S5_WIDTH), tok(d, COL_MERGE // d), tok(d, COL_MERGE // d + 1),
                  vec, full(wa), full(ws), full(wo), full(ng), vec, vec],
        out_specs=[tok(d), pl.BlockSpec((d, tm), lambda b, i: (0, b * (l // tm) + i))],
        out_shape=[jax.ShapeDtypeStruct((bsz, l, d), F32), jax.ShapeDtypeStruct((d, bsz * l), BF16)],
        compiler_params=_params("parallel", "parallel"),
        name="merge",
    )(x, o_attn, y_ssm, proj, proj, g1, wa, ws, wo, ng, sc2, sh2)


PEER_SCORE_TM = SUBLANES * LANES
PEER_RANKS = PEER_TOPK + 1
_PAIR_COUNTS = [PEER_RANKS // (a + 1) for a in range(PEER_RANKS)]
_N_PAIRS = sum(_PAIR_COUNTS)


def _rows_to_vregs(x):
    v = [x[:, b * LANES:(b + 1) * LANES] for b in range(SUBLANES)]
    sub = lax.broadcasted_iota(jnp.int32, (SUBLANES, LANES), 0)
    for dist in (4, 2, 1):
        keep = (sub & dist) == 0
        nv = list(v)
        for i in range(SUBLANES):
            if i & dist:
                continue
            lo, hi = v[i], v[i + dist]
            nv[i] = jnp.where(keep, lo, pltpu.roll(hi, dist, axis=0))
            nv[i + dist] = jnp.where(keep, pltpu.roll(lo, SUBLANES - dist, axis=0), hi)
        v = nv
    return v


def _tree_max(x):
    while x.shape[0] > 1:
        n = x.shape[0]
        half = n // 2
        top = jnp.maximum(x[:half], x[half:2 * half])
        x = top if n % 2 == 0 else jnp.concatenate([top, x[2 * half:]], axis=0)
    return x[0]


def _peer_score_kernel(ht_ref, wqt_ref, khi_ref, klo_ref, s_ref, st_ref, q_scr, key_scr, top_scr, cand_scr, best_scr):
    half = PEER_QDIM // 2
    nk = PEER_KEYS
    chunk = 2 * SUBLANES

    def drain(ref, out_ref, n_out):
        n = ref.shape[0]

        def body(r, prev):
            acc = jnp.full((chunk, SUBLANES, LANES), -jnp.inf, F32)
            for c0 in range(0, n, chunk):
                part = ref[c0:min(c0 + chunk, n)]
                part = jnp.where(part < prev[None], part, -jnp.inf)
                if part.shape[0] < chunk:
                    part = jnp.concatenate(
                        [part, jnp.full((chunk - part.shape[0], SUBLANES, LANES), -jnp.inf, F32)], axis=0)
                acc = jnp.maximum(acc, part)
            m = _tree_max(acc)
            out_ref[r] = m
            return m

        lax.fori_loop(0, n_out, body, jnp.full((SUBLANES, LANES), jnp.inf, F32))

    q_scr[...] = jnp.dot(wqt_ref[...], ht_ref[...], preferred_element_type=F32)

    def head(h, carry):
        for c in range(2):
            hc = 2 * h + c
            qt = q_scr[pl.ds(pl.multiple_of(hc * half, half), half), :]
            q_hi = qt.astype(BF16)
            q_lo = (qt - q_hi.astype(F32)).astype(BF16)
            sc = (jnp.dot(khi_ref[h, c], q_hi, preferred_element_type=F32)
                  + jnp.dot(klo_ref[h, c], q_hi, preferred_element_type=F32)
                  + jnp.dot(khi_ref[h, c], q_lo, preferred_element_type=F32))
            s_ref[hc] = sc
            for kg in range(nk // SUBLANES):
                rows = _rows_to_vregs(sc[kg * SUBLANES:(kg + 1) * SUBLANES, :])
                for k in range(SUBLANES):
                    key_scr[c, kg * SUBLANES + k] = rows[k]
            drain(key_scr.at[c], top_scr.at[c], PEER_RANKS)
        i = 0
        for a in range(PEER_RANKS):
            for bb in range(_PAIR_COUNTS[a]):
                cand_scr[i] = top_scr[0, a] + top_scr[1, bb]
                i += 1
        drain(cand_scr, best_scr, PEER_RANKS)
        theta = 0.5 * (best_scr[PEER_TOPK - 1] + best_scr[PEER_TOPK])
        z = jnp.zeros((SUBLANES, LANES), F32)
        for r in range(PEER_TOPK):
            z = z + jnp.exp(best_scr[r] - best_scr[0])
        for row, val in enumerate((theta, top_scr[0, 0], top_scr[1, 0], 1.0 / z)):
            for blk in range(SUBLANES):
                st_ref[h, row:row + 1, blk * LANES:(blk + 1) * LANES] = val[blk:blk + 1, :]
        st_ref[h, 4:, :] = jnp.zeros((SUBLANES - 4, st_ref.shape[2]), F32)
        return carry

    lax.fori_loop(0, PEER_HEADS, head, 0)


def peer_scores(h2t, w_q, sub_keys, tm=PEER_SCORE_TM):
    d, t = h2t.shape
    wqt = w_q.T.astype(BF16)
    k_hi = sub_keys.astype(BF16)
    k_lo = (sub_keys - k_hi.astype(F32)).astype(BF16)
    nk = PEER_KEYS
    return pl.pallas_call(
        _peer_score_kernel,
        grid=(t // tm,),
        in_specs=[pl.BlockSpec((d, tm), lambda i: (0, i)),
                  pl.BlockSpec(wqt.shape, lambda i: (0, 0)),
                  pl.BlockSpec(sub_keys.shape, lambda i: (0, 0, 0, 0)),
                  pl.BlockSpec(sub_keys.shape, lambda i: (0, 0, 0, 0))],
        out_specs=[pl.BlockSpec((2 * PEER_HEADS, nk, tm), lambda i: (0, 0, i)),
                   pl.BlockSpec((PEER_HEADS, SUBLANES, tm), lambda i: (0, 0, i))],
        out_shape=[jax.ShapeDtypeStruct((2 * PEER_HEADS, nk, t), F32),
                   jax.ShapeDtypeStruct((PEER_HEADS, SUBLANES, t), F32)],
        scratch_shapes=[pltpu.VMEM((wqt.shape[0], tm), F32),
                        pltpu.VMEM((2, nk, SUBLANES, LANES), F32),
                        pltpu.VMEM((2, PEER_RANKS, SUBLANES, LANES), F32),
                        pltpu.VMEM((_N_PAIRS, SUBLANES, LANES), F32),
                        pltpu.VMEM((PEER_RANKS, SUBLANES, LANES), F32)],
        compiler_params=_params("parallel"),
        name="peer_scores",
    )(h2t, wqt, k_hi, k_lo)


PEER_TM = 512
PEER_TE = 1024
PEER_MXU_TILE = 256


def _peer_kernel(ht_ref, u_ref, vt_ref, s_ref, st_ref, o_ref, e_scr, thr_scr, e1_scr, at_a, at_b, gw_a, gw_b,
                 *, n_static):
    j = pl.program_id(1)
    nk = PEER_KEYS
    d, tm = ht_ref.shape
    te = u_ref.shape[0]
    rpt = te // nk
    mx = PEER_MXU_TILE

    @pl.when(j == 0)
    def _():
        o_ref[...] = jnp.zeros(o_ref.shape, F32)
        for h in range(PEER_HEADS):
            st = st_ref[h]
            e_scr[2 * h] = jnp.exp(s_ref[2 * h] - st[1:2]) * st[3:4]
            e_scr[2 * h + 1] = jnp.exp(s_ref[2 * h + 1] - st[2:3])

    def step(at_w, at_r, gw_w, gw_r, do1, do2, do3):
        def stage1(k):
            r0, c0 = (k // (tm // mx)) * mx, (k % (tm // mx)) * mx
            at_w[r0:r0 + mx, c0:c0 + mx] = jnp.dot(u_ref[r0:r0 + mx, :], ht_ref[:, c0:c0 + mx],
                                                   preferred_element_type=F32)

        def stage3(k):
            r0, c0 = (k // (tm // mx)) * mx, (k % (tm // mx)) * mx
            o_ref[r0:r0 + mx, c0:c0 + mx] += jnp.dot(vt_ref[0, r0:r0 + mx, :], gw_r[:, c0:c0 + mx],
                                                     preferred_element_type=F32)

        n1 = (te // mx) * (tm // mx) if do1 else 0
        n3 = (d // mx) * (tm // mx) if do3 else 0
        if not do2:
            for k in range(n1):
                stage1(k)
            for k in range(n3):
                stage3(k)
            return

        for h in range(PEER_HEADS):
            for rr in range(rpt):
                row = h * rpt + rr
                r = (j - 1) * rpt + rr
                thr_scr[row:row + 1, :] = st_ref[h, 0:1, :] - s_ref[2 * h, pl.ds(r, 1), :]
                e1_scr[row:row + 1, :] = e_scr[2 * h, pl.ds(r, 1), :]

        kh = nk // 8
        grp = 8
        n2 = (tm // LANES) * (nk // kh) * (rpt // grp)
        done1 = done3 = 0
        blk = 0
        for ch in range(tm // LANES):
            cs = slice(ch * LANES, (ch + 1) * LANES)
            for hf in range(nk // kh):
                ks = slice(hf * kh, (hf + 1) * kh)
                for g0 in range(0, rpt, grp):
                    ws = [jnp.zeros((kh, LANES), F32) for _ in range(grp)]
                    for h in range(PEER_HEADS):
                        s2c = s_ref[2 * h + 1, ks, cs]
                        e2c = e_scr[2 * h + 1, ks, cs]
                        for q in range(grp):
                            row = h * rpt + g0 + q
                            sel = jnp.where(s2c >= thr_scr[row:row + 1, cs], e2c, 0.0)
                            ws[q] = ws[q] + sel * e1_scr[row:row + 1, cs]
                    for q in range(grp):
                        r0 = (g0 + q) * nk + hf * kh
                        a = at_r[r0:r0 + kh, cs]
                        gw_w[r0:r0 + kh, cs] = _gelu(a.astype(BF16)) * ws[q].astype(BF16)
                    blk += 1
                    while done1 * n2 < blk * n1:
                        stage1(done1)
                        done1 += 1
                    while done3 * n2 < blk * n3:
                        stage3(done3)
                        done3 += 1

    bufs = ((at_a, at_b, gw_a, gw_b), (at_b, at_a, gw_b, gw_a))
    steady = (j >= 2) & (j < n_static)
    for jj, flags in ((0, (True, False, False)), (1, (True, True, False)),
                      (n_static, (False, True, True)), (n_static + 1, (False, False, True))):
        pl.when(j == jj)(functools.partial(step, *bufs[jj % 2], *flags))
    for par in range(2):
        pl.when(steady & (j % 2 == par))(functools.partial(step, *bufs[par], True, True, True))


def peer_experts(h2t, u_bf, vt_bf, scores, stats, tm=PEER_TM, te=PEER_TE):
    d, t = h2t.shape
    n_tiles = vt_bf.shape[0]
    return pl.pallas_call(
        functools.partial(_peer_kernel, n_static=n_tiles),
        grid=(t // tm, n_tiles + 2),
        in_specs=[pl.BlockSpec((d, tm), lambda i, j: (0, i)),
                  pl.BlockSpec((te, d), lambda i, j: (jnp.minimum(j, n_tiles - 1), 0)),
                  pl.BlockSpec((1, d, te), lambda i, j: (jnp.clip(j - 2, 0, n_tiles - 1), 0, 0)),
                  pl.BlockSpec((2 * PEER_HEADS, PEER_KEYS, tm), lambda i, j: (0, 0, i)),
                  pl.BlockSpec((PEER_HEADS, SUBLANES, tm), lambda i, j: (0, 0, i))],
        out_specs=pl.BlockSpec((d, tm), lambda i, j: (0, i)),
        out_shape=jax.ShapeDtypeStruct((d, t), F32),
        scratch_shapes=[pltpu.VMEM((2 * PEER_HEADS, PEER_KEYS, tm), F32),
                        pltpu.VMEM((PEER_HEADS * te // PEER_KEYS, tm), F32),
                        pltpu.VMEM((PEER_HEADS * te // PEER_KEYS, tm), F32),
                        pltpu.VMEM((te, tm), F32), pltpu.VMEM((te, tm), F32),
                        pltpu.VMEM((te, tm), BF16), pltpu.VMEM((te, tm), BF16)],
        compiler_params=_params("parallel", "arbitrary"),
        name="peer_experts",
    )(h2t, u_bf, vt_bf, scores, stats)


def _final_kernel(x1_ref, pt_ref, g2_ref, fg_ref, o_ref):
    x2 = x1_ref[...] + g2_ref[0] * pt_ref[...].T
    ms = jnp.mean(x2 * x2, axis=-1, keepdims=True)
    o_ref[...] = (x2 * lax.rsqrt(ms + EPS)) * fg_ref[...]


def final_norm(x1, peer_t, g2, final_g, l, tm=512):
    t, d = x1.shape
    tpb = l // tm
    return pl.pallas_call(
        _final_kernel,
        grid=(t // tm,),
        in_specs=[pl.BlockSpec((tm, d), lambda i: (i, 0)),
                  pl.BlockSpec((d, tm), lambda i: (0, i)),
                  pl.BlockSpec((1, 1, d), lambda i: (i // tpb, 0, 0)),
                  pl.BlockSpec((1, d), lambda i: (0, 0))],
        out_specs=pl.BlockSpec((tm, d), lambda i: (i, 0)),
        out_shape=jax.ShapeDtypeStruct((t, d), F32),
        compiler_params=_params("parallel"),
        name="final_norm",
    )(x1, peer_t, g2, final_g.reshape(1, d))


def kernel(x, c, ada_w, ada_b, norm_mix_g, norm_ffn_g, w_in, cmp_pos, cmp_w1, cmp_b1, cmp_w2, cmp_b2, rel_bias,
           s5_lam_re, s5_lam_im, s5_log_step, s5_b_re, s5_b_im, s5_c_re, s5_c_im, s5_d, glu_w, glu_b,
           w_up_attn, w_up_ssm, w_out, peer_w_q, peer_sub_keys, peer_u, peer_v, final_g):
    bsz, l, d = x.shape
    assert ada_w.shape[0] == 1, "single-layer block"
    mod = adaln(c, ada_w[0], ada_b[0])
    sh1, sc1, g1, sh2, sc2, g2 = [m[:, None, :] for m in jnp.split(mod, 6, axis=-1)]

    proj = inproj(x, norm_mix_g[0], sc1, sh1, build_wcat(w_in[0]))

    n16 = l // CMP_STRIDE

    def blocks16(cols):
        return cols.reshape(bsz, l, NSA_KV_HEADS, HEAD_DIM).transpose(0, 2, 1, 3).reshape(
            bsz, NSA_KV_HEADS, n16, CMP_STRIDE * HEAD_DIM)

    cmp_kv = compress(blocks16(proj[:, :, COL_CMP:COL_CMP + KV_WIDTH]),
                      blocks16(proj[:, :, COL_CMP + KV_WIDTH:COL_CMP + 2 * KV_WIDTH]),
                      cmp_w1[0], cmp_pos[0], cmp_b1[0], cmp_w2[0], cmp_b2[0])
    diag, cmpb = attention_tables(rel_bias)
    o_attn = attention(proj, cmp_kv, diag, cmpb)

    tables = s5_tables(s5_lam_re[0], s5_lam_im[0], s5_log_step[0], s5_b_re[0], s5_b_im[0],
                       s5_c_re[0], s5_c_im[0])
    y_ssm = s5_glu(proj, tables, s5_d[0], glu_w[0], glu_b[0])

    x1, h2t = merge(x, o_attn, y_ssm, proj, g1, w_up_attn[0], w_up_ssm[0], w_out[0],
                   norm_ffn_g[0], sc2, sh2)

    t = bsz * l
    scores, stats = peer_scores(h2t, peer_w_q[0], peer_sub_keys[0])
    v_tiles = peer_v[0].reshape(PEER_EXPERTS // PEER_TE, PEER_TE, d).transpose(0, 2, 1).astype(BF16)
    peer_t = peer_experts(h2t, peer_u[0].astype(BF16), v_tiles, scores, stats)
    out = final_norm(x1.reshape(t, d), peer_t, g2, final_g, l)
    return out.reshape(bsz, l, d)
```
